```python
import jax, jax.numpy as jnp
from jax import lax
import numpy as np

D_MODEL = 2048
BATCH = 4
SEQ = 8192
DEPTH = 1
DEC_BATCH = 8
DEC_SEQ = 32
PAST_LEN = 1024

CHUNK = 64
Q_BLOCK = 128
EPS = 1e-6
ROPE_THETA = 10000.0
N_HEADS_A = 16
N_KV_A = 4
HEAD_DIM_A = 128
KV_GROUP = N_HEADS_A // N_KV_A
N_HEADS_IDX = 16
HEAD_DIM_IDX = 64
TOPK_MAX = 256
N_HEADS_R = 8
KEY_DIM_R = 128
VAL_DIM_R = 256
WIDTH_A = N_HEADS_A * HEAD_DIM_A
WIDTH_B = N_HEADS_R * VAL_DIM_R
D_FF = ((8 * D_MODEL + 3 * 256 - 1) // (3 * 256)) * 256
PROJ_WIDTHS = (WIDTH_A, N_KV_A * HEAD_DIM_A, N_KV_A * HEAD_DIM_A,
               N_HEADS_IDX * HEAD_DIM_IDX, HEAD_DIM_IDX, N_HEADS_IDX,
               N_HEADS_R * KEY_DIM_R, N_HEADS_R * KEY_DIM_R, WIDTH_B, WIDTH_B,
               D_MODEL, D_MODEL)

kernel_name = 'hybrid_dsa_retention_stream_step'


def rms_norm(x, g):
    xf = x.astype(jnp.float32)
    y = xf * lax.rsqrt(jnp.mean(xf * xf, axis=-1, keepdims=True) + EPS)
    return (y * g.astype(jnp.float32)).astype(x.dtype)


def rope(x, pos):
    half = x.shape[-1] // 2
    inv_freq = ROPE_THETA ** (-jnp.arange(half, dtype=jnp.float32) / half)
    ang = pos.astype(jnp.float32)[:, None] * inv_freq[None, :]
    cos = jnp.cos(ang)[None, :, None, :]
    sin = jnp.sin(ang)[None, :, None, :]
    xf = x.astype(jnp.float32)
    x1, x2 = xf[..., :half], xf[..., half:]
    return jnp.concatenate([x1 * cos - x2 * sin, x2 * cos + x1 * sin], axis=-1).astype(x.dtype)


def retention_log_decay():
    return jnp.log1p(-(2.0 ** (-5.0 - jnp.arange(N_HEADS_R, dtype=jnp.float32))))


def split_projection(h, w_in, pos):
    B, T, _ = h.shape
    points = [int(p) for p in np.cumsum(PROJ_WIDTHS)[:-1]]
    qa, ka, va, qi, ki, wi, qr, kr, vr, gr, ga, gb = jnp.split(h @ w_in, points, axis=-1)
    qa = rope(qa.reshape(B, T, N_HEADS_A, HEAD_DIM_A), pos)
    ka = rope(ka.reshape(B, T, N_KV_A, HEAD_DIM_A), pos)
    va = va.reshape(B, T, N_KV_A, HEAD_DIM_A)
    qi = rope(qi.reshape(B, T, N_HEADS_IDX, HEAD_DIM_IDX), pos)
    ki = rope(ki[:, :, None, :], pos)[:, :, 0, :]
    wi = wi * (N_HEADS_IDX ** -0.5)
    qr = rope(qr.reshape(B, T, N_HEADS_R, KEY_DIM_R), pos)
    kr = rope(kr.reshape(B, T, N_HEADS_R, KEY_DIM_R), pos) * (KEY_DIM_R ** -0.5)
    vr = vr.reshape(B, T, N_HEADS_R, VAL_DIM_R)
    return qa, ka, va, qi, ki, wi, qr, kr, vr, gr, ga, gb


def dsa_block(qa, qi, wi, q_pos, ka, va, ki, k_pos, top_k):
    B, Tq = qa.shape[:2]
    logits = jnp.einsum('bqjd,bsd->bqjs', qi.astype(jnp.float32), ki.astype(jnp.float32)) * (HEAD_DIM_IDX ** -0.5)
    score = jnp.einsum('bqjs,bqj->bqs', jax.nn.relu(logits), wi.astype(jnp.float32))
    admissible = (k_pos[None, :] // CHUNK) <= (q_pos[:, None] // CHUNK)
    score = jnp.where(admissible[None], score, -jnp.inf)
    top_val, top_idx = lax.top_k(score, top_k)
    valid = jnp.isfinite(top_val)
    k_sel = jax.vmap(lambda kb, ib: kb[ib])(ka, top_idx)
    v_sel = jax.vmap(lambda vb, ib: vb[ib])(va, top_idx)
    qg = qa.reshape(B, Tq, N_KV_A, KV_GROUP, HEAD_DIM_A).astype(jnp.float32)
    s = jnp.einsum('bqhgd,bqnhd->bqhgn', qg, k_sel.astype(jnp.float32)) * (HEAD_DIM_A ** -0.5)
    s = jnp.where(valid[:, :, None, None, :], s, -jnp.inf)
    p = jax.nn.softmax(s, axis=-1)
    o = jnp.einsum('bqhgn,bqnhd->bqhgd', p, v_sel.astype(jnp.float32))
    return o.reshape(B, Tq, WIDTH_A).astype(qa.dtype)


def dsa_prompt(qa, qi, wi, ka, va, ki, pos, top_k):
    B, T = qa.shape[:2]
    nb = T // Q_BLOCK

    def blocks(a):
        return a.reshape(B, nb, Q_BLOCK, *a.shape[2:]).swapaxes(0, 1)

    def one_block(args):
        qa_b, qi_b, wi_b, pos_b = args
        return dsa_block(qa_b, qi_b, wi_b, pos_b, ka, va, ki, pos, top_k)

    o = lax.map(one_block, (blocks(qa), blocks(qi), blocks(wi), pos.reshape(nb, Q_BLOCK)))
    return o.swapaxes(0, 1).reshape(B, T, WIDTH_A)


def retention_chunk(state, q, k, v, log_gamma):
    C = q.shape[1]
    idx = jnp.arange(C, dtype=jnp.float32)
    diff = idx[:, None] - idx[None, :]
    decay = jnp.where(diff[None] >= 0,
                      jnp.exp(jnp.maximum(diff, 0.0)[None] * log_gamma[:, None, None]), 0.0)
    inner = jnp.einsum('bihd,bjhd->bhij', q, k) * decay[None]
    o = jnp.einsum('bhij,bjhe->bihe', inner, v)
    cross_decay = jnp.exp((idx + 1.0)[:, None] * log_gamma[None, :])
    o = o + jnp.einsum('bihd,bhde->bihe', q, state) * cross_decay[None, :, :, None]
    k_decay = jnp.exp((C - 1.0 - idx)[:, None] * log_gamma[None, :])
    new_state = (state * jnp.exp(C * log_gamma)[None, :, None, None]
                 + jnp.einsum('bjhd,bjhe->bhde', k * k_decay[None, :, :, None], v))
    return o, new_state


def retention_prompt(q, k, v):
    B, T = q.shape[:2]
    nc = T // CHUNK
    log_gamma = retention_log_decay()

    def chunks(a):
        return a.astype(jnp.float32).reshape(B, nc, CHUNK, *a.shape[2:]).swapaxes(0, 1)

    def step(state, qkv):
        o, s = retention_chunk(state, qkv[0], qkv[1], qkv[2], log_gamma)
        return s, o

    state0 = jnp.zeros((B, N_HEADS_R, KEY_DIM_R, VAL_DIM_R), jnp.float32)
    final, o = lax.scan(step, state0, (chunks(q), chunks(k), chunks(v)))
    return o.swapaxes(0, 1).reshape(B, T, N_HEADS_R, VAL_DIM_R), final


def merge_and_ffn(x, o_a, o_b, gr, ga, gb, w_pa, w_pb, w_o, n2, wg, wu, wd):
    B, T, _ = x.shape
    o_b = o_b * lax.rsqrt(jnp.mean(o_b * o_b, axis=-1, keepdims=True) + EPS)
    o_b = o_b.reshape(B, T, WIDTH_B).astype(x.dtype) * jax.nn.silu(gr)
    merged = jax.nn.sigmoid(ga) * (o_a @ w_pa) + jax.nn.sigmoid(gb) * (o_b @ w_pb)
    x = x + merged @ w_o
    h2 = rms_norm(x, n2)
    return x + (jax.nn.silu(h2 @ wg) * (h2 @ wu)) @ wd


def setup_inputs(seed: int = 0) -> dict:
    key = jax.random.key(seed)
    ks = jax.random.split(key, 18)
    f32 = jnp.float32
    p_total = sum(PROJ_WIDTHS)

    def nrm(k, shape, scale):
        return jax.random.normal(k, shape, f32) * scale

    return {
        'x_prompt': nrm(ks[0], (BATCH, SEQ, D_MODEL), 1.0),
        'x_sample': nrm(ks[1], (DEC_BATCH, DEC_SEQ, D_MODEL), 1.0),
        'cache_k': nrm(ks[2], (DEPTH, DEC_BATCH, PAST_LEN, N_KV_A, HEAD_DIM_A), 1.0),
        'cache_v': nrm(ks[3], (DEPTH, DEC_BATCH, PAST_LEN, N_KV_A, HEAD_DIM_A), 1.0),
        'cache_idx_k': nrm(ks[4], (DEPTH, DEC_BATCH, PAST_LEN, HEAD_DIM_IDX), 1.0),
        'state_ret': nrm(ks[5], (DEPTH, DEC_BATCH, N_HEADS_R, KEY_DIM_R, VAL_DIM_R), 0.5),
        'norm1_g': 1.0 + nrm(ks[6], (DEPTH, D_MODEL), 0.01),
        'w_in': nrm(ks[7], (DEPTH, D_MODEL, p_total), D_MODEL ** -0.5),
        'w_pa': nrm(ks[8], (DEPTH, WIDTH_A, D_MODEL), WIDTH_A ** -0.5),
        'w_pb': nrm(ks[9], (DEPTH, WIDTH_B, D_MODEL), WIDTH_B ** -0.5),
        'w_o': nrm(ks[10], (DEPTH, D_MODEL, D_MODEL), D_MODEL ** -0.5),
        'norm2_g': 1.0 + nrm(ks[11], (DEPTH, D_MODEL), 0.01),
        'w_ffn_gate': nrm(ks[12], (DEPTH, D_MODEL, D_FF), D_MODEL ** -0.5),
        'w_ffn_up': nrm(ks[13], (DEPTH, D_MODEL, D_FF), D_MODEL ** -0.5),
        'w_ffn_down': nrm(ks[14], (DEPTH, D_FF, D_MODEL), D_FF ** -0.5),
        'norm_f_g': 1.0 + nrm(ks[15], (D_MODEL,), 0.01),
    }


def reference(x_prompt, x_sample, cache_k, cache_v, cache_idx_k, state_ret, norm1_g, w_in, w_pa, w_pb,
              w_o, norm2_g, w_ffn_gate, w_ffn_up, w_ffn_down, norm_f_g):
    t_p = x_prompt.shape[1]
    t_s = x_sample.shape[1]
    past = cache_k.shape[2]
    pos_p = jnp.arange(t_p, dtype=jnp.int32)
    pos_s = past + jnp.arange(t_s, dtype=jnp.int32)
    key_pos_s = jnp.arange(past + t_s, dtype=jnp.int32)
    top_k_p = min(TOPK_MAX, t_p // 4)
    top_k_s = min(TOPK_MAX, (past + t_s) // 4)
    log_gamma = retention_log_decay()

    xp, xs = x_prompt, x_sample
    kp, vp, ip, sp, ksl, vsl, isl, ssl = [], [], [], [], [], [], [], []
    for l in range(DEPTH):
        qa, ka, va, qi, ki, wi, qr, kr, vr, gr, ga, gb = split_projection(rms_norm(xp, norm1_g[l]), w_in[l], pos_p)
        o_a = dsa_prompt(qa, qi, wi, ka, va, ki, pos_p, top_k_p)
        o_b, st_p = retention_prompt(qr, kr, vr)
        xp = merge_and_ffn(xp, o_a, o_b, gr, ga, gb, w_pa[l], w_pb[l], w_o[l], norm2_g[l],
                           w_ffn_gate[l], w_ffn_up[l], w_ffn_down[l])
        kp.append(ka)
        vp.append(va)
        ip.append(ki)
        sp.append(st_p.astype(x_prompt.dtype))

        qa, ka, va, qi, ki, wi, qr, kr, vr, gr, ga, gb = split_projection(rms_norm(xs, norm1_g[l]), w_in[l], pos_s)
        k_all = jnp.concatenate([cache_k[l].astype(ka.dtype), ka], axis=1)
        v_all = jnp.concatenate([cache_v[l].astype(va.dtype), va], axis=1)
        i_all = jnp.concatenate([cache_idx_k[l].astype(ki.dtype), ki], axis=1)
        o_a = dsa_block(qa, qi, wi, pos_s, k_all, v_all, i_all, key_pos_s, top_k_s)
        o_b, st_s = retention_chunk(state_ret[l].astype(jnp.float32), qr.astype(jnp.float32),
                                    kr.astype(jnp.float32), vr.astype(jnp.float32), log_gamma)
        xs = merge_and_ffn(xs, o_a, o_b, gr, ga, gb, w_pa[l], w_pb[l], w_o[l], norm2_g[l],
                           w_ffn_gate[l], w_ffn_up[l], w_ffn_down[l])
        ksl.append(ka)
        vsl.append(va)
        isl.append(ki)
        ssl.append(st_s.astype(state_ret.dtype))

    y_prompt = rms_norm(xp, norm_f_g)
    y_sample = rms_norm(xs, norm_f_g)
    return (y_prompt, y_sample, jnp.stack(kp), jnp.stack(vp), jnp.stack(ip), jnp.stack(sp),
            jnp.stack(ksl), jnp.stack(vsl), jnp.stack(isl), jnp.stack(ssl))
```

```python
import functools
import math

import jax
import jax.numpy as jnp
from jax import lax
from jax.experimental import pallas as pl
from jax.experimental.pallas import tpu as pltpu

F32 = jnp.float32
BF16 = jnp.bfloat16

CHUNK = 64
EPS = 1e-6
ROPE_THETA = 10000.0
N_HEADS_A = 16
N_KV_A = 4
HEAD_DIM_A = 128
N_HEADS_IDX = 16
HEAD_DIM_IDX = 64
TOPK_MAX = 256
N_HEADS_R = 8
KEY_DIM_R = 128
VAL_DIM_R = 256

LANES = 128
V7X_VMEM_BYTES = 64 * 1024 * 1024
VMEM_LIMIT = V7X_VMEM_BYTES - 8 * 1024 * 1024
MAX_BISECT_ITERS = 64
NEG_BIG = -1e30


def _cparams(*sem):
    return pltpu.CompilerParams(dimension_semantics=sem, vmem_limit_bytes=VMEM_LIMIT)


def _dot(a, b):
    return jnp.dot(a, b, preferred_element_type=F32)


def _dot_nt(a, b):
    return lax.dot_general(a, b, (((1,), (1,)), ((), ())), preferred_element_type=F32)


def _rmsnorm_rows(x, g):
    return x * lax.rsqrt(jnp.mean(x * x, axis=-1, keepdims=True) + EPS) * g


def _rmsnorm_kernel(x_ref, g_ref, o_ref):
    o_ref[...] = _rmsnorm_rows(x_ref[...], g_ref[...]).astype(o_ref.dtype)


def _rmsnorm(x, g, tm):
    n, d = x.shape
    return pl.pallas_call(
        _rmsnorm_kernel,
        grid=(n // tm,),
        in_specs=[pl.BlockSpec((tm, d), lambda i: (i, 0)),
                  pl.BlockSpec((1, d), lambda i: (0, 0))],
        out_specs=pl.BlockSpec((tm, d), lambda i: (i, 0)),
        out_shape=jax.ShapeDtypeStruct((n, d), BF16),
        compiler_params=_cparams("parallel"),
        name="rmsnorm",
    )(x, g.reshape(1, d))


def _rope_tables(pos):
    posf = pos.astype(F32)[:, None]

    def cs(half):
        inv_freq = ROPE_THETA ** (-jnp.arange(half, dtype=F32) / half)
        ang = posf * inv_freq[None, :]
        return jnp.cos(ang), jnp.sin(ang)

    c, s = cs(HEAD_DIM_A // 2)
    c128 = jnp.concatenate([c, c], axis=1)
    s128 = jnp.concatenate([-s, s], axis=1)
    c, s = cs(HEAD_DIM_IDX // 2)
    z = jnp.zeros_like(s)
    c64 = jnp.tile(c, (1, 4))
    a64 = jnp.tile(jnp.concatenate([-s, z], axis=1), (1, 2))
    b64 = jnp.tile(jnp.concatenate([z, s], axis=1), (1, 2))
    return jnp.stack([c128, s128, c64, a64, b64])


def _rope128(x, rope_ref):
    return x * rope_ref[0] + pltpu.roll(x, 64, 1) * rope_ref[1]


def _rope64(x, rope_ref):
    return (x * rope_ref[2] + pltpu.roll(x, 96, 1) * rope_ref[3]
            + pltpu.roll(x, 32, 1) * rope_ref[4])


def _proj_main_kernel(h_ref, w_ref, rope_ref, o_ref, acc_ref, *, groups, tn):
    j = pl.program_id(1)
    acc_ref[...] = _dot(h_ref[...], w_ref[...])
    for lo, hi, mode, scale in groups:
        @pl.when(jnp.logical_and(j >= lo, j < hi))
        def _(mode=mode, scale=scale):
            for c in range(tn // LANES):
                sl = slice(c * LANES, (c + 1) * LANES)
                x = acc_ref[:, sl]
                if mode == "rope128":
                    y = _rope128(x, rope_ref)
                elif mode == "rope64":
                    y = _rope64(x, rope_ref)
                elif mode == "sigmoid":
                    y = 1.0 / (1.0 + jnp.exp(-x))
                elif mode == "silu":
                    y = x / (1.0 + jnp.exp(-x))
                else:
                    y = x
                if scale != 1.0:
                    y = y * scale
                o_ref[:, sl] = y.astype(o_ref.dtype)


def _proj_main(h, w_main, rope, groups, tm, tn):
    n, d = h.shape
    p = w_main.shape[1]
    n_pos_tiles = rope.shape[1] // tm
    return pl.pallas_call(
        functools.partial(_proj_main_kernel, groups=groups, tn=tn),
        grid=(n // tm, p // tn),
        in_specs=[pl.BlockSpec((tm, d), lambda i, j: (i, 0)),
                  pl.BlockSpec((d, tn), lambda i, j: (0, j)),
                  pl.BlockSpec((5, tm, LANES), lambda i, j: (0, i % n_pos_tiles, 0))],
        out_specs=pl.BlockSpec((tm, tn), lambda i, j: (i, j)),
        out_shape=jax.ShapeDtypeStruct((n, p), BF16),
        scratch_shapes=[pltpu.VMEM((tm, tn), F32)],
        compiler_params=_cparams("parallel", "arbitrary"),
        name="proj_main",
    )(h, w_main, rope)


def _proj_kv_kernel(h_ref, w_ref, rope_ref, ka_ref, va_ref, ki_ref, wt_ref,
                    kab_ref, vtb_ref, kia_ref, kib_ref, acc_ref, *, nk, idx_scale):
    acc_ref[...] = _dot(h_ref[...], w_ref[...])
    for c in range(nk // LANES):
        sl = slice(c * LANES, (c + 1) * LANES)
        y = _rope128(acc_ref[:, sl], rope_ref)
        ka_ref[:, sl] = y
        kab_ref[:, sl] = y.astype(BF16)
    v = acc_ref[:, nk:2 * nk]
    va_ref[...] = v
    vtb_ref[...] = v.T.astype(BF16)
    z = acc_ref[:, 2 * nk:2 * nk + LANES]
    y = _rope64(z, rope_ref)
    ki_ref[...] = y[:, :HEAD_DIM_IDX]
    lane = lax.broadcasted_iota(jnp.int32, y.shape, 1)
    ya = jnp.where(lane < HEAD_DIM_IDX, y, 0.0)
    kia_ref[...] = ya.astype(BF16)
    kib_ref[...] = pltpu.roll(ya, HEAD_DIM_IDX, 1).astype(BF16)
    wt_ref[...] = (z * idx_scale).T[HEAD_DIM_IDX:HEAD_DIM_IDX + N_HEADS_IDX, :]


def _proj_kv(h, w_kv, rope, tm):
    n, d = h.shape
    nk = N_KV_A * HEAD_DIM_A
    pw = w_kv.shape[1]
    n_pos_tiles = rope.shape[1] // tm
    idx_scale = (HEAD_DIM_IDX ** -0.5) * (N_HEADS_IDX ** -0.5)
    row = lambda i: (i, 0)
    return pl.pallas_call(
        functools.partial(_proj_kv_kernel, nk=nk, idx_scale=idx_scale),
        grid=(n // tm,),
        in_specs=[pl.BlockSpec((tm, d), row),
                  pl.BlockSpec((d, pw), lambda i: (0, 0)),
                  pl.BlockSpec((5, tm, LANES), lambda i: (0, i % n_pos_tiles, 0))],
        out_specs=[pl.BlockSpec((tm, nk), row), pl.BlockSpec((tm, nk), row),
                   pl.BlockSpec((tm, HEAD_DIM_IDX), row),
                   pl.BlockSpec((N_HEADS_IDX, tm), lambda i: (0, i)),
                   pl.BlockSpec((tm, nk), row),
                   pl.BlockSpec((nk, tm), lambda i: (0, i)),
                   pl.BlockSpec((tm, LANES), row), pl.BlockSpec((tm, LANES), row)],
        out_shape=[jax.ShapeDtypeStruct((n, nk), F32), jax.ShapeDtypeStruct((n, nk), F32),
                   jax.ShapeDtypeStruct((n, HEAD_DIM_IDX), F32),
                   jax.ShapeDtypeStruct((N_HEADS_IDX, n), F32),
                   jax.ShapeDtypeStruct((n, nk), BF16),
                   jax.ShapeDtypeStruct((nk, n), BF16),
                   jax.ShapeDtypeStruct((n, LANES), BF16), jax.ShapeDtypeStruct((n, LANES), BF16)],
        scratch_shapes=[pltpu.VMEM((tm, pw), F32)],
        compiler_params=_cparams("parallel"),
        name="proj_kv",
    )(h, w_kv, rope)


def _index_kernel(qi_ref, wt_ref, kia_ref, kib_ref, m_ref, s_ref, *,
                  tq, ck, n_ck_total, l_valid, pos_base, causal, top_k):
    i = pl.program_id(1)
    if causal:
        n_c = (i * tq + tq + ck - 1) // ck
    else:
        n_c = n_ck_total
    qpos = pos_base + i * tq + lax.broadcasted_iota(jnp.int32, (1, tq), 1)
    lim = jnp.minimum(qpos - lax.rem(qpos, CHUNK) + CHUNK, l_valid)
    limf = lim.astype(F32)
    kprime = jnp.minimum(float(top_k), limf)
    inf = jnp.float32(jnp.inf)

    def key_index(off):
        return off + lax.broadcasted_iota(jnp.int32, (ck, tq), 0)

    def score_chunk(c, carry):
        rmax, rmin = carry
        off = pl.multiple_of(c * ck, ck)
        ka = kia_ref[pl.ds(off, ck), :]
        kb = kib_ref[pl.ds(off, ck), :]
        acc = jnp.zeros((ck, tq), F32)
        for p in range(N_HEADS_IDX // 2):
            qp = qi_ref[:, p * LANES:(p + 1) * LANES]
            acc += jnp.maximum(_dot_nt(ka, qp), 0.0) * wt_ref[2 * p:2 * p + 1, :]
            acc += jnp.maximum(_dot_nt(kb, qp), 0.0) * wt_ref[2 * p + 1:2 * p + 2, :]
        adm = key_index(off) < lim
        s_ref[pl.ds(off, ck), :] = jnp.where(adm, acc, -inf)
        rmax = jnp.maximum(rmax, jnp.max(jnp.where(adm, acc, -inf), axis=0, keepdims=True))
        rmin = jnp.minimum(rmin, jnp.min(jnp.where(adm, acc, inf), axis=0, keepdims=True))
        return rmax, rmin

    rmax, rmin = lax.fori_loop(0, n_c, score_chunk,
                               (jnp.full((1, tq), -inf, F32), jnp.full((1, tq), inf, F32)))

    def fold(ind):
        return ind.reshape(ck // 8, 8, tq).sum(axis=0)

    def count_ge(th):
        def body(c, acc):
            off = pl.multiple_of(c * ck, ck)
            return acc + fold(jnp.where(s_ref[pl.ds(off, ck), :] >= th, 1.0, 0.0))
        acc = lax.fori_loop(0, n_c, body, jnp.zeros((8, tq), F32))
        return acc.sum(axis=0, keepdims=True)

    def cond(st):
        it, lo, hi, cnt, stuck = st
        active = jnp.where(jnp.logical_and(cnt > kprime, stuck < 0.5), 1.0, 0.0)
        return jnp.logical_and(it < MAX_BISECT_ITERS, jnp.max(active) > 0.5)

    def body(st):
        it, lo, hi, cnt, stuck = st
        mid = jnp.where(hi == inf, rmax, lo + 0.5 * (hi - lo))
        c = count_ge(mid)
        ge = c >= kprime
        no_progress = jnp.logical_or(mid <= lo, mid >= hi)
        return (it + 1, jnp.where(ge, mid, lo), jnp.where(ge, hi, mid), jnp.where(ge, c, cnt),
                jnp.where(no_progress, 1.0, stuck))

    _, lo, hi, cnt, _ = lax.while_loop(
        cond, body, (jnp.int32(0), rmin, jnp.full((1, tq), inf, F32), limf, jnp.zeros((1, tq), F32)))

    unresolved = jnp.max(jnp.where(cnt > kprime, 1.0, 0.0)) > 0.5

    @pl.when(jnp.logical_not(unresolved))
    def _():
        def body(c, _):
            off = pl.multiple_of(c * ck, ck)
            m_ref[pl.ds(off, ck), :] = jnp.where(s_ref[pl.ds(off, ck), :] >= lo, 1.0, 0.0).astype(m_ref.dtype)
            return 0
        lax.fori_loop(0, n_c, body, 0)

    @pl.when(unresolved)
    def _():
        need = kprime - count_ge(hi)

        def in_tie(blk):
            return jnp.logical_and(blk >= lo, blk < hi)

        def count_tie_below(jcut):
            def body(c, acc):
                off = pl.multiple_of(c * ck, ck)
                blk = s_ref[pl.ds(off, ck), :]
                e = jnp.logical_and(in_tie(blk), key_index(off).astype(F32) < jcut)
                return acc + fold(jnp.where(e, 1.0, 0.0))
            acc = lax.fori_loop(0, n_c, body, jnp.zeros((8, tq), F32))
            return acc.sum(axis=0, keepdims=True)

        def jbody(_, st):
            jlo, jhi = st
            mid = jnp.floor(0.5 * (jlo + jhi))
            ok = count_tie_below(mid) >= need
            return jnp.where(ok, jlo, mid + 1.0), jnp.where(ok, mid, jhi)

        n_total = n_ck_total * ck
        _, jcut = lax.fori_loop(0, int(math.ceil(math.log2(n_total + 1))), jbody,
                                (jnp.zeros((1, tq), F32), jnp.full((1, tq), float(n_total), F32)))

        def body(c, _):
            off = pl.multiple_of(c * ck, ck)
            blk = s_ref[pl.ds(off, ck), :]
            e = jnp.logical_and(in_tie(blk), key_index(off).astype(F32) < jcut)
            keep = jnp.logical_or(blk >= hi, e)
            m_ref[pl.ds(off, ck), :] = jnp.where(keep, 1.0, 0.0).astype(m_ref.dtype)
            return 0
        lax.fori_loop(0, n_c, body, 0)

    def zero_body(c, _):
        off = pl.multiple_of(c * ck, ck)
        m_ref[pl.ds(off, ck), :] = jnp.zeros((ck, tq), m_ref.dtype)
        return 0
    lax.fori_loop(n_c, n_ck_total, zero_body, 0)


def _index_mask(p_main, qi_col_block, wt, kia, kib, *, tq, ck, l_valid, pos_base, causal, top_k):
    b, lp, _ = kia.shape
    nq = p_main.shape[0] // (b * tq)
    qw = N_HEADS_IDX * HEAD_DIM_IDX
    return pl.pallas_call(
        functools.partial(_index_kernel, tq=tq, ck=ck, n_ck_total=lp // ck, l_valid=l_valid,
                          pos_base=pos_base, causal=causal, top_k=top_k),
        grid=(b, nq),
        in_specs=[pl.BlockSpec((tq, qw), lambda bi, i: (bi * nq + i, qi_col_block)),
                  pl.BlockSpec((N_HEADS_IDX, tq), lambda bi, i: (0, bi * nq + i)),
                  pl.BlockSpec((None, lp, LANES), lambda bi, i: (bi, 0, 0)),
                  pl.BlockSpec((None, lp, LANES), lambda bi, i: (bi, 0, 0))],
        out_specs=pl.BlockSpec((None, lp, tq), lambda bi, i: (bi, 0, i)),
        out_shape=jax.ShapeDtypeStruct((b, lp, nq * tq), BF16),
        scratch_shapes=[pltpu.VMEM((lp, tq), F32)],
        compiler_params=_cparams("parallel", "parallel"),
        name="index_mask",
    )(p_main, wt, kia, kib)


def _attn_kernel(q_ref, k_ref, vt_ref, m_ref, o_ref, qs_ref, acc_ref, *, tq, tk, n_kt_total, causal, group):
    i = pl.program_id(1)
    if causal:
        n_kt = (i * tq + tq + tk - 1) // tk
    else:
        n_kt = n_kt_total
    for h in range(group):
        qs_ref[h * tq:(h + 1) * tq, :] = q_ref[:, h * HEAD_DIM_A:(h + 1) * HEAD_DIM_A]
    acc_ref[...] = jnp.zeros_like(acc_ref)

    def body(kt, carry):
        m, l = carry
        off = pl.multiple_of(kt * tk, tk)
        s = _dot_nt(k_ref[pl.ds(off, tk), :], qs_ref[...])
        sel = m_ref[pl.ds(off, tk), :].astype(F32)
        sel = jnp.concatenate([sel] * group, axis=1) > 0.5
        s = jnp.where(sel, s, -jnp.inf)
        m_new = jnp.maximum(m, jnp.max(s, axis=0, keepdims=True))
        alpha = jnp.exp(m - m_new)
        p = jnp.exp(s - m_new)
        l = alpha * l + jnp.sum(p, axis=0, keepdims=True)
        acc_ref[...] = acc_ref[...] * alpha + _dot(vt_ref[:, pl.ds(off, tk)], p.astype(BF16))
        return m_new, l

    _, l = lax.fori_loop(0, n_kt, body,
                         (jnp.full((1, group * tq), NEG_BIG, F32), jnp.zeros((1, group * tq), F32)))
    o = acc_ref[...] / l
    for h in range(group):
        o_ref[:, h * HEAD_DIM_A:(h + 1) * HEAD_DIM_A] = o[:, h * tq:(h + 1) * tq].T.astype(o_ref.dtype)


def _attention(p_main, k_bf, vt_bf, mask_t, *, tq, tk, causal):
    b, lp, _ = k_bf.shape
    nq = mask_t.shape[2] // tq
    group = N_HEADS_A // N_KV_A
    gw = group * HEAD_DIM_A
    return pl.pallas_call(
        functools.partial(_attn_kernel, tq=tq, tk=tk, n_kt_total=lp // tk, causal=causal, group=group),
        grid=(b, nq, N_KV_A),
        in_specs=[pl.BlockSpec((tq, gw), lambda bi, i, g: (bi * nq + i, g)),
                  pl.BlockSpec((None, lp, HEAD_DIM_A), lambda bi, i, g: (bi, 0, g)),
                  pl.BlockSpec((HEAD_DIM_A, lp), lambda bi, i, g: (g, bi)),
                  pl.BlockSpec((None, lp, tq), lambda bi, i, g: (bi, 0, i))],
        out_specs=pl.BlockSpec((tq, gw), lambda bi, i, g: (bi * nq + i, g)),
        out_shape=jax.ShapeDtypeStruct((b * nq * tq, N_HEADS_A * HEAD_DIM_A), BF16),
        scratch_shapes=[pltpu.VMEM((group * tq, HEAD_DIM_A), BF16),
                        pltpu.VMEM((HEAD_DIM_A, group * tq), F32)],
        compiler_params=_cparams("parallel", "parallel", "arbitrary"),
        name="attention",
    )(p_main, k_bf, vt_bf, mask_t)


def _retention_tables(c):
    log_gamma = jnp.log1p(-(2.0 ** (-5.0 - jnp.arange(N_HEADS_R, dtype=F32))))
    idx = jnp.arange(c, dtype=F32)
    diff = idx[:, None] - idx[None, :]
    decay = jnp.where(diff[None] >= 0,
                      jnp.exp(jnp.maximum(diff, 0.0)[None] * log_gamma[:, None, None]), 0.0)
    cross = jnp.exp((idx + 1.0)[None, :] * log_gamma[:, None])
    kdec = jnp.exp((c - 1.0 - idx)[None, :] * log_gamma[:, None])
    full = jnp.exp(c * log_gamma)
    bc = lambda a, w: jnp.broadcast_to(a[..., None], a.shape + (w,))
    return decay, bc(cross, KEY_DIM_R), bc(kdec, KEY_DIM_R), bc(full[:, None], VAL_DIM_R)


def _retention_kernel(q_ref, k_ref, v_ref, g_ref, dec_ref, qsc_ref, ksc_ref, gc_ref, s0_ref,
                      o_ref, st_ref):
    @pl.when(pl.program_id(1) == 0)
    def _():
        st_ref[...] = s0_ref[...]

    for h in range(N_HEADS_R):
        ks = slice(h * KEY_DIM_R, (h + 1) * KEY_DIM_R)
        vs = slice(h * VAL_DIM_R, (h + 1) * VAL_DIM_R)
        q = q_ref[:, ks]
        k = k_ref[:, ks]
        v = v_ref[:, vs]
        st = st_ref[h]
        inner = _dot_nt(q, k) * dec_ref[h]
        qd = (q.astype(F32) * qsc_ref[h]).astype(BF16)
        o = _dot(inner.astype(BF16), v) + _dot(qd, st.astype(BF16))
        kdt = (k.astype(F32) * ksc_ref[h]).T.astype(BF16)
        st_ref[h] = st * gc_ref[h] + _dot(kdt, v)
        o = o * lax.rsqrt(jnp.mean(o * o, axis=-1, keepdims=True) + EPS)
        o_ref[:, vs] = (o * g_ref[:, vs].astype(F32)).astype(o_ref.dtype)


def _retention(p_main, cols, state0, c):
    b = state0.shape[0]
    nc = p_main.shape[0] // (b * c)
    kw = N_HEADS_R * KEY_DIM_R
    vw = N_HEADS_R * VAL_DIM_R
    dec, qsc, ksc, gcs = _retention_tables(c)
    qb, kb, vb, gb = cols
    const3 = lambda bi, ci: (0, 0, 0)
    st_spec = pl.BlockSpec((None, N_HEADS_R, KEY_DIM_R, VAL_DIM_R), lambda bi, ci: (bi, 0, 0, 0))
    return pl.pallas_call(
        _retention_kernel,
        grid=(b, nc),
        in_specs=[pl.BlockSpec((c, kw), lambda bi, ci: (bi * nc + ci, qb)),
                  pl.BlockSpec((c, kw), lambda bi, ci: (bi * nc + ci, kb)),
                  pl.BlockSpec((c, vw), lambda bi, ci: (bi * nc + ci, vb)),
                  pl.BlockSpec((c, vw), lambda bi, ci: (bi * nc + ci, gb)),
                  pl.BlockSpec(dec.shape, const3), pl.BlockSpec(qsc.shape, const3),
                  pl.BlockSpec(ksc.shape, const3), pl.BlockSpec(gcs.shape, const3),
                  st_spec],
        out_specs=[pl.BlockSpec((c, vw), lambda bi, ci: (bi * nc + ci, 0)), st_spec],
        out_shape=[jax.ShapeDtypeStruct((b * nc * c, vw), BF16),
                   jax.ShapeDtypeStruct(state0.shape, F32)],
        compiler_params=_cparams("parallel", "arbitrary"),
        name="retention",
    )(p_main, p_main, p_main, p_main, dec, qsc, ksc, gcs, state0)


def _merge_kernel(oa_ref, ob_ref, wa_ref, wb_ref, sa_ref, sb_ref, o_ref):
    m = (_dot(oa_ref[...], wa_ref[...]) * sa_ref[...].astype(F32)
         + _dot(ob_ref[...], wb_ref[...]) * sb_ref[...].astype(F32))
    o_ref[...] = m.astype(o_ref.dtype)


def _merge(o_a, o_b, w_pa, w_pb, p_main, ga_off, gb_off, tm, tn):
    n, wa = o_a.shape
    wb = o_b.shape[1]
    d = w_pa.shape[1]
    return pl.pallas_call(
        _merge_kernel,
        grid=(n // tm, d // tn),
        in_specs=[pl.BlockSpec((tm, wa), lambda i, j: (i, 0)),
                  pl.BlockSpec((tm, wb), lambda i, j: (i, 0)),
                  pl.BlockSpec((wa, tn), lambda i, j: (0, j)),
                  pl.BlockSpec((wb, tn), lambda i, j: (0, j)),
                  pl.BlockSpec((tm, tn), lambda i, j: (i, ga_off // tn + j)),
                  pl.BlockSpec((tm, tn), lambda i, j: (i, gb_off // tn + j))],
        out_specs=pl.BlockSpec((tm, tn), lambda i, j: (i, j)),
        out_shape=jax.ShapeDtypeStruct((n, d), BF16),
        compiler_params=_cparams("parallel", "arbitrary"),
        name="merge",
    )(o_a, o_b, w_pa, w_pb, p_main, p_main)


def _out_proj_kernel(x_ref, m_ref, w_ref, g_ref, x1_ref, h2_ref):
    x1 = x_ref[...] + _dot(m_ref[...], w_ref[...])
    x1_ref[...] = x1
    h2_ref[...] = _rmsnorm_rows(x1, g_ref[...]).astype(h2_ref.dtype)


def _out_proj(x, merged, w_o, g2, tm):
    n, d = x.shape
    row = lambda i: (i, 0)
    return pl.pallas_call(
        _out_proj_kernel,
        grid=(n // tm,),
        in_specs=[pl.BlockSpec((tm, d), row), pl.BlockSpec((tm, d), row),
                  pl.BlockSpec((d, d), lambda i: (0, 0)), pl.BlockSpec((1, d), lambda i: (0, 0))],
        out_specs=[pl.BlockSpec((tm, d), row), pl.BlockSpec((tm, d), row)],
        out_shape=[jax.ShapeDtypeStruct((n, d), F32), jax.ShapeDtypeStruct((n, d), BF16)],
        compiler_params=_cparams("parallel"),
        name="out_proj",
    )(x, merged, w_o, g2.reshape(1, d))


def _ffn_up_kernel(h_ref, wg_ref, wu_ref, o_ref):
    h = h_ref[...]
    a = _dot(h, wg_ref[...])
    u = _dot(h, wu_ref[...])
    o_ref[...] = (a / (1.0 + jnp.exp(-a)) * u).astype(o_ref.dtype)


def _ffn_up(h2, wg, wu, tm, tn):
    n, d = h2.shape
    f = wg.shape[1]
    return pl.pallas_call(
        _ffn_up_kernel,
        grid=(n // tm, f // tn),
        in_specs=[pl.BlockSpec((tm, d), lambda i, j: (i, 0)),
                  pl.BlockSpec((d, tn), lambda i, j: (0, j)),
                  pl.BlockSpec((d, tn), lambda i, j: (0, j))],
        out_specs=pl.BlockSpec((tm, tn), lambda i, j: (i, j)),
        out_shape=jax.ShapeDtypeStruct((n, f), BF16),
        compiler_params=_cparams("parallel", "arbitrary"),
        name="ffn_up",
    )(h2, wg, wu)


def _ffn_down_kernel(u_ref, w_ref, x1_ref, g_ref, y_ref, acc_ref, *, final_norm):
    k = pl.program_id(1)

    @pl.when(k == 0)
    def _():
        acc_ref[...] = x1_ref[...]

    acc_ref[...] += _dot(u_ref[...], w_ref[...])

    @pl.when(k == pl.num_programs(1) - 1)
    def _():
        y = acc_ref[...]
        if final_norm:
            y = _rmsnorm_rows(y, g_ref[...])
        y_ref[...] = y


def _ffn_down(u, wd, x1, gf, tm, tk, final_norm):
    n, f = u.shape
    d = wd.shape[1]
    return pl.pallas_call(
        functools.partial(_ffn_down_kernel, final_norm=final_norm),
        grid=(n // tm, f // tk),
        in_specs=[pl.BlockSpec((tm, tk), lambda i, k: (i, k)),
                  pl.BlockSpec((tk, d), lambda i, k: (k, 0)),
                  pl.BlockSpec((tm, d), lambda i, k: (i, 0)),
                  pl.BlockSpec((1, d), lambda i, k: (0, 0))],
        out_specs=pl.BlockSpec((tm, d), lambda i, k: (i, 0)),
        out_shape=jax.ShapeDtypeStruct((n, d), F32),
        scratch_shapes=[pltpu.VMEM((tm, d), F32)],
        compiler_params=_cparams("parallel", "arbitrary"),
        name="ffn_down",
    )(u, wd, x1, gf.reshape(1, d))


class _MainLayout:
    def __init__(self, d_model):
        wa = N_HEADS_A * HEAD_DIM_A
        wb = N_HEADS_R * VAL_DIM_R
        kr = N_HEADS_R * KEY_DIM_R
        qi = N_HEADS_IDX * HEAD_DIM_IDX
        order = [("qa", wa, "rope128", HEAD_DIM_A ** -0.5), ("vr", wb, "plain", 1.0),
                 ("gr", wb, "silu", 1.0), ("ga", d_model, "sigmoid", 1.0), ("gb", d_model, "sigmoid", 1.0),
                 ("qi", qi, "rope64", 1.0), ("qr", kr, "rope128", 1.0),
                 ("kr", kr, "rope128", KEY_DIM_R ** -0.5)]
        self.off, self.width = {}, {}
        self.order = order
        o = 0
        for name, w, _, _ in order:
            self.off[name], self.width[name] = o, w
            o += w
        self.total = o

    def groups(self, tn):
        out = []
        for name, w, mode, scale in self.order:
            assert self.off[name] % tn == 0 and w % tn == 0
            out.append((self.off[name] // tn, (self.off[name] + w) // tn, mode, scale))
        return tuple(out)

    def block(self, name, width=None):
        width = width or self.width[name]
        assert self.off[name] % width == 0
        return self.off[name] // width


def _split_w_in(w, d_model):
    wa = N_HEADS_A * HEAD_DIM_A
    nk = N_KV_A * HEAD_DIM_A
    wb = N_HEADS_R * VAL_DIM_R
    kr = N_HEADS_R * KEY_DIM_R
    names = ("qa", "ka", "va", "qi", "ki", "wi", "qr", "kr", "vr", "gr", "ga", "gb")
    widths = (wa, nk, nk, N_HEADS_IDX * HEAD_DIM_IDX, HEAD_DIM_IDX, N_HEADS_IDX, kr, kr, wb, wb,
              d_model, d_model)
    parts, o = {}, 0
    for name, wd in zip(names, widths):
        parts[name] = w[:, o:o + wd]
        o += wd
    assert o == w.shape[1]
    return parts


def _tile(n, pref):
    t = min(n, pref)
    assert n % t == 0
    return t


def _dense_pre(x2, pos_rows, lw, lay):
    n = x2.shape[0]
    tm = _tile(n, 1024)
    rope = _rope_tables(pos_rows)
    h = _rmsnorm(x2, lw["g1"], tm)
    tn = 1024
    p_main = _proj_main(h, lw["w_main"], rope, lay.groups(tn), tm, tn)
    kv = _proj_kv(h, lw["w_kv"], rope, _tile(n, 512))
    return p_main, kv


def _dense_post(x2, o_a, o_b, p_main, lw, lay, gf, final_norm):
    n = x2.shape[0]
    tm = _tile(n, 1024)
    merged = _merge(o_a, o_b, lw["w_pa"], lw["w_pb"], p_main, lay.off["ga"], lay.off["gb"], tm, 512)
    x1, h2 = _out_proj(x2, merged, lw["w_o"], lw["g2"], _tile(n, 512))
    u = _ffn_up(h2, lw["wg"], lw["wu"], tm, 512)
    return _ffn_down(u, lw["wd"], x1, gf, _tile(n, 512), 512, final_norm)


def _prompt_layer(xp, lw, lay, gf, final_norm):
    b, t, d = xp.shape
    x2 = xp.reshape(b * t, d)
    p_main, (ka, va, ki, wt, ka_bf, vt_bf, kia, kib) = _dense_pre(x2, jnp.arange(t, dtype=jnp.int32), lw, lay)
    top_k = min(TOPK_MAX, t // 4)
    tq, ck = _tile(t, 256), _tile(t, 512)
    mask_t = _index_mask(p_main, lay.block("qi"), wt, kia.reshape(b, t, LANES), kib.reshape(b, t, LANES),
                         tq=tq, ck=ck, l_valid=t, pos_base=0, causal=True, top_k=top_k)
    nk = N_KV_A * HEAD_DIM_A
    o_a = _attention(p_main, ka_bf.reshape(b, t, nk), vt_bf, mask_t, tq=tq, tk=ck, causal=True)
    state0 = jnp.zeros((b, N_HEADS_R, KEY_DIM_R, VAL_DIM_R), F32)
    cols = (lay.block("qr"), lay.block("kr"), lay.block("vr"), lay.block("gr"))
    o_b, st = _retention(p_main, cols, state0, _tile(t, 256))
    y = _dense_post(x2, o_a, o_b, p_main, lw, lay, gf, final_norm)
    return (y.reshape(b, t, d), ka.reshape(b, t, N_KV_A, HEAD_DIM_A), va.reshape(b, t, N_KV_A, HEAD_DIM_A),
            ki.reshape(b, t, HEAD_DIM_IDX), st)


def _sample_layer(xs, cache_k, cache_v, cache_i, state, lw, lay, gf, final_norm):
    b, t, d = xs.shape
    past = cache_k.shape[1]
    nk = N_KV_A * HEAD_DIM_A
    x2 = xs.reshape(b * t, d)
    pos = jnp.tile(past + jnp.arange(t, dtype=jnp.int32), b)
    p_main, (ka, va, ki, wt, ka_bf, vt_bf, kia, kib) = _dense_pre(x2, pos, lw, lay)

    tq = LANES
    l_valid = past + t
    tk = 3 * LANES
    lp = -(-l_valid // tk) * tk
    pq = jnp.pad(p_main.reshape(b, t, -1), ((0, 0), (0, tq - t), (0, 0))).reshape(b * tq, -1)
    wtq = jnp.pad(wt.reshape(N_HEADS_IDX, b, t), ((0, 0), (0, 0), (0, tq - t))).reshape(N_HEADS_IDX, b * tq)
    kpad = ((0, 0), (0, lp - l_valid), (0, 0))
    ci = cache_i.astype(BF16)
    zi = jnp.zeros_like(ci)
    kia_all = jnp.pad(jnp.concatenate([jnp.concatenate([ci, zi], -1), kia.reshape(b, t, LANES)], 1), kpad)
    kib_all = jnp.pad(jnp.concatenate([jnp.concatenate([zi, ci], -1), kib.reshape(b, t, LANES)], 1), kpad)
    k_all = jnp.pad(jnp.concatenate([cache_k.reshape(b, past, nk).astype(BF16), ka_bf.reshape(b, t, nk)], 1), kpad)
    vt_new = vt_bf.reshape(nk, b, t).transpose(1, 0, 2)
    vt_all = jnp.concatenate([cache_v.reshape(b, past, nk).transpose(0, 2, 1).astype(BF16), vt_new], 2)
    vt_all = jnp.pad(vt_all, ((0, 0), (0, 0), (0, lp - l_valid))).transpose(1, 0, 2).reshape(nk, b * lp)

    top_k = min(TOPK_MAX, l_valid // 4)
    mask_t = _index_mask(pq, lay.block("qi"), wtq, kia_all, kib_all, tq=tq, ck=tk, l_valid=l_valid,
                         pos_base=past, causal=False, top_k=top_k)
    o_a = _attention(pq, k_all, vt_all, mask_t, tq=tq, tk=tk, causal=False)
    o_a = o_a.reshape(b, tq, -1)[:, :t].reshape(b * t, -1)
    cols = (lay.block("qr"), lay.block("kr"), lay.block("vr"), lay.block("gr"))
    o_b, st = _retention(p_main, cols, state.astype(F32), t)
    y = _dense_post(x2, o_a, o_b, p_main, lw, lay, gf, final_norm)
    return (y.reshape(b, t, d), ka.reshape(b, t, N_KV_A, HEAD_DIM_A), va.reshape(b, t, N_KV_A, HEAD_DIM_A),
            ki.reshape(b, t, HEAD_DIM_IDX), st)


def kernel(x_prompt, x_sample, cache_k, cache_v, cache_idx_k, state_ret, norm1_g, w_in, w_pa, w_pb, w_o,
           norm2_g, w_ffn_gate, w_ffn_up, w_ffn_down, norm_f_g):
    assert HEAD_DIM_A == LANES and HEAD_DIM_IDX * 2 == LANES and N_HEADS_IDX % 2 == 0
    depth, d_model = norm1_g.shape
    lay = _MainLayout(d_model)
    xp, xs = x_prompt, x_sample
    outs = [[] for _ in range(8)]
    for l in range(depth):
        parts = _split_w_in(w_in[l], d_model)
        pad = jnp.zeros((d_model, LANES - HEAD_DIM_IDX - N_HEADS_IDX), F32)
        lw = {
            "g1": norm1_g[l], "g2": norm2_g[l],
            "w_main": jnp.concatenate([parts[name] for name, _, _, _ in lay.order], axis=1).astype(BF16),
            "w_kv": jnp.concatenate([parts["ka"], parts["va"], parts["ki"], parts["wi"], pad], axis=1).astype(BF16),
            "w_pa": w_pa[l].astype(BF16), "w_pb": w_pb[l].astype(BF16), "w_o": w_o[l].astype(BF16),
            "wg": w_ffn_gate[l].astype(BF16), "wu": w_ffn_up[l].astype(BF16), "wd": w_ffn_down[l].astype(BF16),
        }
        last = l == depth - 1
        xp, kp, vp, ip, sp = _prompt_layer(xp, lw, lay, norm_f_g, last)
        xs, ks, vs, isl, ss = _sample_layer(xs, cache_k[l], cache_v[l], cache_idx_k[l], state_ret[l],
                                            lw, lay, norm_f_g, last)
        for lst, val in zip(outs, (kp, vp, ip, sp, ks, vs, isl, ss)):
            lst.append(val)
    stacked = [jnp.stack(o) for o in outs]
    return (xp, xs, stacked[0], stacked[1], stacked[2], stacked[3].astype(x_prompt.dtype),
            stacked[4], stacked[5], stacked[6], stacked[7].astype(state_ret.dtype))
```

```python
import functools
import math

import jax
import jax.numpy as jnp
from jax import lax
from jax.experimental import pallas as pl
from jax.experimental.pallas import tpu as pltpu

F32 = jnp.float32
BF16 = jnp.bfloat16

CHUNK = 64
EPS = 1e-6
ROPE_THETA = 10000.0
N_HEADS_A = 16
N_KV_A = 4
HEAD_DIM_A = 128
N_HEADS_IDX = 16
HEAD_DIM_IDX = 64
TOPK_MAX = 256
N_HEADS_R = 8
KEY_DIM_R = 128
VAL_DIM_R = 256

LANES = 128
V7X_VMEM_BYTES = 64 * 1024 * 1024
VMEM_LIMIT = V7X_VMEM_BYTES - 8 * 1024 * 1024
BF16_SUBLANES = 16
VT_ROWS = HEAD_DIM_A + BF16_SUBLANES
LOG2E = math.log2(math.e)
MAX_BISECT_ITERS = 64
NEG_BIG = -1e30


def _cparams(*sem):
    return pltpu.CompilerParams(dimension_semantics=sem, vmem_limit_bytes=VMEM_LIMIT)


def _dot(a, b):
    return jnp.dot(a, b, preferred_element_type=F32)


def _dot_nt(a, b):
    return lax.dot_general(a, b, (((1,), (1,)), ((), ())), preferred_element_type=F32)


def _rmsnorm_rows(x, g):
    return x * lax.rsqrt(jnp.mean(x * x, axis=-1, keepdims=True) + EPS) * g


def _rmsnorm_kernel(x_ref, g_ref, o_ref):
    o_ref[...] = _rmsnorm_rows(x_ref[...], g_ref[...]).astype(o_ref.dtype)


def _rmsnorm(x, g, tm):
    n, d = x.shape
    return pl.pallas_call(
        _rmsnorm_kernel,
        grid=(n // tm,),
        in_specs=[pl.BlockSpec((tm, d), lambda i: (i, 0)),
                  pl.BlockSpec((1, d), lambda i: (0, 0))],
        out_specs=pl.BlockSpec((tm, d), lambda i: (i, 0)),
        out_shape=jax.ShapeDtypeStruct((n, d), BF16),
        compiler_params=_cparams("parallel"),
        name="rmsnorm",
    )(x, g.reshape(1, d))


def _rope_tables(pos):
    posf = pos.astype(F32)[:, None]

    def cs(half):
        inv_freq = ROPE_THETA ** (-jnp.arange(half, dtype=F32) / half)
        ang = posf * inv_freq[None, :]
        return jnp.cos(ang), jnp.sin(ang)

    c, s = cs(HEAD_DIM_A // 2)
    c128 = jnp.concatenate([c, c], axis=1)
    s128 = jnp.concatenate([-s, s], axis=1)
    c, s = cs(HEAD_DIM_IDX // 2)
    z = jnp.zeros_like(s)
    c64 = jnp.tile(c, (1, 4))
    a64 = jnp.tile(jnp.concatenate([-s, z], axis=1), (1, 2))
    b64 = jnp.tile(jnp.concatenate([z, s], axis=1), (1, 2))
    return jnp.stack([c128, s128, c64, a64, b64])


def _rope128(x, rope_ref):
    return x * rope_ref[0] + pltpu.roll(x, 64, 1) * rope_ref[1]


def _rope64(x, rope_ref):
    return (x * rope_ref[2] + pltpu.roll(x, 96, 1) * rope_ref[3]
            + pltpu.roll(x, 32, 1) * rope_ref[4])


def _proj_main_kernel(h_ref, w_ref, rope_ref, o_ref, acc_ref, *, groups, tn):
    j = pl.program_id(1)
    acc_ref[...] = _dot(h_ref[...], w_ref[...])
    for lo, hi, mode, scale in groups:
        @pl.when(jnp.logical_and(j >= lo, j < hi))
        def _(mode=mode, scale=scale):
            for c in range(tn // LANES):
                sl = slice(c * LANES, (c + 1) * LANES)
                x = acc_ref[:, sl]
                if mode == "rope128":
                    y = _rope128(x, rope_ref)
                elif mode == "rope64":
                    y = _rope64(x, rope_ref)
                elif mode == "sigmoid":
                    y = 1.0 / (1.0 + jnp.exp(-x))
                elif mode == "silu":
                    y = x / (1.0 + jnp.exp(-x))
                else:
                    y = x
                if scale != 1.0:
                    y = y * scale
                o_ref[:, sl] = y.astype(o_ref.dtype)


def _proj_main(h, w_main, rope, groups, tm, tn):
    n, d = h.shape
    p = w_main.shape[1]
    n_pos_tiles = rope.shape[1] // tm
    return pl.pallas_call(
        functools.partial(_proj_main_kernel, groups=groups, tn=tn),
        grid=(n // tm, p // tn),
        in_specs=[pl.BlockSpec((tm, d), lambda i, j: (i, 0)),
                  pl.BlockSpec((d, tn), lambda i, j: (0, j)),
                  pl.BlockSpec((5, tm, LANES), lambda i, j: (0, i % n_pos_tiles, 0))],
        out_specs=pl.BlockSpec((tm, tn), lambda i, j: (i, j)),
        out_shape=jax.ShapeDtypeStruct((n, p), BF16),
        scratch_shapes=[pltpu.VMEM((tm, tn), F32)],
        compiler_params=_cparams("parallel", "arbitrary"),
        name="proj_main",
    )(h, w_main, rope)


def _proj_kv_kernel(h_ref, w_ref, rope_ref, ka_ref, va_ref, ki_ref, wt_ref,
                    kab_ref, vtb_ref, kia_ref, kib_ref, acc_ref, *, nk, idx_scale):
    acc_ref[...] = _dot(h_ref[...], w_ref[...])
    for c in range(nk // LANES):
        sl = slice(c * LANES, (c + 1) * LANES)
        y = _rope128(acc_ref[:, sl], rope_ref)
        ka_ref[:, sl] = y
        kab_ref[:, sl] = y.astype(BF16)
    va_ref[...] = acc_ref[:, nk:2 * nk]
    tm = acc_ref.shape[0]
    ones_rows = jnp.where(lax.broadcasted_iota(jnp.int32, (VT_ROWS - HEAD_DIM_A, tm), 0) == 0, 1.0, 0.0)
    for g in range(N_KV_A):
        vg = acc_ref[:, nk + g * HEAD_DIM_A:nk + (g + 1) * HEAD_DIM_A]
        vtb_ref[g * VT_ROWS:g * VT_ROWS + HEAD_DIM_A, :] = vg.T.astype(BF16)
        vtb_ref[g * VT_ROWS + HEAD_DIM_A:(g + 1) * VT_ROWS, :] = ones_rows.astype(BF16)
    z = acc_ref[:, 2 * nk:2 * nk + LANES]
    y = _rope64(z, rope_ref)
    ki_ref[...] = y[:, :HEAD_DIM_IDX]
    lane = lax.broadcasted_iota(jnp.int32, y.shape, 1)
    ya = jnp.where(lane < HEAD_DIM_IDX, y, 0.0)
    kia_ref[...] = ya.astype(BF16)
    kib_ref[...] = pltpu.roll(ya, HEAD_DIM_IDX, 1).astype(BF16)
    wt_ref[...] = (z * idx_scale).T[HEAD_DIM_IDX:HEAD_DIM_IDX + N_HEADS_IDX, :]


def _proj_kv(h, w_kv, rope, tm):
    n, d = h.shape
    nk = N_KV_A * HEAD_DIM_A
    pw = w_kv.shape[1]
    n_pos_tiles = rope.shape[1] // tm
    idx_scale = (HEAD_DIM_IDX ** -0.5) * (N_HEADS_IDX ** -0.5)
    row = lambda i: (i, 0)
    return pl.pallas_call(
        functools.partial(_proj_kv_kernel, nk=nk, idx_scale=idx_scale),
        grid=(n // tm,),
        in_specs=[pl.BlockSpec((tm, d), row),
                  pl.BlockSpec((d, pw), lambda i: (0, 0)),
                  pl.BlockSpec((5, tm, LANES), lambda i: (0, i % n_pos_tiles, 0))],
        out_specs=[pl.BlockSpec((tm, nk), row), pl.BlockSpec((tm, nk), row),
                   pl.BlockSpec((tm, HEAD_DIM_IDX), row),
                   pl.BlockSpec((N_HEADS_IDX, tm), lambda i: (0, i)),
                   pl.BlockSpec((tm, nk), row),
                   pl.BlockSpec((N_KV_A * VT_ROWS, tm), lambda i: (0, i)),
                   pl.BlockSpec((tm, LANES), row), pl.BlockSpec((tm, LANES), row)],
        out_shape=[jax.ShapeDtypeStruct((n, nk), F32), jax.ShapeDtypeStruct((n, nk), F32),
                   jax.ShapeDtypeStruct((n, HEAD_DIM_IDX), F32),
                   jax.ShapeDtypeStruct((N_HEADS_IDX, n), F32),
                   jax.ShapeDtypeStruct((n, nk), BF16),
                   jax.ShapeDtypeStruct((N_KV_A * VT_ROWS, n), BF16),
                   jax.ShapeDtypeStruct((n, LANES), BF16), jax.ShapeDtypeStruct((n, LANES), BF16)],
        scratch_shapes=[pltpu.VMEM((tm, pw), F32)],
        compiler_params=_cparams("parallel"),
        name="proj_kv",
    )(h, w_kv, rope)


def _index_kernel(qi_ref, wt_ref, kia_ref, kib_ref, m_ref, s_ref, *,
                  tq, ck, n_ck_total, l_valid, pos_base, causal, top_k):
    i = pl.program_id(1)
    if causal:
        n_c = (i * tq + tq + ck - 1) // ck
    else:
        n_c = n_ck_total
    qpos = pos_base + i * tq + lax.broadcasted_iota(jnp.int32, (1, tq), 1)
    lim = jnp.minimum(qpos - lax.rem(qpos, CHUNK) + CHUNK, l_valid)
    limf = lim.astype(F32)
    kprime = jnp.minimum(float(top_k), limf)
    inf = jnp.float32(jnp.inf)

    def key_index(off):
        return off + lax.broadcasted_iota(jnp.int32, (ck, tq), 0)

    def score_chunk(c, carry):
        rmax, rmin = carry
        off = pl.multiple_of(c * ck, ck)
        ka = kia_ref[pl.ds(off, ck), :]
        kb = kib_ref[pl.ds(off, ck), :]
        acc = jnp.zeros((ck, tq), F32)
        for p in range(N_HEADS_IDX // 2):
            qp = qi_ref[:, p * LANES:(p + 1) * LANES]
            acc += jnp.maximum(_dot_nt(ka, qp), 0.0) * wt_ref[2 * p:2 * p + 1, :]
            acc += jnp.maximum(_dot_nt(kb, qp), 0.0) * wt_ref[2 * p + 1:2 * p + 2, :]
        adm = key_index(off) < lim
        s_ref[pl.ds(off, ck), :] = jnp.where(adm, acc, -inf)
        rmax = jnp.maximum(rmax, jnp.max(jnp.where(adm, acc, -inf), axis=0, keepdims=True))
        rmin = jnp.minimum(rmin, jnp.min(jnp.where(adm, acc, inf), axis=0, keepdims=True))
        return rmax, rmin

    rmax, rmin = lax.fori_loop(0, n_c, score_chunk,
                               (jnp.full((1, tq), -inf, F32), jnp.full((1, tq), inf, F32)))

    fold_rows = min(ck, 64)

    def fold(ind):
        return ind.reshape(ck // fold_rows, fold_rows, tq).sum(axis=0)

    def count_ge(th):
        def body(c, acc):
            off = pl.multiple_of(c * ck, ck)
            return acc + fold(jnp.where(s_ref[pl.ds(off, ck), :] >= th, 1.0, 0.0))
        acc = lax.fori_loop(0, n_c, body, jnp.zeros((fold_rows, tq), F32))
        return acc.sum(axis=0, keepdims=True)

    def cond(st):
        it, lo, hi, cnt, stuck = st
        active = jnp.where(jnp.logical_and(cnt > kprime, stuck < 0.5), 1.0, 0.0)
        return jnp.logical_and(it < MAX_BISECT_ITERS, jnp.max(active) > 0.5)

    def body(st):
        it, lo, hi, cnt, stuck = st
        mid = jnp.where(hi == inf, rmax, lo + 0.5 * (hi - lo))
        c = count_ge(mid)
        ge = c >= kprime
        no_progress = jnp.logical_or(mid <= lo, mid >= hi)
        return (it + 1, jnp.where(ge, mid, lo), jnp.where(ge, hi, mid), jnp.where(ge, c, cnt),
                jnp.where(no_progress, 1.0, stuck))

    _, lo, hi, cnt, _ = lax.while_loop(
        cond, body, (jnp.int32(0), rmin, jnp.full((1, tq), inf, F32), limf, jnp.zeros((1, tq), F32)))

    unresolved = jnp.max(jnp.where(cnt > kprime, 1.0, 0.0)) > 0.5

    @pl.when(jnp.logical_not(unresolved))
    def _():
        def body(c, _):
            off = pl.multiple_of(c * ck, ck)
            m_ref[pl.ds(off, ck), :] = jnp.where(s_ref[pl.ds(off, ck), :] >= lo, 0.0, -inf).astype(m_ref.dtype)
            return 0
        lax.fori_loop(0, n_c, body, 0)

    @pl.when(unresolved)
    def _():
        need = kprime - count_ge(hi)

        def in_tie(blk):
            return jnp.logical_and(blk >= lo, blk < hi)

        def count_tie_below(jcut):
            def body(c, acc):
                off = pl.multiple_of(c * ck, ck)
                blk = s_ref[pl.ds(off, ck), :]
                e = jnp.logical_and(in_tie(blk), key_index(off).astype(F32) < jcut)
                return acc + fold(jnp.where(e, 1.0, 0.0))
            acc = lax.fori_loop(0, n_c, body, jnp.zeros((fold_rows, tq), F32))
            return acc.sum(axis=0, keepdims=True)

        def jbody(_, st):
            jlo, jhi = st
            mid = jnp.floor(0.5 * (jlo + jhi))
            ok = count_tie_below(mid) >= need
            return jnp.where(ok, jlo, mid + 1.0), jnp.where(ok, mid, jhi)

        n_total = n_ck_total * ck
        _, jcut = lax.fori_loop(0, int(math.ceil(math.log2(n_total + 1))), jbody,
                                (jnp.zeros((1, tq), F32), jnp.full((1, tq), float(n_total), F32)))

        def body(c, _):
            off = pl.multiple_of(c * ck, ck)
            blk = s_ref[pl.ds(off, ck), :]
            e = jnp.logical_and(in_tie(blk), key_index(off).astype(F32) < jcut)
            keep = jnp.logical_or(blk >= hi, e)
            m_ref[pl.ds(off, ck), :] = jnp.where(keep, 0.0, -inf).astype(m_ref.dtype)
            return 0
        lax.fori_loop(0, n_c, body, 0)

    def zero_body(c, _):
        off = pl.multiple_of(c * ck, ck)
        m_ref[pl.ds(off, ck), :] = jnp.full((ck, tq), -inf, m_ref.dtype)
        return 0
    lax.fori_loop(n_c, n_ck_total, zero_body, 0)


def _index_mask(p_main, qi_col_block, wt, kia, kib, *, tq, ck, l_valid, pos_base, causal, top_k):
    b, lp, _ = kia.shape
    nq = p_main.shape[0] // (b * tq)
    qw = N_HEADS_IDX * HEAD_DIM_IDX
    return pl.pallas_call(
        functools.partial(_index_kernel, tq=tq, ck=ck, n_ck_total=lp // ck, l_valid=l_valid,
                          pos_base=pos_base, causal=causal, top_k=top_k),
        grid=(b, nq),
        in_specs=[pl.BlockSpec((tq, qw), lambda bi, i: (bi * nq + i, qi_col_block)),
                  pl.BlockSpec((N_HEADS_IDX, tq), lambda bi, i: (0, bi * nq + i)),
                  pl.BlockSpec((None, lp, LANES), lambda bi, i: (bi, 0, 0)),
                  pl.BlockSpec((None, lp, LANES), lambda bi, i: (bi, 0, 0))],
        out_specs=pl.BlockSpec((None, lp, tq), lambda bi, i: (bi, 0, i)),
        out_shape=jax.ShapeDtypeStruct((b, lp, nq * tq), BF16),
        scratch_shapes=[pltpu.VMEM((lp, tq), F32)],
        compiler_params=_cparams("parallel", "parallel"),
        name="index_mask",
    )(p_main, wt, kia, kib)


def _attn_kernel(q_ref, k_ref, vt_ref, m_ref, o_ref, acc_ref, sa_ref, sb_ref, *,
                 tq, tk, n_kt_total, causal, group):
    i = pl.program_id(1)
    if causal:
        n_kt = (i * tq + tq + tk - 1) // tk
    else:
        n_kt = n_kt_total
    acc_ref[...] = jnp.zeros_like(acc_ref)

    def scores(kt, s_ref):
        off = pl.multiple_of(kt * tk, tk)
        k = k_ref[pl.ds(off, tk), :]
        for h in range(group):
            s_ref[h] = _dot_nt(k, q_ref[:, h * HEAD_DIM_A:(h + 1) * HEAD_DIM_A])

    def softmax_pv(kt, s_ref, ms):
        off = pl.multiple_of(kt * tk, tk)
        vt = vt_ref[:, pl.ds(off, tk)]
        bias = m_ref[pl.ds(off, tk), :].astype(F32)
        out = []
        for h in range(group):
            s = s_ref[h] + bias
            m_new = jnp.maximum(ms[h], jnp.max(s, axis=0, keepdims=True))
            alpha = jnp.exp2(ms[h] - m_new)
            p = jnp.exp2(s - m_new).astype(BF16)
            acc_ref[h] = acc_ref[h] * alpha + _dot(vt, p)
            out.append(m_new)
        return tuple(out)

    def body(j, ms):
        scores(2 * j + 1, sb_ref)
        ms = softmax_pv(2 * j, sa_ref, ms)
        scores(2 * j + 2, sa_ref)
        return softmax_pv(2 * j + 1, sb_ref, ms)

    scores(0, sa_ref)
    n_pairs = (n_kt - 1) // 2
    ms = lax.fori_loop(0, n_pairs, body, tuple(jnp.full((1, tq), NEG_BIG, F32) for _ in range(group)))
    even = n_kt - 1 > 2 * n_pairs

    @pl.when(even)
    def _():
        scores(2 * n_pairs + 1, sb_ref)

    ms = softmax_pv(2 * n_pairs, sa_ref, ms)

    @pl.when(even)
    def _():
        softmax_pv(2 * n_pairs + 1, sb_ref, ms)

    for h in range(group):
        a = acc_ref[h]
        o = a[:HEAD_DIM_A] / a[HEAD_DIM_A:HEAD_DIM_A + 1]
        o_ref[:, h * HEAD_DIM_A:(h + 1) * HEAD_DIM_A] = o.T.astype(o_ref.dtype)


def _attention(p_main, k_bf, vt_bf, mask_t, *, tq, tk, causal):
    b, lp, _ = k_bf.shape
    nq = mask_t.shape[2] // tq
    group = N_HEADS_A // N_KV_A
    gw = group * HEAD_DIM_A
    return pl.pallas_call(
        functools.partial(_attn_kernel, tq=tq, tk=tk, n_kt_total=lp // tk, causal=causal, group=group),
        grid=(b, nq, N_KV_A),
        in_specs=[pl.BlockSpec((tq, gw), lambda bi, i, g: (bi * nq + i, g)),
                  pl.BlockSpec((None, lp, HEAD_DIM_A), lambda bi, i, g: (bi, 0, g)),
                  pl.BlockSpec((VT_ROWS, lp), lambda bi, i, g: (g, bi)),
                  pl.BlockSpec((None, lp, tq), lambda bi, i, g: (bi, 0, i))],
        out_specs=pl.BlockSpec((tq, gw), lambda bi, i, g: (bi * nq + i, g)),
        out_shape=jax.ShapeDtypeStruct((b * nq * tq, N_HEADS_A * HEAD_DIM_A), BF16),
        scratch_shapes=[pltpu.VMEM((group, VT_ROWS, tq), F32),
                        pltpu.VMEM((group, tk, tq), F32), pltpu.VMEM((group, tk, tq), F32)],
        compiler_params=_cparams("parallel", "parallel", "arbitrary"),
        name="attention",
    )(p_main, k_bf, vt_bf, mask_t)


def _retention_tables(c):
    log_gamma = jnp.log1p(-(2.0 ** (-5.0 - jnp.arange(N_HEADS_R, dtype=F32))))
    idx = jnp.arange(c, dtype=F32)
    diff = idx[:, None] - idx[None, :]
    decay = jnp.where(diff[None] >= 0,
                      jnp.exp(jnp.maximum(diff, 0.0)[None] * log_gamma[:, None, None]), 0.0)
    cross = jnp.exp((idx + 1.0)[None, :] * log_gamma[:, None])
    kdec = jnp.exp((c - 1.0 - idx)[None, :] * log_gamma[:, None])
    full = jnp.exp(c * log_gamma)
    bc = lambda a, w: jnp.broadcast_to(a[..., None], a.shape + (w,))
    return decay, bc(cross, KEY_DIM_R), bc(kdec, KEY_DIM_R), bc(full[:, None], VAL_DIM_R)


def _retention_kernel(q_ref, k_ref, v_ref, g_ref, dec_ref, qsc_ref, ksc_ref, gc_ref, s0_ref,
                      o_ref, st_ref):
    @pl.when(pl.program_id(1) == 0)
    def _():
        st_ref[...] = s0_ref[...]

    for h in range(N_HEADS_R):
        ks = slice(h * KEY_DIM_R, (h + 1) * KEY_DIM_R)
        vs = slice(h * VAL_DIM_R, (h + 1) * VAL_DIM_R)
        q = q_ref[:, ks]
        k = k_ref[:, ks]
        v = v_ref[:, vs]
        st = st_ref[h]
        inner = _dot_nt(q, k) * dec_ref[h]
        qd = (q.astype(F32) * qsc_ref[h]).astype(BF16)
        o = _dot(inner.astype(BF16), v) + _dot(qd, st.astype(BF16))
        kdt = (k.astype(F32) * ksc_ref[h]).T.astype(BF16)
        st_ref[h] = st * gc_ref[h] + _dot(kdt, v)
        o = o * lax.rsqrt(jnp.mean(o * o, axis=-1, keepdims=True) + EPS)
        o_ref[:, vs] = (o * g_ref[:, vs].astype(F32)).astype(o_ref.dtype)


def _retention(p_main, cols, state0, c):
    b = state0.shape[0]
    nc = p_main.shape[0] // (b * c)
    kw = N_HEADS_R * KEY_DIM_R
    vw = N_HEADS_R * VAL_DIM_R
    dec, qsc, ksc, gcs = _retention_tables(c)
    qb, kb, vb, gb = cols
    const3 = lambda bi, ci: (0, 0, 0)
    st_spec = pl.BlockSpec((None, N_HEADS_R, KEY_DIM_R, VAL_DIM_R), lambda bi, ci: (bi, 0, 0, 0))
    return pl.pallas_call(
        _retention_kernel,
        grid=(b, nc),
        in_specs=[pl.BlockSpec((c, kw), lambda bi, ci: (bi * nc + ci, qb)),
                  pl.BlockSpec((c, kw), lambda bi, ci: (bi * nc + ci, kb)),
                  pl.BlockSpec((c, vw), lambda bi, ci: (bi * nc + ci, vb)),
                  pl.BlockSpec((c, vw), lambda bi, ci: (bi * nc + ci, gb)),
                  pl.BlockSpec(dec.shape, const3), pl.BlockSpec(qsc.shape, const3),
                  pl.BlockSpec(ksc.shape, const3), pl.BlockSpec(gcs.shape, const3),
                  st_spec],
        out_specs=[pl.BlockSpec((c, vw), lambda bi, ci: (bi * nc + ci, 0)), st_spec],
        out_shape=[jax.ShapeDtypeStruct((b * nc * c, vw), BF16),
                   jax.ShapeDtypeStruct(state0.shape, F32)],
        compiler_params=_cparams("parallel", "arbitrary"),
        name="retention",
    )(p_main, p_main, p_main, p_main, dec, qsc, ksc, gcs, state0)


def _merge_kernel(oa_ref, ob_ref, wa_ref, wb_ref, sa_ref, sb_ref, o_ref):
    m = (_dot(oa_ref[...], wa_ref[...]) * sa_ref[...].astype(F32)
         + _dot(ob_ref[...], wb_ref[...]) * sb_ref[...].astype(F32))
    o_ref[...] = m.astype(o_ref.dtype)


def _merge(o_a, o_b, w_pa, w_pb, p_main, ga_off, gb_off, tm, tn):
    n, wa = o_a.shape
    wb = o_b.shape[1]
    d = w_pa.shape[1]
    return pl.pallas_call(
        _merge_kernel,
        grid=(n // tm, d // tn),
        in_specs=[pl.BlockSpec((tm, wa), lambda i, j: (i, 0)),
                  pl.BlockSpec((tm, wb), lambda i, j: (i, 0)),
                  pl.BlockSpec((wa, tn), lambda i, j: (0, j)),
                  pl.BlockSpec((wb, tn), lambda i, j: (0, j)),
                  pl.BlockSpec((tm, tn), lambda i, j: (i, ga_off // tn + j)),
                  pl.BlockSpec((tm, tn), lambda i, j: (i, gb_off // tn + j))],
        out_specs=pl.BlockSpec((tm, tn), lambda i, j: (i, j)),
        out_shape=jax.ShapeDtypeStruct((n, d), BF16),
        compiler_params=_cparams("parallel", "arbitrary"),
        name="merge",
    )(o_a, o_b, w_pa, w_pb, p_main, p_main)


def _out_proj_kernel(x_ref, m_ref, w_ref, g_ref, x1_ref, h2_ref):
    x1 = x_ref[...] + _dot(m_ref[...], w_ref[...])
    x1_ref[...] = x1
    h2_ref[...] = _rmsnorm_rows(x1, g_ref[...]).astype(h2_ref.dtype)


def _out_proj(x, merged, w_o, g2, tm):
    n, d = x.shape
    row = lambda i: (i, 0)
    return pl.pallas_call(
        _out_proj_kernel,
        grid=(n // tm,),
        in_specs=[pl.BlockSpec((tm, d), row), pl.BlockSpec((tm, d), row),
                  pl.BlockSpec((d, d), lambda i: (0, 0)), pl.BlockSpec((1, d), lambda i: (0, 0))],
        out_specs=[pl.BlockSpec((tm, d), row), pl.BlockSpec((tm, d), row)],
        out_shape=[jax.ShapeDtypeStruct((n, d), F32), jax.ShapeDtypeStruct((n, d), BF16)],
        compiler_params=_cparams("parallel"),
        name="out_proj",
    )(x, merged, w_o, g2.reshape(1, d))


def _ffn_up_kernel(h_ref, wg_ref, wu_ref, o_ref):
    h = h_ref[...]
    a = _dot(h, wg_ref[...])
    u = _dot(h, wu_ref[...])
    o_ref[...] = (a / (1.0 + jnp.exp(-a)) * u).astype(o_ref.dtype)


def _ffn_up(h2, wg, wu, tm, tn):
    n, d = h2.shape
    f = wg.shape[1]
    return pl.pallas_call(
        _ffn_up_kernel,
        grid=(n // tm, f // tn),
        in_specs=[pl.BlockSpec((tm, d), lambda i, j: (i, 0)),
                  pl.BlockSpec((d, tn), lambda i, j: (0, j)),
                  pl.BlockSpec((d, tn), lambda i, j: (0, j))],
        out_specs=pl.BlockSpec((tm, tn), lambda i, j: (i, j)),
        out_shape=jax.ShapeDtypeStruct((n, f), BF16),
        compiler_params=_cparams("parallel", "arbitrary"),
        name="ffn_up",
    )(h2, wg, wu)


def _ffn_down_kernel(u_ref, w_ref, x1_ref, g_ref, y_ref, acc_ref, *, final_norm):
    k = pl.program_id(1)

    @pl.when(k == 0)
    def _():
        acc_ref[...] = x1_ref[...]

    acc_ref[...] += _dot(u_ref[...], w_ref[...])

    @pl.when(k == pl.num_programs(1) - 1)
    def _():
        y = acc_ref[...]
        if final_norm:
            y = _rmsnorm_rows(y, g_ref[...])
        y_ref[...] = y


def _ffn_down(u, wd, x1, gf, tm, tk, final_norm):
    n, f = u.shape
    d = wd.shape[1]
    return pl.pallas_call(
        functools.partial(_ffn_down_kernel, final_norm=final_norm),
        grid=(n // tm, f // tk),
        in_specs=[pl.BlockSpec((tm, tk), lambda i, k: (i, k)),
                  pl.BlockSpec((tk, d), lambda i, k: (k, 0)),
                  pl.BlockSpec((tm, d), lambda i, k: (i, 0)),
                  pl.BlockSpec((1, d), lambda i, k: (0, 0))],
        out_specs=pl.BlockSpec((tm, d), lambda i, k: (i, 0)),
        out_shape=jax.ShapeDtypeStruct((n, d), F32),
        scratch_shapes=[pltpu.VMEM((tm, d), F32)],
        compiler_params=_cparams("parallel", "arbitrary"),
        name="ffn_down",
    )(u, wd, x1, gf.reshape(1, d))


class _MainLayout:
    def __init__(self, d_model):
        wa = N_HEADS_A * HEAD_DIM_A
        wb = N_HEADS_R * VAL_DIM_R
        kr = N_HEADS_R * KEY_DIM_R
        qi = N_HEADS_IDX * HEAD_DIM_IDX
        order = [("qa", wa, "rope128", HEAD_DIM_A ** -0.5 * LOG2E), ("vr", wb, "plain", 1.0),
                 ("gr", wb, "silu", 1.0), ("ga", d_model, "sigmoid", 1.0), ("gb", d_model, "sigmoid", 1.0),
                 ("qi", qi, "rope64", 1.0), ("qr", kr, "rope128", 1.0),
                 ("kr", kr, "rope128", KEY_DIM_R ** -0.5)]
        self.off, self.width = {}, {}
        self.order = order
        o = 0
        for name, w, _, _ in order:
            self.off[name], self.width[name] = o, w
            o += w
        self.total = o

    def groups(self, tn):
        out = []
        for name, w, mode, scale in self.order:
            assert self.off[name] % tn == 0 and w % tn == 0
            out.append((self.off[name] // tn, (self.off[name] + w) // tn, mode, scale))
        return tuple(out)

    def block(self, name, width=None):
        width = width or self.width[name]
        assert self.off[name] % width == 0
        return self.off[name] // width


def _split_w_in(w, d_model):
    wa = N_HEADS_A * HEAD_DIM_A
    nk = N_KV_A * HEAD_DIM_A
    wb = N_HEADS_R * VAL_DIM_R
    kr = N_HEADS_R * KEY_DIM_R
    names = ("qa", "ka", "va", "qi", "ki", "wi", "qr", "kr", "vr", "gr", "ga", "gb")
    widths = (wa, nk, nk, N_HEADS_IDX * HEAD_DIM_IDX, HEAD_DIM_IDX, N_HEADS_IDX, kr, kr, wb, wb,
              d_model, d_model)
    parts, o = {}, 0
    for name, wd in zip(names, widths):
        parts[name] = w[:, o:o + wd]
        o += wd
    assert o == w.shape[1]
    return parts


def _tile(n, pref):
    t = min(n, pref)
    assert n % t == 0
    return t


def _dense_pre(x2, pos_rows, lw, lay):
    n = x2.shape[0]
    tm = _tile(n, 1024)
    rope = _rope_tables(pos_rows)
    h = _rmsnorm(x2, lw["g1"], tm)
    tn = 1024
    p_main = _proj_main(h, lw["w_main"], rope, lay.groups(tn), tm, tn)
    kv = _proj_kv(h, lw["w_kv"], rope, _tile(n, 512))
    return p_main, kv


def _dense_post(x2, o_a, o_b, p_main, lw, lay, gf, final_norm):
    n = x2.shape[0]
    tm = _tile(n, 1024)
    merged = _merge(o_a, o_b, lw["w_pa"], lw["w_pb"], p_main, lay.off["ga"], lay.off["gb"], tm, 512)
    x1, h2 = _out_proj(x2, merged, lw["w_o"], lw["g2"], _tile(n, 512))
    u = _ffn_up(h2, lw["wg"], lw["wu"], tm, 512)
    return _ffn_down(u, lw["wd"], x1, gf, _tile(n, 512), 512, final_norm)


def _prompt_layer(xp, lw, lay, gf, final_norm):
    b, t, d = xp.shape
    x2 = xp.reshape(b * t, d)
    p_main, (ka, va, ki, wt, ka_bf, vt_bf, kia, kib) = _dense_pre(x2, jnp.arange(t, dtype=jnp.int32), lw, lay)
    top_k = min(TOPK_MAX, t // 4)
    tq, ck = _tile(t, 256), _tile(t, 512)
    mask_t = _index_mask(p_main, lay.block("qi"), wt, kia.reshape(b, t, LANES), kib.reshape(b, t, LANES),
                         tq=tq, ck=ck, l_valid=t, pos_base=0, causal=True, top_k=top_k)
    nk = N_KV_A * HEAD_DIM_A
    o_a = _attention(p_main, ka_bf.reshape(b, t, nk), vt_bf, mask_t, tq=tq, tk=ck, causal=True)
    state0 = jnp.zeros((b, N_HEADS_R, KEY_DIM_R, VAL_DIM_R), F32)
    cols = (lay.block("qr"), lay.block("kr"), lay.block("vr"), lay.block("gr"))
    o_b, st = _retention(p_main, cols, state0, _tile(t, 256))
    y = _dense_post(x2, o_a, o_b, p_main, lw, lay, gf, final_norm)
    return (y.reshape(b, t, d), ka.reshape(b, t, N_KV_A, HEAD_DIM_A), va.reshape(b, t, N_KV_A, HEAD_DIM_A),
            ki.reshape(b, t, HEAD_DIM_IDX), st)


def _sample_layer(xs, cache_k, cache_v, cache_i, state, lw, lay, gf, final_norm):
    b, t, d = xs.shape
    past = cache_k.shape[1]
    nk = N_KV_A * HEAD_DIM_A
    x2 = xs.reshape(b * t, d)
    pos = jnp.tile(past + jnp.arange(t, dtype=jnp.int32), b)
    p_main, (ka, va, ki, wt, ka_bf, vt_bf, kia, kib) = _dense_pre(x2, pos, lw, lay)

    tq = LANES
    l_valid = past + t
    tk = 3 * LANES
    lp = -(-l_valid // tk) * tk
    pq = jnp.pad(p_main.reshape(b, t, -1), ((0, 0), (0, tq - t), (0, 0))).reshape(b * tq, -1)
    wtq = jnp.pad(wt.reshape(N_HEADS_IDX, b, t), ((0, 0), (0, 0), (0, tq - t))).reshape(N_HEADS_IDX, b * tq)
    kpad = ((0, 0), (0, lp - l_valid), (0, 0))
    ci = cache_i.astype(BF16)
    zi = jnp.zeros_like(ci)
    kia_all = jnp.pad(jnp.concatenate([jnp.concatenate([ci, zi], -1), kia.reshape(b, t, LANES)], 1), kpad)
    kib_all = jnp.pad(jnp.concatenate([jnp.concatenate([zi, ci], -1), kib.reshape(b, t, LANES)], 1), kpad)
    k_all = jnp.pad(jnp.concatenate([cache_k.reshape(b, past, nk).astype(BF16), ka_bf.reshape(b, t, nk)], 1), kpad)
    vt_new = vt_bf.reshape(N_KV_A, VT_ROWS, b, t)
    vt_cache = cache_v.reshape(b, past, N_KV_A, HEAD_DIM_A).transpose(2, 3, 0, 1).astype(BF16)
    ones_rows = jnp.zeros((N_KV_A, VT_ROWS - HEAD_DIM_A, b, past), BF16).at[:, 0].set(1.0)
    vt_all = jnp.concatenate([jnp.concatenate([vt_cache, ones_rows], 1), vt_new], 3)
    vt_all = jnp.pad(vt_all, ((0, 0), (0, 0), (0, 0), (0, lp - l_valid))).reshape(N_KV_A * VT_ROWS, b * lp)

    top_k = min(TOPK_MAX, l_valid // 4)
    mask_t = _index_mask(pq, lay.block("qi"), wtq, kia_all, kib_all, tq=tq, ck=tk, l_valid=l_valid,
                         pos_base=past, causal=False, top_k=top_k)
    o_a = _attention(pq, k_all, vt_all, mask_t, tq=tq, tk=tk, causal=False)
    o_a = o_a.reshape(b, tq, -1)[:, :t].reshape(b * t, -1)
    cols = (lay.block("qr"), lay.block("kr"), lay.block("vr"), lay.block("gr"))
    o_b, st = _retention(p_main, cols, state.astype(F32), t)
    y = _dense_post(x2, o_a, o_b, p_main, lw, lay, gf, final_norm)
    return (y.reshape(b, t, d), ka.reshape(b, t, N_KV_A, HEAD_DIM_A), va.reshape(b, t, N_KV_A, HEAD_DIM_A),
            ki.reshape(b, t, HEAD_DIM_IDX), st)


def kernel(x_prompt, x_sample, cache_k, cache_v, cache_idx_k, state_ret, norm1_g, w_in, w_pa, w_pb, w_o,
           norm2_g, w_ffn_gate, w_ffn_up, w_ffn_down, norm_f_g):
    assert HEAD_DIM_A == LANES and HEAD_DIM_IDX * 2 == LANES and N_HEADS_IDX % 2 == 0
    depth, d_model = norm1_g.shape
    lay = _MainLayout(d_model)
    xp, xs = x_prompt, x_sample
    outs = [[] for _ in range(8)]
    for l in range(depth):
        parts = _split_w_in(w_in[l], d_model)
        pad = jnp.zeros((d_model, LANES - HEAD_DIM_IDX - N_HEADS_IDX), F32)
        lw = {
            "g1": norm1_g[l], "g2": norm2_g[l],
            "w_main": jnp.concatenate([parts[name] for name, _, _, _ in lay.order], axis=1).astype(BF16),
            "w_kv": jnp.concatenate([parts["ka"], parts["va"], parts["ki"], parts["wi"], pad], axis=1).astype(BF16),
            "w_pa": w_pa[l].astype(BF16), "w_pb": w_pb[l].astype(BF16), "w_o": w_o[l].astype(BF16),
            "wg": w_ffn_gate[l].astype(BF16), "wu": w_ffn_up[l].astype(BF16), "wd": w_ffn_down[l].astype(BF16),
        }
        last = l == depth - 1
        xp, kp, vp, ip, sp = _prompt_layer(xp, lw, lay, norm_f_g, last)
        xs, ks, vs, isl, ss = _sample_layer(xs, cache_k[l], cache_v[l], cache_idx_k[l], state_ret[l],
                                            lw, lay, norm_f_g, last)
        for lst, val in zip(outs, (kp, vp, ip, sp, ks, vs, isl, ss)):
            lst.append(val)
    stacked = [jnp.stack(o) for o in outs]
    return (xp, xs, stacked[0], stacked[1], stacked[2], stacked[3].astype(x_prompt.dtype),
            stacked[4], stacked[5], stacked[6], stacked[7].astype(state_ret.dtype))
```

```python
import functools
import math

import jax
import jax.numpy as jnp
from jax import lax
from jax.experimental import pallas as pl
from jax.experimental.pallas import tpu as pltpu

F32 = jnp.float32
BF16 = jnp.bfloat16

CHUNK = 64
EPS = 1e-6
ROPE_THETA = 10000.0
N_HEADS_A = 16
N_KV_A = 4
HEAD_DIM_A = 128
N_HEADS_IDX = 16
HEAD_DIM_IDX = 64
TOPK_MAX = 256
N_HEADS_R = 8
KEY_DIM_R = 128
VAL_DIM_R = 256

LANES = 128
MXU_COLS = 256
V7X_VMEM_BYTES = 64 * 1024 * 1024
VMEM_LIMIT = V7X_VMEM_BYTES - 8 * 1024 * 1024
BF16_SUBLANES = 16
VT_ROWS = HEAD_DIM_A + BF16_SUBLANES
LOG2E = math.log2(math.e)
MAX_BISECT_ITERS = 64
NEG_BIG = -1e30


def _cparams(*sem):
    return pltpu.CompilerParams(dimension_semantics=sem, vmem_limit_bytes=VMEM_LIMIT)


def _dot(a, b):
    return jnp.dot(a, b, preferred_element_type=F32)


def _dot_nt(a, b):
    return lax.dot_general(a, b, (((1,), (1,)), ((), ())), preferred_element_type=F32)


def _rmsnorm_rows(x, g):
    return x * lax.rsqrt(jnp.mean(x * x, axis=-1, keepdims=True) + EPS) * g


def _rmsnorm_kernel(x_ref, g_ref, o_ref):
    o_ref[...] = _rmsnorm_rows(x_ref[...], g_ref[...]).astype(o_ref.dtype)


def _rmsnorm(x, g, tm):
    n, d = x.shape
    return pl.pallas_call(
        _rmsnorm_kernel,
        grid=(n // tm,),
        in_specs=[pl.BlockSpec((tm, d), lambda i: (i, 0)),
                  pl.BlockSpec((1, d), lambda i: (0, 0))],
        out_specs=pl.BlockSpec((tm, d), lambda i: (i, 0)),
        out_shape=jax.ShapeDtypeStruct((n, d), BF16),
        compiler_params=_cparams("parallel"),
        name="rmsnorm",
    )(x, g.reshape(1, d))


def _rope_tables(pos):
    posf = pos.astype(F32)[:, None]

    def cs(half):
        inv_freq = ROPE_THETA ** (-jnp.arange(half, dtype=F32) / half)
        ang = posf * inv_freq[None, :]
        return jnp.cos(ang), jnp.sin(ang)

    c, s = cs(HEAD_DIM_A // 2)
    c128 = jnp.concatenate([c, c], axis=1)
    s128 = jnp.concatenate([-s, s], axis=1)
    c, s = cs(HEAD_DIM_IDX // 2)
    z = jnp.zeros_like(s)
    c64 = jnp.tile(c, (1, 4))
    a64 = jnp.tile(jnp.concatenate([-s, z], axis=1), (1, 2))
    b64 = jnp.tile(jnp.concatenate([z, s], axis=1), (1, 2))
    return jnp.stack([c128, s128, c64, a64, b64])


def _rope128(x, rope_ref):
    return x * rope_ref[0] + pltpu.roll(x, 64, 1) * rope_ref[1]


def _rope64(x, rope_ref):
    return (x * rope_ref[2] + pltpu.roll(x, 96, 1) * rope_ref[3]
            + pltpu.roll(x, 32, 1) * rope_ref[4])


def _proj_main_kernel(h_ref, w_ref, rope_ref, o_ref, *, groups, tn):
    j = pl.program_id(1)
    for lo, hi, mode, scale in groups:
        @pl.when(jnp.logical_and(j >= lo, j < hi))
        def _(mode=mode, scale=scale):
            for b in range(tn // MXU_COLS):
                acc = _dot(h_ref[...], w_ref[:, b * MXU_COLS:(b + 1) * MXU_COLS])
                for c in range(MXU_COLS // LANES):
                    x = acc[:, c * LANES:(c + 1) * LANES]
                    if mode == "rope128":
                        y = _rope128(x, rope_ref)
                    elif mode == "rope64":
                        y = _rope64(x, rope_ref)
                    elif mode == "sigmoid":
                        y = 1.0 / (1.0 + jnp.exp(-x))
                    elif mode == "silu":
                        y = x / (1.0 + jnp.exp(-x))
                    else:
                        y = x
                    if scale != 1.0:
                        y = y * scale
                    col = b * MXU_COLS + c * LANES
                    o_ref[:, col:col + LANES] = y.astype(o_ref.dtype)


def _proj_main(h, w_main, rope, groups, tm, tn):
    n, d = h.shape
    p = w_main.shape[1]
    n_pos_tiles = rope.shape[1] // tm
    return pl.pallas_call(
        functools.partial(_proj_main_kernel, groups=groups, tn=tn),
        grid=(n // tm, p // tn),
        in_specs=[pl.BlockSpec((tm, d), lambda i, j: (i, 0)),
                  pl.BlockSpec((d, tn), lambda i, j: (0, j)),
                  pl.BlockSpec((5, tm, LANES), lambda i, j: (0, i % n_pos_tiles, 0))],
        out_specs=pl.BlockSpec((tm, tn), lambda i, j: (i, j)),
        out_shape=jax.ShapeDtypeStruct((n, p), BF16),
        compiler_params=_cparams("parallel", "arbitrary"),
        name="proj_main",
    )(h, w_main, rope)


def _proj_kv_kernel(h_ref, w_ref, rope_ref, ka_ref, va_ref, ki_ref, wt_ref,
                    kab_ref, vtb_ref, kia_ref, kib_ref, acc_ref, *, nk, idx_scale):
    acc_ref[...] = _dot(h_ref[...], w_ref[...])
    for c in range(nk // LANES):
        sl = slice(c * LANES, (c + 1) * LANES)
        y = _rope128(acc_ref[:, sl], rope_ref)
        ka_ref[:, sl] = y
        kab_ref[:, sl] = y.astype(BF16)
    va_ref[...] = acc_ref[:, nk:2 * nk]
    tm = acc_ref.shape[0]
    ones_rows = jnp.where(lax.broadcasted_iota(jnp.int32, (VT_ROWS - HEAD_DIM_A, tm), 0) == 0, 1.0, 0.0)
    for g in range(N_KV_A):
        vg = acc_ref[:, nk + g * HEAD_DIM_A:nk + (g + 1) * HEAD_DIM_A]
        vtb_ref[g * VT_ROWS:g * VT_ROWS + HEAD_DIM_A, :] = vg.T.astype(BF16)
        vtb_ref[g * VT_ROWS + HEAD_DIM_A:(g + 1) * VT_ROWS, :] = ones_rows.astype(BF16)
    z = acc_ref[:, 2 * nk:2 * nk + LANES]
    y = _rope64(z, rope_ref)
    ki_ref[...] = y[:, :HEAD_DIM_IDX]
    lane = lax.broadcasted_iota(jnp.int32, y.shape, 1)
    ya = jnp.where(lane < HEAD_DIM_IDX, y, 0.0)
    kia_ref[...] = ya.astype(BF16)
    kib_ref[...] = pltpu.roll(ya, HEAD_DIM_IDX, 1).astype(BF16)
    wt_ref[...] = (z * idx_scale).T[HEAD_DIM_IDX:HEAD_DIM_IDX + N_HEADS_IDX, :]


def _proj_kv(h, w_kv, rope, tm):
    n, d = h.shape
    nk = N_KV_A * HEAD_DIM_A
    pw = w_kv.shape[1]
    n_pos_tiles = rope.shape[1] // tm
    idx_scale = (HEAD_DIM_IDX ** -0.5) * (N_HEADS_IDX ** -0.5)
    row = lambda i: (i, 0)
    return pl.pallas_call(
        functools.partial(_proj_kv_kernel, nk=nk, idx_scale=idx_scale),
        grid=(n // tm,),
        in_specs=[pl.BlockSpec((tm, d), row),
                  pl.BlockSpec((d, pw), lambda i: (0, 0)),
                  pl.BlockSpec((5, tm, LANES), lambda i: (0, i % n_pos_tiles, 0))],
        out_specs=[pl.BlockSpec((tm, nk), row), pl.BlockSpec((tm, nk), row),
                   pl.BlockSpec((tm, HEAD_DIM_IDX), row),
                   pl.BlockSpec((N_HEADS_IDX, tm), lambda i: (0, i)),
                   pl.BlockSpec((tm, nk), row),
                   pl.BlockSpec((N_KV_A * VT_ROWS, tm), lambda i: (0, i)),
                   pl.BlockSpec((tm, LANES), row), pl.BlockSpec((tm, LANES), row)],
        out_shape=[jax.ShapeDtypeStruct((n, nk), F32), jax.ShapeDtypeStruct((n, nk), F32),
                   jax.ShapeDtypeStruct((n, HEAD_DIM_IDX), F32),
                   jax.ShapeDtypeStruct((N_HEADS_IDX, n), F32),
                   jax.ShapeDtypeStruct((n, nk), BF16),
                   jax.ShapeDtypeStruct((N_KV_A * VT_ROWS, n), BF16),
                   jax.ShapeDtypeStruct((n, LANES), BF16), jax.ShapeDtypeStruct((n, LANES), BF16)],
        scratch_shapes=[pltpu.VMEM((tm, pw), F32)],
        compiler_params=_cparams("parallel"),
        name="proj_kv",
    )(h, w_kv, rope)


def _index_kernel(qi_ref, wt_ref, kia_ref, kib_ref, m_ref, s_ref, *,
                  tq, ck, n_ck_total, l_valid, pos_base, causal, top_k):
    i = pl.program_id(1)
    if causal:
        n_c = (i * tq + tq + ck - 1) // ck
    else:
        n_c = n_ck_total
    qpos = pos_base + i * tq + lax.broadcasted_iota(jnp.int32, (1, tq), 1)
    lim = jnp.minimum(qpos - lax.rem(qpos, CHUNK) + CHUNK, l_valid)
    limf = lim.astype(F32)
    kprime = jnp.minimum(float(top_k), limf)
    inf = jnp.float32(jnp.inf)

    def key_index(off):
        return off + lax.broadcasted_iota(jnp.int32, (ck, tq), 0)

    def score_chunk(c, carry):
        rmax, rmin = carry
        off = pl.multiple_of(c * ck, ck)
        ka = kia_ref[pl.ds(off, ck), :]
        kb = kib_ref[pl.ds(off, ck), :]
        acc = jnp.zeros((ck, tq), F32)
        for p in range(N_HEADS_IDX // 2):
            qp = qi_ref[:, p * LANES:(p + 1) * LANES]
            acc += jnp.maximum(_dot_nt(ka, qp), 0.0) * wt_ref[2 * p:2 * p + 1, :]
            acc += jnp.maximum(_dot_nt(kb, qp), 0.0) * wt_ref[2 * p + 1:2 * p + 2, :]
        adm = key_index(off) < lim
        s_ref[pl.ds(off, ck), :] = jnp.where(adm, acc, -inf)
        rmax = jnp.maximum(rmax, jnp.max(jnp.where(adm, acc, -inf), axis=0, keepdims=True))
        rmin = jnp.minimum(rmin, jnp.min(jnp.where(adm, acc, inf), axis=0, keepdims=True))
        return rmax, rmin

    rmax, rmin = lax.fori_loop(0, n_c, score_chunk,
                               (jnp.full((1, tq), -inf, F32), jnp.full((1, tq), inf, F32)))

    fold_rows = min(ck, 64)

    def fold(ind):
        return ind.reshape(ck // fold_rows, fold_rows, tq).sum(axis=0)

    def count_ge(th):
        def body(c, acc):
            off = pl.multiple_of(c * ck, ck)
            return acc + fold(jnp.where(s_ref[pl.ds(off, ck), :] >= th, 1.0, 0.0))
        acc = lax.fori_loop(0, n_c, body, jnp.zeros((fold_rows, tq), F32))
        return acc.sum(axis=0, keepdims=True)

    def cond(st):
        it, lo, hi, cnt, stuck = st
        active = jnp.where(jnp.logical_and(cnt > kprime, stuck < 0.5), 1.0, 0.0)
        return jnp.logical_and(it < MAX_BISECT_ITERS, jnp.max(active) > 0.5)

    def body(st):
        it, lo, hi, cnt, stuck = st
        open_top = hi == inf
        mid = lo + 0.5 * (jnp.where(open_top, rmax, hi) - lo)
        mid = jnp.where(jnp.logical_and(open_top, mid <= lo), rmax, mid)
        c = count_ge(mid)
        ge = c >= kprime
        no_progress = jnp.logical_or(mid <= lo, mid >= hi)
        return (it + 1, jnp.where(ge, mid, lo), jnp.where(ge, hi, mid), jnp.where(ge, c, cnt),
                jnp.where(no_progress, 1.0, stuck))

    _, lo, hi, cnt, _ = lax.while_loop(
        cond, body, (jnp.int32(0), rmin, jnp.full((1, tq), inf, F32), limf, jnp.zeros((1, tq), F32)))

    unresolved = jnp.max(jnp.where(cnt > kprime, 1.0, 0.0)) > 0.5

    @pl.when(jnp.logical_not(unresolved))
    def _():
        def body(c, _):
            off = pl.multiple_of(c * ck, ck)
            m_ref[pl.ds(off, ck), :] = jnp.where(s_ref[pl.ds(off, ck), :] >= lo, 0.0, -inf).astype(m_ref.dtype)
            return 0
        lax.fori_loop(0, n_c, body, 0)

    @pl.when(unresolved)
    def _():
        need = kprime - count_ge(hi)

        def in_tie(blk):
            return jnp.logical_and(blk >= lo, blk < hi)

        def count_tie_below(jcut):
            def body(c, acc):
                off = pl.multiple_of(c * ck, ck)
                blk = s_ref[pl.ds(off, ck), :]
                e = jnp.logical_and(in_tie(blk), key_index(off).astype(F32) < jcut)
                return acc + fold(jnp.where(e, 1.0, 0.0))
            acc = lax.fori_loop(0, n_c, body, jnp.zeros((fold_rows, tq), F32))
            return acc.sum(axis=0, keepdims=True)

        def jbody(_, st):
            jlo, jhi = st
            mid = jnp.floor(0.5 * (jlo + jhi))
            ok = count_tie_below(mid) >= need
            return jnp.where(ok, jlo, mid + 1.0), jnp.where(ok, mid, jhi)

        n_total = n_ck_total * ck
        _, jcut = lax.fori_loop(0, int(math.ceil(math.log2(n_total + 1))), jbody,
                                (jnp.zeros((1, tq), F32), jnp.full((1, tq), float(n_total), F32)))

        def body(c, _):
            off = pl.multiple_of(c * ck, ck)
            blk = s_ref[pl.ds(off, ck), :]
            e = jnp.logical_and(in_tie(blk), key_index(off).astype(F32) < jcut)
            keep = jnp.logical_or(blk >= hi, e)
            m_ref[pl.ds(off, ck), :] = jnp.where(keep, 0.0, -inf).astype(m_ref.dtype)
            return 0
        lax.fori_loop(0, n_c, body, 0)

    def zero_body(c, _):
        off = pl.multiple_of(c * ck, ck)
        m_ref[pl.ds(off, ck), :] = jnp.full((ck, tq), -inf, m_ref.dtype)
        return 0
    lax.fori_loop(n_c, n_ck_total, zero_body, 0)


def _index_mask(p_main, qi_col_block, wt, kia, kib, *, tq, ck, l_valid, pos_base, causal, top_k):
    b, lp, _ = kia.shape
    nq = p_main.shape[0] // (b * tq)
    qw = N_HEADS_IDX * HEAD_DIM_IDX
    return pl.pallas_call(
        functools.partial(_index_kernel, tq=tq, ck=ck, n_ck_total=lp // ck, l_valid=l_valid,
                          pos_base=pos_base, causal=causal, top_k=top_k),
        grid=(b, nq),
        in_specs=[pl.BlockSpec((tq, qw), lambda bi, i: (bi * nq + i, qi_col_block)),
                  pl.BlockSpec((N_HEADS_IDX, tq), lambda bi, i: (0, bi * nq + i)),
                  pl.BlockSpec((None, lp, LANES), lambda bi, i: (bi, 0, 0)),
                  pl.BlockSpec((None, lp, LANES), lambda bi, i: (bi, 0, 0))],
        out_specs=pl.BlockSpec((None, lp, tq), lambda bi, i: (bi, 0, i)),
        out_shape=jax.ShapeDtypeStruct((b, lp, nq * tq), BF16),
        scratch_shapes=[pltpu.VMEM((lp, tq), F32)],
        compiler_params=_cparams("parallel", "parallel"),
        name="index_mask",
    )(p_main, wt, kia, kib)


def _attn_kernel(q_ref, k_ref, vt_ref, m_ref, o_ref, acc_ref, sa_ref, sb_ref, *,
                 tq, tk, n_kt_total, causal, group):
    i = pl.program_id(1)
    if causal:
        n_kt = (i * tq + tq + tk - 1) // tk
    else:
        n_kt = n_kt_total
    acc_ref[...] = jnp.zeros_like(acc_ref)

    def scores(kt, s_ref):
        off = pl.multiple_of(kt * tk, tk)
        k = k_ref[pl.ds(off, tk), :]
        for h in range(group):
            s_ref[h] = _dot_nt(k, q_ref[:, h * HEAD_DIM_A:(h + 1) * HEAD_DIM_A])

    def softmax_pv(kt, s_ref, ms):
        off = pl.multiple_of(kt * tk, tk)
        vt = vt_ref[:, pl.ds(off, tk)]
        bias = m_ref[pl.ds(off, tk), :].astype(F32)
        out = []
        for h in range(group):
            s = s_ref[h] + bias
            m_new = jnp.maximum(ms[h], jnp.max(s, axis=0, keepdims=True))
            alpha = jnp.exp2(ms[h] - m_new)
            p = jnp.exp2(s - m_new).astype(BF16)
            acc_ref[h] = acc_ref[h] * alpha + _dot(vt, p)
            out.append(m_new)
        return tuple(out)

    def body(j, ms):
        scores(2 * j + 1, sb_ref)
        ms = softmax_pv(2 * j, sa_ref, ms)
        scores(2 * j + 2, sa_ref)
        return softmax_pv(2 * j + 1, sb_ref, ms)

    scores(0, sa_ref)
    n_pairs = (n_kt - 1) // 2
    ms = lax.fori_loop(0, n_pairs, body, tuple(jnp.full((1, tq), NEG_BIG, F32) for _ in range(group)))
    even = n_kt - 1 > 2 * n_pairs

    @pl.when(even)
    def _():
        scores(2 * n_pairs + 1, sb_ref)

    ms = softmax_pv(2 * n_pairs, sa_ref, ms)

    @pl.when(even)
    def _():
        softmax_pv(2 * n_pairs + 1, sb_ref, ms)

    for h in range(group):
        a = acc_ref[h]
        o = a[:HEAD_DIM_A] / a[HEAD_DIM_A:HEAD_DIM_A + 1]
        o_ref[:, h * HEAD_DIM_A:(h + 1) * HEAD_DIM_A] = o.T.astype(o_ref.dtype)


def _attention(p_main, k_bf, vt_bf, mask_t, *, tq, tk, causal):
    b, lp, _ = k_bf.shape
    nq = mask_t.shape[2] // tq
    group = N_HEADS_A // N_KV_A
    gw = group * HEAD_DIM_A
    return pl.pallas_call(
        functools.partial(_attn_kernel, tq=tq, tk=tk, n_kt_total=lp // tk, causal=causal, group=group),
        grid=(b, nq, N_KV_A),
        in_specs=[pl.BlockSpec((tq, gw), lambda bi, i, g: (bi * nq + i, g)),
                  pl.BlockSpec((None, lp, HEAD_DIM_A), lambda bi, i, g: (bi, 0, g)),
                  pl.BlockSpec((VT_ROWS, lp), lambda bi, i, g: (g, bi)),
                  pl.BlockSpec((None, lp, tq), lambda bi, i, g: (bi, 0, i))],
        out_specs=pl.BlockSpec((tq, gw), lambda bi, i, g: (bi * nq + i, g)),
        out_shape=jax.ShapeDtypeStruct((b * nq * tq, N_HEADS_A * HEAD_DIM_A), BF16),
        scratch_shapes=[pltpu.VMEM((group, VT_ROWS, tq), F32),
                        pltpu.VMEM((group, tk, tq), F32), pltpu.VMEM((group, tk, tq), F32)],
        compiler_params=_cparams("parallel", "parallel", "arbitrary"),
        name="attention",
    )(p_main, k_bf, vt_bf, mask_t)


def _retention_tables(c):
    log_gamma = jnp.log1p(-(2.0 ** (-5.0 - jnp.arange(N_HEADS_R, dtype=F32))))
    idx = jnp.arange(c, dtype=F32)
    diff = idx[:, None] - idx[None, :]
    decay = jnp.where(diff[None] >= 0,
                      jnp.exp(jnp.maximum(diff, 0.0)[None] * log_gamma[:, None, None]), 0.0)
    cross = jnp.exp((idx + 1.0)[None, :] * log_gamma[:, None])
    kdec = jnp.exp((c - 1.0 - idx)[None, :] * log_gamma[:, None])
    full = jnp.exp(c * log_gamma)
    bc = lambda a, w: jnp.broadcast_to(a[..., None], a.shape + (w,))
    return decay, bc(cross, KEY_DIM_R), bc(kdec, KEY_DIM_R), bc(full[:, None], VAL_DIM_R)


def _retention_kernel(q_ref, k_ref, v_ref, g_ref, dec_ref, qsc_ref, ksc_ref, gc_ref, s0_ref,
                      o_ref, st_ref):
    @pl.when(pl.program_id(1) == 0)
    def _():
        st_ref[...] = s0_ref[...]

    for h in range(N_HEADS_R):
        ks = slice(h * KEY_DIM_R, (h + 1) * KEY_DIM_R)
        vs = slice(h * VAL_DIM_R, (h + 1) * VAL_DIM_R)
        q = q_ref[:, ks]
        k = k_ref[:, ks]
        v = v_ref[:, vs]
        st = st_ref[h]
        inner = _dot_nt(q, k) * dec_ref[h]
        qd = (q.astype(F32) * qsc_ref[h]).astype(BF16)
        o = _dot(inner.astype(BF16), v) + _dot(qd, st.astype(BF16))
        kdt = (k.astype(F32) * ksc_ref[h]).T.astype(BF16)
        st_ref[h] = st * gc_ref[h] + _dot(kdt, v)
        o = o * lax.rsqrt(jnp.mean(o * o, axis=-1, keepdims=True) + EPS)
        o_ref[:, vs] = (o * g_ref[:, vs].astype(F32)).astype(o_ref.dtype)


def _retention(p_main, cols, state0, c):
    b = state0.shape[0]
    nc = p_main.shape[0] // (b * c)
    kw = N_HEADS_R * KEY_DIM_R
    vw = N_HEADS_R * VAL_DIM_R
    dec, qsc, ksc, gcs = _retention_tables(c)
    qb, kb, vb, gb = cols
    const3 = lambda bi, ci: (0, 0, 0)
    st_spec = pl.BlockSpec((None, N_HEADS_R, KEY_DIM_R, VAL_DIM_R), lambda bi, ci: (bi, 0, 0, 0))
    return pl.pallas_call(
        _retention_kernel,
        grid=(b, nc),
        in_specs=[pl.BlockSpec((c, kw), lambda bi, ci: (bi * nc + ci, qb)),
                  pl.BlockSpec((c, kw), lambda bi, ci: (bi * nc + ci, kb)),
                  pl.BlockSpec((c, vw), lambda bi, ci: (bi * nc + ci, vb)),
                  pl.BlockSpec((c, vw), lambda bi, ci: (bi * nc + ci, gb)),
                  pl.BlockSpec(dec.shape, const3), pl.BlockSpec(qsc.shape, const3),
                  pl.BlockSpec(ksc.shape, const3), pl.BlockSpec(gcs.shape, const3),
                  st_spec],
        out_specs=[pl.BlockSpec((c, vw), lambda bi, ci: (bi * nc + ci, 0)), st_spec],
        out_shape=[jax.ShapeDtypeStruct((b * nc * c, vw), BF16),
                   jax.ShapeDtypeStruct(state0.shape, F32)],
        compiler_params=_cparams("parallel", "arbitrary"),
        name="retention",
    )(p_main, p_main, p_main, p_main, dec, qsc, ksc, gcs, state0)


def _merge_out_kernel(oa_ref, ob_ref, wa_ref, wb_ref, sa_ref, sb_ref, wo_ref, x_ref, g_ref, x1_ref, h2_ref):
    c = pl.program_id(1)

    @pl.when(c == 0)
    def _():
        x1_ref[...] = x_ref[...]

    ms = []
    for b in range(wa_ref.shape[1] // MXU_COLS):
        sl = slice(b * MXU_COLS, (b + 1) * MXU_COLS)
        m = (_dot(oa_ref[...], wa_ref[:, sl]) * sa_ref[:, sl].astype(F32)
             + _dot(ob_ref[...], wb_ref[:, sl]) * sb_ref[:, sl].astype(F32))
        ms.append(m.astype(BF16))
    upd = _dot(ms[0], wo_ref[0:MXU_COLS, :])
    for b in range(1, len(ms)):
        upd += _dot(ms[b], wo_ref[b * MXU_COLS:(b + 1) * MXU_COLS, :])
    x1_ref[...] += upd

    @pl.when(c == pl.num_programs(1) - 1)
    def _():
        h2_ref[...] = _rmsnorm_rows(x1_ref[...], g_ref[...]).astype(h2_ref.dtype)


def _merge_out(x, o_a, o_b, w_pa, w_pb, w_o, g2, p_main, ga_off, gb_off, tm, tc):
    n, d = x.shape
    wa, wb = o_a.shape[1], o_b.shape[1]
    row = lambda i, c: (i, 0)
    return pl.pallas_call(
        _merge_out_kernel,
        grid=(n // tm, d // tc),
        in_specs=[pl.BlockSpec((tm, wa), row), pl.BlockSpec((tm, wb), row),
                  pl.BlockSpec((wa, tc), lambda i, c: (0, c)),
                  pl.BlockSpec((wb, tc), lambda i, c: (0, c)),
                  pl.BlockSpec((tm, tc), lambda i, c: (i, ga_off // tc + c)),
                  pl.BlockSpec((tm, tc), lambda i, c: (i, gb_off // tc + c)),
                  pl.BlockSpec((tc, d), lambda i, c: (c, 0)),
                  pl.BlockSpec((tm, d), row), pl.BlockSpec((1, d), lambda i, c: (0, 0))],
        out_specs=[pl.BlockSpec((tm, d), row), pl.BlockSpec((tm, d), row)],
        out_shape=[jax.ShapeDtypeStruct((n, d), F32), jax.ShapeDtypeStruct((n, d), BF16)],
        compiler_params=_cparams("parallel", "arbitrary"),
        name="merge_out",
    )(o_a, o_b, w_pa, w_pb, p_main, p_main, w_o, x, g2.reshape(1, d))


def _ffn_kernel(h_ref, wg_ref, wu_ref, wd_ref, x1_ref, g_ref, y_ref, *, final_norm):
    c = pl.program_id(1)

    @pl.when(c == 0)
    def _():
        y_ref[...] = x1_ref[...]

    acts = []
    for b in range(wg_ref.shape[1] // MXU_COLS):
        sl = slice(b * MXU_COLS, (b + 1) * MXU_COLS)
        a = _dot(h_ref[...], wg_ref[:, sl])
        u = _dot(h_ref[...], wu_ref[:, sl])
        acts.append((a / (1.0 + jnp.exp(-a)) * u).astype(BF16))
    upd = _dot(acts[0], wd_ref[0:MXU_COLS, :])
    for b in range(1, len(acts)):
        upd += _dot(acts[b], wd_ref[b * MXU_COLS:(b + 1) * MXU_COLS, :])
    y_ref[...] += upd

    if final_norm:
        @pl.when(c == pl.num_programs(1) - 1)
        def _():
            y_ref[...] = _rmsnorm_rows(y_ref[...], g_ref[...])


def _ffn(h2, wg, wu, wd, x1, gf, tm, tc, final_norm):
    n, d = h2.shape
    f = wg.shape[1]
    row = lambda i, c: (i, 0)
    return pl.pallas_call(
        functools.partial(_ffn_kernel, final_norm=final_norm),
        grid=(n // tm, f // tc),
        in_specs=[pl.BlockSpec((tm, d), row),
                  pl.BlockSpec((d, tc), lambda i, c: (0, c)),
                  pl.BlockSpec((d, tc), lambda i, c: (0, c)),
                  pl.BlockSpec((tc, d), lambda i, c: (c, 0)),
                  pl.BlockSpec((tm, d), row, pipeline_mode=pl.Buffered(1)),
                  pl.BlockSpec((1, d), lambda i, c: (0, 0))],
        out_specs=pl.BlockSpec((tm, d), row),
        out_shape=jax.ShapeDtypeStruct((n, d), F32),
        compiler_params=_cparams("parallel", "arbitrary"),
        name="ffn",
    )(h2, wg, wu, wd, x1, gf.reshape(1, d))


class _MainLayout:
    def __init__(self, d_model):
        wa = N_HEADS_A * HEAD_DIM_A
        wb = N_HEADS_R * VAL_DIM_R
        kr = N_HEADS_R * KEY_DIM_R
        qi = N_HEADS_IDX * HEAD_DIM_IDX
        order = [("qa", wa, "rope128", HEAD_DIM_A ** -0.5 * LOG2E), ("vr", wb, "plain", 1.0),
                 ("gr", wb, "silu", 1.0), ("ga", d_model, "sigmoid", 1.0), ("gb", d_model, "sigmoid", 1.0),
                 ("qi", qi, "rope64", 1.0), ("qr", kr, "rope128", 1.0),
                 ("kr", kr, "rope128", KEY_DIM_R ** -0.5)]
        self.off, self.width = {}, {}
        self.order = order
        o = 0
        for name, w, _, _ in order:
            self.off[name], self.width[name] = o, w
            o += w
        self.total = o

    def groups(self, tn):
        out = []
        for name, w, mode, scale in self.order:
            assert self.off[name] % tn == 0 and w % tn == 0
            out.append((self.off[name] // tn, (self.off[name] + w) // tn, mode, scale))
        return tuple(out)

    def block(self, name, width=None):
        width = width or self.width[name]
        assert self.off[name] % width == 0
        return self.off[name] // width


def _split_w_in(w, d_model):
    wa = N_HEADS_A * HEAD_DIM_A
    nk = N_KV_A * HEAD_DIM_A
    wb = N_HEADS_R * VAL_DIM_R
    kr = N_HEADS_R * KEY_DIM_R
    names = ("qa", "ka", "va", "qi", "ki", "wi", "qr", "kr", "vr", "gr", "ga", "gb")
    widths = (wa, nk, nk, N_HEADS_IDX * HEAD_DIM_IDX, HEAD_DIM_IDX, N_HEADS_IDX, kr, kr, wb, wb,
              d_model, d_model)
    parts, o = {}, 0
    for name, wd in zip(names, widths):
        parts[name] = w[:, o:o + wd]
        o += wd
    assert o == w.shape[1]
    return parts


def _tile(n, pref):
    t = min(n, pref)
    assert n % t == 0
    return t


def _dense_pre(x2, pos_rows, lw, lay):
    n = x2.shape[0]
    tm = _tile(n, 1024)
    rope = _rope_tables(pos_rows)
    h = _rmsnorm(x2, lw["g1"], tm)
    tn = 1024
    p_main = _proj_main(h, lw["w_main"], rope, lay.groups(tn), tm, tn)
    kv = _proj_kv(h, lw["w_kv"], rope, _tile(n, 512))
    return p_main, kv


def _dense_post(x2, o_a, o_b, p_main, lw, lay, gf, final_norm):
    n = x2.shape[0]
    x1, h2 = _merge_out(x2, o_a, o_b, lw["w_pa"], lw["w_pb"], lw["w_o"], lw["g2"], p_main,
                        lay.off["ga"], lay.off["gb"], _tile(n, 512), 512)
    return _ffn(h2, lw["wg"], lw["wu"], lw["wd"], x1, gf, _tile(n, 1024), 512, final_norm)


def _prompt_layer(xp, lw, lay, gf, final_norm):
    b, t, d = xp.shape
    x2 = xp.reshape(b * t, d)
    p_main, (ka, va, ki, wt, ka_bf, vt_bf, kia, kib) = _dense_pre(x2, jnp.arange(t, dtype=jnp.int32), lw, lay)
    top_k = min(TOPK_MAX, t // 4)
    tq, ck = _tile(t, 256), _tile(t, 512)
    mask_t = _index_mask(p_main, lay.block("qi"), wt, kia.reshape(b, t, LANES), kib.reshape(b, t, LANES),
                         tq=tq, ck=ck, l_valid=t, pos_base=0, causal=True, top_k=top_k)
    nk = N_KV_A * HEAD_DIM_A
    o_a = _attention(p_main, ka_bf.reshape(b, t, nk), vt_bf, mask_t, tq=tq, tk=ck, causal=True)
    state0 = jnp.zeros((b, N_HEADS_R, KEY_DIM_R, VAL_DIM_R), F32)
    cols = (lay.block("qr"), lay.block("kr"), lay.block("vr"), lay.block("gr"))
    o_b, st = _retention(p_main, cols, state0, _tile(t, 256))
    y = _dense_post(x2, o_a, o_b, p_main, lw, lay, gf, final_norm)
    return (y.reshape(b, t, d), ka.reshape(b, t, N_KV_A, HEAD_DIM_A), va.reshape(b, t, N_KV_A, HEAD_DIM_A),
            ki.reshape(b, t, HEAD_DIM_IDX), st)


def _sample_layer(xs, cache_k, cache_v, cache_i, state, lw, lay, gf, final_norm):
    b, t, d = xs.shape
    past = cache_k.shape[1]
    nk = N_KV_A * HEAD_DIM_A
    x2 = xs.reshape(b * t, d)
    pos = jnp.tile(past + jnp.arange(t, dtype=jnp.int32), b)
    p_main, (ka, va, ki, wt, ka_bf, vt_bf, kia, kib) = _dense_pre(x2, pos, lw, lay)

    tq = LANES
    l_valid = past + t
    tk = 3 * LANES
    lp = -(-l_valid // tk) * tk
    pq = jnp.pad(p_main.reshape(b, t, -1), ((0, 0), (0, tq - t), (0, 0))).reshape(b * tq, -1)
    wtq = jnp.pad(wt.reshape(N_HEADS_IDX, b, t), ((0, 0), (0, 0), (0, tq - t))).reshape(N_HEADS_IDX, b * tq)
    kpad = ((0, 0), (0, lp - l_valid), (0, 0))
    ci = cache_i.astype(BF16)
    zi = jnp.zeros_like(ci)
    kia_all = jnp.pad(jnp.concatenate([jnp.concatenate([ci, zi], -1), kia.reshape(b, t, LANES)], 1), kpad)
    kib_all = jnp.pad(jnp.concatenate([jnp.concatenate([zi, ci], -1), kib.reshape(b, t, LANES)], 1), kpad)
    k_all = jnp.pad(jnp.concatenate([cache_k.reshape(b, past, nk).astype(BF16), ka_bf.reshape(b, t, nk)], 1), kpad)
    vt_new = vt_bf.reshape(N_KV_A, VT_ROWS, b, t)
    vt_cache = cache_v.reshape(b, past, N_KV_A, HEAD_DIM_A).transpose(2, 3, 0, 1).astype(BF16)
    ones_rows = jnp.zeros((N_KV_A, VT_ROWS - HEAD_DIM_A, b, past), BF16).at[:, 0].set(1.0)
    vt_all = jnp.concatenate([jnp.concatenate([vt_cache, ones_rows], 1), vt_new], 3)
    vt_all = jnp.pad(vt_all, ((0, 0), (0, 0), (0, 0), (0, lp - l_valid))).reshape(N_KV_A * VT_ROWS, b * lp)

    top_k = min(TOPK_MAX, l_valid // 4)
    mask_t = _index_mask(pq, lay.block("qi"), wtq, kia_all, kib_all, tq=tq, ck=tk, l_valid=l_valid,
                         pos_base=past, causal=False, top_k=top_k)
    o_a = _attention(pq, k_all, vt_all, mask_t, tq=tq, tk=tk, causal=False)
    o_a = o_a.reshape(b, tq, -1)[:, :t].reshape(b * t, -1)
    cols = (lay.block("qr"), lay.block("kr"), lay.block("vr"), lay.block("gr"))
    o_b, st = _retention(p_main, cols, state.astype(F32), t)
    y = _dense_post(x2, o_a, o_b, p_main, lw, lay, gf, final_norm)
    return (y.reshape(b, t, d), ka.reshape(b, t, N_KV_A, HEAD_DIM_A), va.reshape(b, t, N_KV_A, HEAD_DIM_A),
            ki.reshape(b, t, HEAD_DIM_IDX), st)


def kernel(x_prompt, x_sample, cache_k, cache_v, cache_idx_k, state_ret, norm1_g, w_in, w_pa, w_pb, w_o,
           norm2_g, w_ffn_gate, w_ffn_up, w_ffn_down, norm_f_g):
    assert HEAD_DIM_A == LANES and HEAD_DIM_IDX * 2 == LANES and N_HEADS_IDX % 2 == 0
    depth, d_model = norm1_g.shape
    lay = _MainLayout(d_model)
    xp, xs = x_prompt, x_sample
    outs = [[] for _ in range(8)]
    for l in range(depth):
        parts = _split_w_in(w_in[l], d_model)
        pad = jnp.zeros((d_model, LANES - HEAD_DIM_IDX - N_HEADS_IDX), F32)
        lw = {
            "g1": norm1_g[l], "g2": norm2_g[l],
            "w_main": jnp.concatenate([parts[name] for name, _, _, _ in lay.order], axis=1).astype(BF16),
            "w_kv": jnp.concatenate([parts["ka"], parts["va"], parts["ki"], parts["wi"], pad], axis=1).astype(BF16),
            "w_pa": w_pa[l].astype(BF16), "w_pb": w_pb[l].astype(BF16), "w_o": w_o[l].astype(BF16),
            "wg": w_ffn_gate[l].astype(BF16), "wu": w_ffn_up[l].astype(BF16), "wd": w_ffn_down[l].astype(BF16),
        }
        last = l == depth - 1
        xp, kp, vp, ip, sp = _prompt_layer(xp, lw, lay, norm_f_g, last)
        xs, ks, vs, isl, ss = _sample_layer(xs, cache_k[l], cache_v[l], cache_idx_k[l], state_ret[l],
                                            lw, lay, norm_f_g, last)
        for lst, val in zip(outs, (kp, vp, ip, sp, ks, vs, isl, ss)):
            lst.append(val)
    stacked = [jnp.stack(o) for o in outs]
    return (xp, xs, stacked[0], stacked[1], stacked[2], stacked[3].astype(x_prompt.dtype),
            stacked[4], stacked[5], stacked[6], stacked[7].astype(state_ret.dtype))
```

```python
import functools
import math

import jax
import jax.numpy as jnp
from jax import lax
from jax.experimental import pallas as pl
from jax.experimental.pallas import tpu as pltpu

F32 = jnp.float32
BF16 = jnp.bfloat16

CHUNK = 64
EPS = 1e-6
ROPE_THETA = 10000.0
N_HEADS_A = 16
N_KV_A = 4
HEAD_DIM_A = 128
N_HEADS_IDX = 16
HEAD_DIM_IDX = 64
TOPK_MAX = 256
N_HEADS_R = 8
KEY_DIM_R = 128
VAL_DIM_R = 256

LANES = 128
MXU_COLS = 256
V7X_VMEM_BYTES = 64 * 1024 * 1024
VMEM_LIMIT = V7X_VMEM_BYTES - 8 * 1024 * 1024
BF16_SUBLANES = 16
VT_ROWS = HEAD_DIM_A + BF16_SUBLANES
LOG2E = math.log2(math.e)
MAX_BISECT_ITERS = 64
NEG_BIG = -1e30


def _cparams(*sem):
    return pltpu.CompilerParams(dimension_semantics=sem, vmem_limit_bytes=VMEM_LIMIT)


def _dot(a, b):
    return jnp.dot(a, b, preferred_element_type=F32)


def _dot_nt(a, b):
    return lax.dot_general(a, b, (((1,), (1,)), ((), ())), preferred_element_type=F32)


def _rmsnorm_rows(x, g):
    return x * lax.rsqrt(jnp.mean(x * x, axis=-1, keepdims=True) + EPS) * g


def _rope_tables(pos):
    posf = pos.astype(F32)[:, None]

    def cs(half):
        inv_freq = ROPE_THETA ** (-jnp.arange(half, dtype=F32) / half)
        ang = posf * inv_freq[None, :]
        return jnp.cos(ang), jnp.sin(ang)

    c, s = cs(HEAD_DIM_A // 2)
    c128 = jnp.concatenate([c, c], axis=1)
    s128 = jnp.concatenate([-s, s], axis=1)
    c, s = cs(HEAD_DIM_IDX // 2)
    z = jnp.zeros_like(s)
    c64 = jnp.tile(c, (1, 4))
    a64 = jnp.tile(jnp.concatenate([-s, z], axis=1), (1, 2))
    b64 = jnp.tile(jnp.concatenate([z, s], axis=1), (1, 2))
    return jnp.stack([c128, s128, c64, a64, b64])


def _rope128(x, rope_ref):
    return x * rope_ref[0] + pltpu.roll(x, 64, 1) * rope_ref[1]


def _rope64(x, rope_ref):
    return (x * rope_ref[2] + pltpu.roll(x, 96, 1) * rope_ref[3]
            + pltpu.roll(x, 32, 1) * rope_ref[4])


def _proj_main_kernel(x_ref, g_ref, w_ref, rope_ref, o_ref, h_ref, *, groups, tn):
    j = pl.program_id(1)

    @pl.when(j == 0)
    def _():
        h_ref[...] = _rmsnorm_rows(x_ref[...], g_ref[...]).astype(h_ref.dtype)

    for lo, hi, mode, scale in groups:
        @pl.when(jnp.logical_and(j >= lo, j < hi))
        def _(mode=mode, scale=scale):
            for b in range(tn // MXU_COLS):
                acc = _dot(h_ref[...], w_ref[:, b * MXU_COLS:(b + 1) * MXU_COLS])
                for c in range(MXU_COLS // LANES):
                    x = acc[:, c * LANES:(c + 1) * LANES]
                    if mode == "rope128":
                        y = _rope128(x, rope_ref)
                    elif mode == "rope64":
                        y = _rope64(x, rope_ref)
                    elif mode == "sigmoid":
                        y = 1.0 / (1.0 + jnp.exp(-x))
                    elif mode == "silu":
                        y = x / (1.0 + jnp.exp(-x))
                    else:
                        y = x
                    if scale != 1.0:
                        y = y * scale
                    col = b * MXU_COLS + c * LANES
                    o_ref[:, col:col + LANES] = y.astype(o_ref.dtype)


def _proj_main(x, g1, w_main, rope, groups, tm, tn):
    n, d = x.shape
    p = w_main.shape[1]
    n_pos_tiles = rope.shape[1] // tm
    return pl.pallas_call(
        functools.partial(_proj_main_kernel, groups=groups, tn=tn),
        grid=(n // tm, p // tn),
        in_specs=[pl.BlockSpec((tm, d), lambda i, j: (i, 0), pipeline_mode=pl.Buffered(1)),
                  pl.BlockSpec((1, d), lambda i, j: (0, 0)),
                  pl.BlockSpec((d, tn), lambda i, j: (0, j)),
                  pl.BlockSpec((5, tm, LANES), lambda i, j: (0, i % n_pos_tiles, 0))],
        out_specs=[pl.BlockSpec((tm, tn), lambda i, j: (i, j)),
                   pl.BlockSpec((tm, d), lambda i, j: (i, 0))],
        out_shape=[jax.ShapeDtypeStruct((n, p), BF16), jax.ShapeDtypeStruct((n, d), BF16)],
        compiler_params=_cparams("parallel", "arbitrary"),
        name="proj_main",
    )(x, g1.reshape(1, d), w_main, rope)


def _proj_kv_kernel(h_ref, w_ref, rope_ref, ka_ref, va_ref, ki_ref, wt_ref,
                    kab_ref, vtb_ref, kia_ref, kib_ref, acc_ref, *, nk, idx_scale):
    acc_ref[...] = _dot(h_ref[...], w_ref[...])
    for c in range(nk // LANES):
        sl = slice(c * LANES, (c + 1) * LANES)
        y = _rope128(acc_ref[:, sl], rope_ref)
        ka_ref[:, sl] = y
        kab_ref[:, sl] = y.astype(BF16)
    va_ref[...] = acc_ref[:, nk:2 * nk]
    tm = acc_ref.shape[0]
    ones_rows = jnp.where(lax.broadcasted_iota(jnp.int32, (VT_ROWS - HEAD_DIM_A, tm), 0) == 0, 1.0, 0.0)
    for g in range(N_KV_A):
        vg = acc_ref[:, nk + g * HEAD_DIM_A:nk + (g + 1) * HEAD_DIM_A]
        vtb_ref[g * VT_ROWS:g * VT_ROWS + HEAD_DIM_A, :] = vg.T.astype(BF16)
        vtb_ref[g * VT_ROWS + HEAD_DIM_A:(g + 1) * VT_ROWS, :] = ones_rows.astype(BF16)
    z = acc_ref[:, 2 * nk:2 * nk + LANES]
    y = _rope64(z, rope_ref)
    ki_ref[...] = y[:, :HEAD_DIM_IDX]
    lane = lax.broadcasted_iota(jnp.int32, y.shape, 1)
    ya = jnp.where(lane < HEAD_DIM_IDX, y, 0.0)
    kia_ref[...] = ya.astype(BF16)
    kib_ref[...] = pltpu.roll(ya, HEAD_DIM_IDX, 1).astype(BF16)
    wt_ref[...] = (z * idx_scale).T[HEAD_DIM_IDX:HEAD_DIM_IDX + N_HEADS_IDX, :]


def _proj_kv(h, w_kv, rope, tm):
    n, d = h.shape
    nk = N_KV_A * HEAD_DIM_A
    pw = w_kv.shape[1]
    n_pos_tiles = rope.shape[1] // tm
    idx_scale = (HEAD_DIM_IDX ** -0.5) * (N_HEADS_IDX ** -0.5)
    row = lambda i: (i, 0)
    return pl.pallas_call(
        functools.partial(_proj_kv_kernel, nk=nk, idx_scale=idx_scale),
        grid=(n // tm,),
        in_specs=[pl.BlockSpec((tm, d), row),
                  pl.BlockSpec((d, pw), lambda i: (0, 0)),
                  pl.BlockSpec((5, tm, LANES), lambda i: (0, i % n_pos_tiles, 0))],
        out_specs=[pl.BlockSpec((tm, nk), row), pl.BlockSpec((tm, nk), row),
                   pl.BlockSpec((tm, HEAD_DIM_IDX), row),
                   pl.BlockSpec((N_HEADS_IDX, tm), lambda i: (0, i)),
                   pl.BlockSpec((tm, nk), row),
                   pl.BlockSpec((N_KV_A * VT_ROWS, tm), lambda i: (0, i)),
                   pl.BlockSpec((tm, LANES), row), pl.BlockSpec((tm, LANES), row)],
        out_shape=[jax.ShapeDtypeStruct((n, nk), F32), jax.ShapeDtypeStruct((n, nk), F32),
                   jax.ShapeDtypeStruct((n, HEAD_DIM_IDX), F32),
                   jax.ShapeDtypeStruct((N_HEADS_IDX, n), F32),
                   jax.ShapeDtypeStruct((n, nk), BF16),
                   jax.ShapeDtypeStruct((N_KV_A * VT_ROWS, n), BF16),
                   jax.ShapeDtypeStruct((n, LANES), BF16), jax.ShapeDtypeStruct((n, LANES), BF16)],
        scratch_shapes=[pltpu.VMEM((tm, pw), F32)],
        compiler_params=_cparams("parallel"),
        name="proj_kv",
    )(h, w_kv, rope)


def _index_kernel(qi_ref, wt_ref, kia_ref, kib_ref, m_ref, s_ref, *,
                  tq, ck, n_ck_total, l_valid, pos_base, causal, top_k):
    i = pl.program_id(1)
    if causal:
        n_c = (i * tq + tq + ck - 1) // ck
    else:
        n_c = n_ck_total
    qpos = pos_base + i * tq + lax.broadcasted_iota(jnp.int32, (1, tq), 1)
    lim = jnp.minimum(qpos - lax.rem(qpos, CHUNK) + CHUNK, l_valid)
    limf = lim.astype(F32)
    kprime = jnp.minimum(float(top_k), limf)
    inf = jnp.float32(jnp.inf)

    def key_index(off):
        return off + lax.broadcasted_iota(jnp.int32, (ck, tq), 0)

    def score_chunk(c, carry):
        rmax, rmin = carry
        off = pl.multiple_of(c * ck, ck)
        ka = kia_ref[pl.ds(off, ck), :]
        kb = kib_ref[pl.ds(off, ck), :]
        acc = jnp.zeros((ck, tq), F32)
        for p in range(N_HEADS_IDX // 2):
            qp = qi_ref[:, p * LANES:(p + 1) * LANES]
            acc += jnp.maximum(_dot_nt(ka, qp), 0.0) * wt_ref[2 * p:2 * p + 1, :]
            acc += jnp.maximum(_dot_nt(kb, qp), 0.0) * wt_ref[2 * p + 1:2 * p + 2, :]
        adm = key_index(off) < lim
        s_ref[pl.ds(off, ck), :] = jnp.where(adm, acc, -inf)
        rmax = jnp.maximum(rmax, jnp.max(jnp.where(adm, acc, -inf), axis=0, keepdims=True))
        rmin = jnp.minimum(rmin, jnp.min(jnp.where(adm, acc, inf), axis=0, keepdims=True))
        return rmax, rmin

    rmax, rmin = lax.fori_loop(0, n_c, score_chunk,
                               (jnp.full((1, tq), -inf, F32), jnp.full((1, tq), inf, F32)))

    fold_rows = min(ck, 64)

    def fold(ind):
        return ind.reshape(ck // fold_rows, fold_rows, tq).sum(axis=0)

    def count_ge(th):
        def body(c, acc):
            off = pl.multiple_of(c * ck, ck)
            return acc + fold(jnp.where(s_ref[pl.ds(off, ck), :] >= th, 1.0, 0.0))
        acc = lax.fori_loop(0, n_c, body, jnp.zeros((fold_rows, tq), F32))
        return acc.sum(axis=0, keepdims=True)

    def cond(st):
        it, lo, hi, cnt, stuck = st
        active = jnp.where(jnp.logical_and(cnt > kprime, stuck < 0.5), 1.0, 0.0)
        return jnp.logical_and(it < MAX_BISECT_ITERS, jnp.max(active) > 0.5)

    def body(st):
        it, lo, hi, cnt, stuck = st
        open_top = hi == inf
        mid = lo + 0.5 * (jnp.where(open_top, rmax, hi) - lo)
        mid = jnp.where(jnp.logical_and(open_top, mid <= lo), rmax, mid)
        c = count_ge(mid)
        ge = c >= kprime
        no_progress = jnp.logical_or(mid <= lo, mid >= hi)
        return (it + 1, jnp.where(ge, mid, lo), jnp.where(ge, hi, mid), jnp.where(ge, c, cnt),
                jnp.where(no_progress, 1.0, stuck))

    _, lo, hi, cnt, _ = lax.while_loop(
        cond, body, (jnp.int32(0), rmin, jnp.full((1, tq), inf, F32), limf, jnp.zeros((1, tq), F32)))

    unresolved = jnp.max(jnp.where(cnt > kprime, 1.0, 0.0)) > 0.5

    @pl.when(jnp.logical_not(unresolved))
    def _():
        def body(c, _):
            off = pl.multiple_of(c * ck, ck)
            m_ref[pl.ds(off, ck), :] = jnp.where(s_ref[pl.ds(off, ck), :] >= lo, 0.0, -inf).astype(m_ref.dtype)
            return 0
        lax.fori_loop(0, n_c, body, 0)

    @pl.when(unresolved)
    def _():
        need = kprime - count_ge(hi)

        def in_tie(blk):
            return jnp.logical_and(blk >= lo, blk < hi)

        def count_tie_below(jcut):
            def body(c, acc):
                off = pl.multiple_of(c * ck, ck)
                blk = s_ref[pl.ds(off, ck), :]
                e = jnp.logical_and(in_tie(blk), key_index(off).astype(F32) < jcut)
                return acc + fold(jnp.where(e, 1.0, 0.0))
            acc = lax.fori_loop(0, n_c, body, jnp.zeros((fold_rows, tq), F32))
            return acc.sum(axis=0, keepdims=True)

        def jbody(_, st):
            jlo, jhi = st
            mid = jnp.floor(0.5 * (jlo + jhi))
            ok = count_tie_below(mid) >= need
            return jnp.where(ok, jlo, mid + 1.0), jnp.where(ok, mid, jhi)

        n_total = n_ck_total * ck
        _, jcut = lax.fori_loop(0, int(math.ceil(math.log2(n_total + 1))), jbody,
                                (jnp.zeros((1, tq), F32), jnp.full((1, tq), float(n_total), F32)))

        def body(c, _):
            off = pl.multiple_of(c * ck, ck)
            blk = s_ref[pl.ds(off, ck), :]
            e = jnp.logical_and(in_tie(blk), key_index(off).astype(F32) < jcut)
            keep = jnp.logical_or(blk >= hi, e)
            m_ref[pl.ds(off, ck), :] = jnp.where(keep, 0.0, -inf).astype(m_ref.dtype)
            return 0
        lax.fori_loop(0, n_c, body, 0)

    def zero_body(c, _):
        off = pl.multiple_of(c * ck, ck)
        m_ref[pl.ds(off, ck), :] = jnp.full((ck, tq), -inf, m_ref.dtype)
        return 0
    lax.fori_loop(n_c, n_ck_total, zero_body, 0)


def _index_mask(p_main, qi_col_block, wt, kia, kib, *, tq, ck, l_valid, pos_base, causal, top_k):
    b, lp, _ = kia.shape
    nq = p_main.shape[0] // (b * tq)
    qw = N_HEADS_IDX * HEAD_DIM_IDX
    return pl.pallas_call(
        functools.partial(_index_kernel, tq=tq, ck=ck, n_ck_total=lp // ck, l_valid=l_valid,
                          pos_base=pos_base, causal=causal, top_k=top_k),
        grid=(b, nq),
        in_specs=[pl.BlockSpec((tq, qw), lambda bi, i: (bi * nq + i, qi_col_block)),
                  pl.BlockSpec((N_HEADS_IDX, tq), lambda bi, i: (0, bi * nq + i)),
                  pl.BlockSpec((None, lp, LANES), lambda bi, i: (bi, 0, 0)),
                  pl.BlockSpec((None, lp, LANES), lambda bi, i: (bi, 0, 0))],
        out_specs=pl.BlockSpec((None, lp, tq), lambda bi, i: (bi, 0, i)),
        out_shape=jax.ShapeDtypeStruct((b, lp, nq * tq), BF16),
        scratch_shapes=[pltpu.VMEM((lp, tq), F32)],
        compiler_params=_cparams("parallel", "parallel"),
        name="index_mask",
    )(p_main, wt, kia, kib)


def _attn_kernel(q_ref, k_ref, vt_ref, m_ref, o_ref, acc_ref, sa_ref, sb_ref, *,
                 tq, tk, n_kt_total, causal, group, kvb):
    i = pl.program_id(1)
    heads = kvb * group
    if causal:
        n_kt = (i * tq + tq + tk - 1) // tk
    else:
        n_kt = n_kt_total
    acc_ref[...] = jnp.zeros_like(acc_ref)

    def scores(kt, s_ref):
        off = pl.multiple_of(kt * tk, tk)
        for h in range(heads):
            kv = h // group
            k = k_ref[pl.ds(off, tk), kv * HEAD_DIM_A:(kv + 1) * HEAD_DIM_A]
            s_ref[h] = _dot_nt(k, q_ref[:, h * HEAD_DIM_A:(h + 1) * HEAD_DIM_A])

    def softmax_pv(kt, s_ref, ms):
        off = pl.multiple_of(kt * tk, tk)
        bias = m_ref[pl.ds(off, tk), :].astype(F32)
        out = []
        for h in range(heads):
            kv = h // group
            vt = vt_ref[kv * VT_ROWS:(kv + 1) * VT_ROWS, pl.ds(off, tk)]
            s = s_ref[h] + bias
            m_new = jnp.maximum(ms[h], jnp.max(s, axis=0, keepdims=True))
            alpha = jnp.exp2(ms[h] - m_new)
            p = jnp.exp2(s - m_new).astype(BF16)
            acc_ref[h] = acc_ref[h] * alpha + _dot(vt, p)
            out.append(m_new)
        return tuple(out)

    def body(j, ms):
        scores(2 * j + 1, sb_ref)
        ms = softmax_pv(2 * j, sa_ref, ms)
        scores(2 * j + 2, sa_ref)
        return softmax_pv(2 * j + 1, sb_ref, ms)

    scores(0, sa_ref)
    n_pairs = (n_kt - 1) // 2
    ms = lax.fori_loop(0, n_pairs, body, tuple(jnp.full((1, tq), NEG_BIG, F32) for _ in range(heads)))
    even = n_kt - 1 > 2 * n_pairs

    @pl.when(even)
    def _():
        scores(2 * n_pairs + 1, sb_ref)

    ms = softmax_pv(2 * n_pairs, sa_ref, ms)

    @pl.when(even)
    def _():
        softmax_pv(2 * n_pairs + 1, sb_ref, ms)

    for h in range(heads):
        a = acc_ref[h]
        o = a[:HEAD_DIM_A] / a[HEAD_DIM_A:HEAD_DIM_A + 1]
        o_ref[:, h * HEAD_DIM_A:(h + 1) * HEAD_DIM_A] = o.T.astype(o_ref.dtype)


def _attention(p_main, k_bf, vt_bf, mask_t, *, tq, tk, causal, kvb):
    b, lp, _ = k_bf.shape
    nq = mask_t.shape[2] // tq
    group = N_HEADS_A // N_KV_A
    heads = kvb * group
    gw = heads * HEAD_DIM_A
    return pl.pallas_call(
        functools.partial(_attn_kernel, tq=tq, tk=tk, n_kt_total=lp // tk, causal=causal, group=group,
                          kvb=kvb),
        grid=(b, nq, N_KV_A // kvb),
        in_specs=[pl.BlockSpec((tq, gw), lambda bi, i, g: (bi * nq + i, g)),
                  pl.BlockSpec((None, lp, kvb * HEAD_DIM_A), lambda bi, i, g: (bi, 0, g)),
                  pl.BlockSpec((kvb * VT_ROWS, lp), lambda bi, i, g: (g, bi)),
                  pl.BlockSpec((None, lp, tq), lambda bi, i, g: (bi, 0, i))],
        out_specs=pl.BlockSpec((tq, gw), lambda bi, i, g: (bi * nq + i, g)),
        out_shape=jax.ShapeDtypeStruct((b * nq * tq, N_HEADS_A * HEAD_DIM_A), BF16),
        scratch_shapes=[pltpu.VMEM((heads, VT_ROWS, tq), F32),
                        pltpu.VMEM((heads, tk, tq), F32), pltpu.VMEM((heads, tk, tq), F32)],
        compiler_params=_cparams("parallel", "parallel", "arbitrary"),
        name="attention",
    )(p_main, k_bf, vt_bf, mask_t)


def _retention_tables(c):
    log_gamma = jnp.log1p(-(2.0 ** (-5.0 - jnp.arange(N_HEADS_R, dtype=F32))))
    idx = jnp.arange(c, dtype=F32)
    diff = idx[:, None] - idx[None, :]
    decay = jnp.where(diff[None] >= 0,
                      jnp.exp(jnp.maximum(diff, 0.0)[None] * log_gamma[:, None, None]), 0.0)
    cross = jnp.exp((idx + 1.0)[None, :] * log_gamma[:, None])
    kdec = jnp.exp((c - 1.0 - idx)[None, :] * log_gamma[:, None])
    full = jnp.exp(c * log_gamma)
    bc = lambda a, w: jnp.broadcast_to(a[..., None], a.shape + (w,))
    return decay, bc(cross, KEY_DIM_R), bc(kdec, KEY_DIM_R), bc(full[:, None], VAL_DIM_R)


def _retention_kernel(q_ref, k_ref, v_ref, g_ref, dec_ref, qsc_ref, ksc_ref, gc_ref, s0_ref,
                      o_ref, st_ref):
    @pl.when(pl.program_id(1) == 0)
    def _():
        st_ref[...] = s0_ref[...]

    for h in range(N_HEADS_R):
        ks = slice(h * KEY_DIM_R, (h + 1) * KEY_DIM_R)
        vs = slice(h * VAL_DIM_R, (h + 1) * VAL_DIM_R)
        q = q_ref[:, ks]
        k = k_ref[:, ks]
        v = v_ref[:, vs]
        st = st_ref[h]
        inner = _dot_nt(q, k) * dec_ref[h]
        qd = (q.astype(F32) * qsc_ref[h]).astype(BF16)
        o = _dot(inner.astype(BF16), v) + _dot(qd, st.astype(BF16))
        kdt = (k.astype(F32) * ksc_ref[h]).T.astype(BF16)
        st_ref[h] = st * gc_ref[h] + _dot(kdt, v)
        o = o * lax.rsqrt(jnp.mean(o * o, axis=-1, keepdims=True) + EPS)
        o_ref[:, vs] = (o * g_ref[:, vs].astype(F32)).astype(o_ref.dtype)


def _retention(p_main, cols, state0, c):
    b = state0.shape[0]
    nc = p_main.shape[0] // (b * c)
    kw = N_HEADS_R * KEY_DIM_R
    vw = N_HEADS_R * VAL_DIM_R
    dec, qsc, ksc, gcs = _retention_tables(c)
    qb, kb, vb, gb = cols
    const3 = lambda bi, ci: (0, 0, 0)
    st_spec = pl.BlockSpec((None, N_HEADS_R, KEY_DIM_R, VAL_DIM_R), lambda bi, ci: (bi, 0, 0, 0))
    return pl.pallas_call(
        _retention_kernel,
        grid=(b, nc),
        in_specs=[pl.BlockSpec((c, kw), lambda bi, ci: (bi * nc + ci, qb)),
                  pl.BlockSpec((c, kw), lambda bi, ci: (bi * nc + ci, kb)),
                  pl.BlockSpec((c, vw), lambda bi, ci: (bi * nc + ci, vb)),
                  pl.BlockSpec((c, vw), lambda bi, ci: (bi * nc + ci, gb)),
                  pl.BlockSpec(dec.shape, const3), pl.BlockSpec(qsc.shape, const3),
                  pl.BlockSpec(ksc.shape, const3), pl.BlockSpec(gcs.shape, const3),
                  st_spec],
        out_specs=[pl.BlockSpec((c, vw), lambda bi, ci: (bi * nc + ci, 0)), st_spec],
        out_shape=[jax.ShapeDtypeStruct((b * nc * c, vw), BF16),
                   jax.ShapeDtypeStruct(state0.shape, F32)],
        compiler_params=_cparams("parallel", "arbitrary"),
        name="retention",
    )(p_main, p_main, p_main, p_main, dec, qsc, ksc, gcs, state0)


def _merge_out_kernel(oa_ref, ob_ref, wa_ref, wb_ref, sa_ref, sb_ref, wo_ref, x_ref, x1_ref):
    c = pl.program_id(1)

    @pl.when(c == 0)
    def _():
        x1_ref[...] = x_ref[...]

    ms = []
    for b in range(wa_ref.shape[1] // MXU_COLS):
        sl = slice(b * MXU_COLS, (b + 1) * MXU_COLS)
        m = (_dot(oa_ref[...], wa_ref[:, sl]) * sa_ref[:, sl].astype(F32)
             + _dot(ob_ref[...], wb_ref[:, sl]) * sb_ref[:, sl].astype(F32))
        ms.append(m.astype(BF16))
    upd = _dot(ms[0], wo_ref[0:MXU_COLS, :])
    for b in range(1, len(ms)):
        upd += _dot(ms[b], wo_ref[b * MXU_COLS:(b + 1) * MXU_COLS, :])
    x1_ref[...] += upd


def _merge_out(x, o_a, o_b, w_pa, w_pb, w_o, p_main, ga_off, gb_off, tm, tc):
    n, d = x.shape
    wa, wb = o_a.shape[1], o_b.shape[1]
    row = lambda i, c: (i, 0)
    once = pl.Buffered(1)
    return pl.pallas_call(
        _merge_out_kernel,
        grid=(n // tm, d // tc),
        in_specs=[pl.BlockSpec((tm, wa), row, pipeline_mode=once),
                  pl.BlockSpec((tm, wb), row, pipeline_mode=once),
                  pl.BlockSpec((wa, tc), lambda i, c: (0, c)),
                  pl.BlockSpec((wb, tc), lambda i, c: (0, c)),
                  pl.BlockSpec((tm, tc), lambda i, c: (i, ga_off // tc + c)),
                  pl.BlockSpec((tm, tc), lambda i, c: (i, gb_off // tc + c)),
                  pl.BlockSpec((tc, d), lambda i, c: (c, 0)),
                  pl.BlockSpec((tm, d), row, pipeline_mode=once)],
        out_specs=pl.BlockSpec((tm, d), row),
        out_shape=jax.ShapeDtypeStruct((n, d), F32),
        compiler_params=_cparams("parallel", "arbitrary"),
        name="merge_out",
    )(o_a, o_b, w_pa, w_pb, p_main, p_main, w_o, x)


def _ffn_kernel(x1_ref, g2_ref, wg_ref, wu_ref, wd_ref, g_ref, y_ref, h_ref, *, final_norm):
    c = pl.program_id(1)

    @pl.when(c == 0)
    def _():
        x1 = x1_ref[...]
        y_ref[...] = x1
        h_ref[...] = _rmsnorm_rows(x1, g2_ref[...]).astype(h_ref.dtype)

    acts = []
    for b in range(wg_ref.shape[1] // MXU_COLS):
        sl = slice(b * MXU_COLS, (b + 1) * MXU_COLS)
        a = _dot(h_ref[...], wg_ref[:, sl])
        u = _dot(h_ref[...], wu_ref[:, sl])
        acts.append((a / (1.0 + jnp.exp(-a)) * u).astype(BF16))
    upd = _dot(acts[0], wd_ref[0:MXU_COLS, :])
    for b in range(1, len(acts)):
        upd += _dot(acts[b], wd_ref[b * MXU_COLS:(b + 1) * MXU_COLS, :])
    y_ref[...] += upd

    if final_norm:
        @pl.when(c == pl.num_programs(1) - 1)
        def _():
            y_ref[...] = _rmsnorm_rows(y_ref[...], g_ref[...])


def _ffn(x1, g2, wg, wu, wd, gf, tm, tc, final_norm):
    n, d = x1.shape
    f = wg.shape[1]
    row = lambda i, c: (i, 0)
    vec = pl.BlockSpec((1, d), lambda i, c: (0, 0))
    return pl.pallas_call(
        functools.partial(_ffn_kernel, final_norm=final_norm),
        grid=(n // tm, f // tc),
        in_specs=[pl.BlockSpec((tm, d), row, pipeline_mode=pl.Buffered(1)), vec,
                  pl.BlockSpec((d, tc), lambda i, c: (0, c)),
                  pl.BlockSpec((d, tc), lambda i, c: (0, c)),
                  pl.BlockSpec((tc, d), lambda i, c: (c, 0)), vec],
        out_specs=pl.BlockSpec((tm, d), row),
        out_shape=jax.ShapeDtypeStruct((n, d), F32),
        scratch_shapes=[pltpu.VMEM((tm, d), BF16)],
        compiler_params=_cparams("parallel", "arbitrary"),
        name="ffn",
    )(x1, g2.reshape(1, d), wg, wu, wd, gf.reshape(1, d))


class _MainLayout:
    def __init__(self, d_model):
        wa = N_HEADS_A * HEAD_DIM_A
        wb = N_HEADS_R * VAL_DIM_R
        kr = N_HEADS_R * KEY_DIM_R
        qi = N_HEADS_IDX * HEAD_DIM_IDX
        order = [("qa", wa, "rope128", HEAD_DIM_A ** -0.5 * LOG2E), ("vr", wb, "plain", 1.0),
                 ("gr", wb, "silu", 1.0), ("ga", d_model, "sigmoid", 1.0), ("gb", d_model, "sigmoid", 1.0),
                 ("qi", qi, "rope64", 1.0), ("qr", kr, "rope128", 1.0),
                 ("kr", kr, "rope128", KEY_DIM_R ** -0.5)]
        self.off, self.width = {}, {}
        self.order = order
        o = 0
        for name, w, _, _ in order:
            self.off[name], self.width[name] = o, w
            o += w
        self.total = o

    def groups(self, tn):
        out = []
        for name, w, mode, scale in self.order:
            assert self.off[name] % tn == 0 and w % tn == 0
            out.append((self.off[name] // tn, (self.off[name] + w) // tn, mode, scale))
        return tuple(out)

    def block(self, name, width=None):
        width = width or self.width[name]
        assert self.off[name] % width == 0
        return self.off[name] // width


def _split_w_in(w, d_model):
    wa = N_HEADS_A * HEAD_DIM_A
    nk = N_KV_A * HEAD_DIM_A
    wb = N_HEADS_R * VAL_DIM_R
    kr = N_HEADS_R * KEY_DIM_R
    names = ("qa", "ka", "va", "qi", "ki", "wi", "qr", "kr", "vr", "gr", "ga", "gb")
    widths = (wa, nk, nk, N_HEADS_IDX * HEAD_DIM_IDX, HEAD_DIM_IDX, N_HEADS_IDX, kr, kr, wb, wb,
              d_model, d_model)
    parts, o = {}, 0
    for name, wd in zip(names, widths):
        parts[name] = w[:, o:o + wd]
        o += wd
    assert o == w.shape[1]
    return parts


def _tile(n, pref):
    t = min(n, pref)
    assert n % t == 0
    return t


def _dense_pre(x2, pos_rows, lw, lay):
    n = x2.shape[0]
    tm = _tile(n, 1024)
    rope = _rope_tables(pos_rows)
    tn = 1024
    p_main, h = _proj_main(x2, lw["g1"], lw["w_main"], rope, lay.groups(tn), tm, tn)
    kv = _proj_kv(h, lw["w_kv"], rope, _tile(n, 512))
    return p_main, kv


def _dense_post(x2, o_a, o_b, p_main, lw, lay, gf, final_norm):
    n = x2.shape[0]
    tm = _tile(n, 1024)
    x1 = _merge_out(x2, o_a, o_b, lw["w_pa"], lw["w_pb"], lw["w_o"], p_main,
                    lay.off["ga"], lay.off["gb"], tm, 256)
    return _ffn(x1, lw["g2"], lw["wg"], lw["wu"], lw["wd"], gf, tm, 512, final_norm)


def _prompt_layer(xp, lw, lay, gf, final_norm):
    b, t, d = xp.shape
    x2 = xp.reshape(b * t, d)
    p_main, (ka, va, ki, wt, ka_bf, vt_bf, kia, kib) = _dense_pre(x2, jnp.arange(t, dtype=jnp.int32), lw, lay)
    top_k = min(TOPK_MAX, t // 4)
    tq, ck = _tile(t, 256), _tile(t, 512)
    mask_t = _index_mask(p_main, lay.block("qi"), wt, kia.reshape(b, t, LANES), kib.reshape(b, t, LANES),
                         tq=tq, ck=ck, l_valid=t, pos_base=0, causal=True, top_k=top_k)
    nk = N_KV_A * HEAD_DIM_A
    o_a = _attention(p_main, ka_bf.reshape(b, t, nk), vt_bf, mask_t, tq=tq, tk=ck, causal=True, kvb=2)
    state0 = jnp.zeros((b, N_HEADS_R, KEY_DIM_R, VAL_DIM_R), F32)
    cols = (lay.block("qr"), lay.block("kr"), lay.block("vr"), lay.block("gr"))
    o_b, st = _retention(p_main, cols, state0, _tile(t, 256))
    y = _dense_post(x2, o_a, o_b, p_main, lw, lay, gf, final_norm)
    return (y.reshape(b, t, d), ka.reshape(b, t, N_KV_A, HEAD_DIM_A), va.reshape(b, t, N_KV_A, HEAD_DIM_A),
            ki.reshape(b, t, HEAD_DIM_IDX), st)


def _sample_layer(xs, cache_k, cache_v, cache_i, state, lw, lay, gf, final_norm):
    b, t, d = xs.shape
    past = cache_k.shape[1]
    nk = N_KV_A * HEAD_DIM_A
    x2 = xs.reshape(b * t, d)
    pos = jnp.tile(past + jnp.arange(t, dtype=jnp.int32), b)
    p_main, (ka, va, ki, wt, ka_bf, vt_bf, kia, kib) = _dense_pre(x2, pos, lw, lay)

    tq = LANES
    l_valid = past + t
    tk = 3 * LANES
    lp = -(-l_valid // tk) * tk
    pq = jnp.pad(p_main.reshape(b, t, -1), ((0, 0), (0, tq - t), (0, 0))).reshape(b * tq, -1)
    wtq = jnp.pad(wt.reshape(N_HEADS_IDX, b, t), ((0, 0), (0, 0), (0, tq - t))).reshape(N_HEADS_IDX, b * tq)
    kpad = ((0, 0), (0, lp - l_valid), (0, 0))
    ci = cache_i.astype(BF16)
    zi = jnp.zeros_like(ci)
    kia_all = jnp.pad(jnp.concatenate([jnp.concatenate([ci, zi], -1), kia.reshape(b, t, LANES)], 1), kpad)
    kib_all = jnp.pad(jnp.concatenate([jnp.concatenate([zi, ci], -1), kib.reshape(b, t, LANES)], 1), kpad)
    k_all = jnp.pad(jnp.concatenate([cache_k.reshape(b, past, nk).astype(BF16), ka_bf.reshape(b, t, nk)], 1), kpad)
    vt_new = vt_bf.reshape(N_KV_A, VT_ROWS, b, t)
    vt_cache = cache_v.reshape(b, past, N_KV_A, HEAD_DIM_A).transpose(2, 3, 0, 1).astype(BF16)
    ones_rows = jnp.zeros((N_KV_A, VT_ROWS - HEAD_DIM_A, b, past), BF16).at[:, 0].set(1.0)
    vt_all = jnp.concatenate([jnp.concatenate([vt_cache, ones_rows], 1), vt_new], 3)
    vt_all = jnp.pad(vt_all, ((0, 0), (0, 0), (0, 0), (0, lp - l_valid))).reshape(N_KV_A * VT_ROWS, b * lp)

    top_k = min(TOPK_MAX, l_valid // 4)
    mask_t = _index_mask(pq, lay.block("qi"), wtq, kia_all, kib_all, tq=tq, ck=tk, l_valid=l_valid,
                         pos_base=past, causal=False, top_k=top_k)
    o_a = _attention(pq, k_all, vt_all, mask_t, tq=tq, tk=tk, causal=False, kvb=1)
    o_a = o_a.reshape(b, tq, -1)[:, :t].reshape(b * t, -1)
    cols = (lay.block("qr"), lay.block("kr"), lay.block("vr"), lay.block("gr"))
    o_b, st = _retention(p_main, cols, state.astype(F32), t)
    y = _dense_post(x2, o_a, o_b, p_main, lw, lay, gf, final_norm)
    return (y.reshape(b, t, d), ka.reshape(b, t, N_KV_A, HEAD_DIM_A), va.reshape(b, t, N_KV_A, HEAD_DIM_A),
            ki.reshape(b, t, HEAD_DIM_IDX), st)


def kernel(x_prompt, x_sample, cache_k, cache_v, cache_idx_k, state_ret, norm1_g, w_in, w_pa, w_pb, w_o,
           norm2_g, w_ffn_gate, w_ffn_up, w_ffn_down, norm_f_g):
    assert HEAD_DIM_A == LANES and HEAD_DIM_IDX * 2 == LANES and N_HEADS_IDX % 2 == 0
    depth, d_model = norm1_g.shape
    lay = _MainLayout(d_model)
    xp, xs = x_prompt, x_sample
    outs = [[] for _ in range(8)]
    for l in range(depth):
        parts = _split_w_in(w_in[l], d_model)
        pad = jnp.zeros((d_model, LANES - HEAD_DIM_IDX - N_HEADS_IDX), F32)
        lw = {
            "g1": norm1_g[l], "g2": norm2_g[l],
            "w_main": jnp.concatenate([parts[name] for name, _, _, _ in lay.order], axis=1).astype(BF16),
            "w_kv": jnp.concatenate([parts["ka"], parts["va"], parts["ki"], parts["wi"], pad], axis=1).astype(BF16),
            "w_pa": w_pa[l].astype(BF16), "w_pb": w_pb[l].astype(BF16), "w_o": w_o[l].astype(BF16),
            "wg": w_ffn_gate[l].astype(BF16), "wu": w_ffn_up[l].astype(BF16), "wd": w_ffn_down[l].astype(BF16),
        }
        last = l == depth - 1
        xp, kp, vp, ip, sp = _prompt_layer(xp, lw, lay, norm_f_g, last)
        xs, ks, vs, isl, ss = _sample_layer(xs, cache_k[l], cache_v[l], cache_idx_k[l], state_ret[l],
                                            lw, lay, norm_f_g, last)
        for lst, val in zip(outs, (kp, vp, ip, sp, ks, vs, isl, ss)):
            lst.append(val)
    stacked = [jnp.stack(o) for o in outs]
    return (xp, xs, stacked[0], stacked[1], stacked[2], stacked[3].astype(x_prompt.dtype),
            stacked[4], stacked[5], stacked[6], stacked[7].astype(state_ret.dtype))
```

```python
import functools
import math

import jax
import jax.numpy as jnp
from jax import lax
from jax.experimental import pallas as pl
from jax.experimental.pallas import tpu as pltpu

F32 = jnp.float32
BF16 = jnp.bfloat16

CHUNK = 64
EPS = 1e-6
ROPE_THETA = 10000.0
N_HEADS_A = 16
N_KV_A = 4
HEAD_DIM_A = 128
N_HEADS_IDX = 16
HEAD_DIM_IDX = 64
TOPK_MAX = 256
N_HEADS_R = 8
KEY_DIM_R = 128
VAL_DIM_R = 256

LANES = 128
MXU_COLS = 256
V7X_VMEM_BYTES = 64 * 1024 * 1024
VMEM_LIMIT = V7X_VMEM_BYTES - 8 * 1024 * 1024
BF16_SUBLANES = 16
VT_ROWS = HEAD_DIM_A + BF16_SUBLANES
LOG2E = math.log2(math.e)
MAX_BISECT_ITERS = 64
NEG_BIG = -1e30


def _cparams(*sem):
    return pltpu.CompilerParams(dimension_semantics=sem, vmem_limit_bytes=VMEM_LIMIT)


def _dot(a, b):
    return jnp.dot(a, b, preferred_element_type=F32)


def _dot_nt(a, b):
    return lax.dot_general(a, b, (((1,), (1,)), ((), ())), preferred_element_type=F32)


def _rmsnorm_rows(x, g):
    return x * lax.rsqrt(jnp.mean(x * x, axis=-1, keepdims=True) + EPS) * g


def _rope_tables(pos):
    posf = pos.astype(F32)[:, None]

    def cs(half):
        inv_freq = ROPE_THETA ** (-jnp.arange(half, dtype=F32) / half)
        ang = posf * inv_freq[None, :]
        return jnp.cos(ang), jnp.sin(ang)

    c, s = cs(HEAD_DIM_A // 2)
    c128 = jnp.concatenate([c, c], axis=1)
    s128 = jnp.concatenate([-s, s], axis=1)
    c, s = cs(HEAD_DIM_IDX // 2)
    z = jnp.zeros_like(s)
    c64 = jnp.tile(c, (1, 4))
    a64 = jnp.tile(jnp.concatenate([-s, z], axis=1), (1, 2))
    b64 = jnp.tile(jnp.concatenate([z, s], axis=1), (1, 2))
    return jnp.stack([c128, s128, c64, a64, b64])


def _rope128(x, rope_ref):
    return x * rope_ref[0] + pltpu.roll(x, 64, 1) * rope_ref[1]


def _rope64(x, rope_ref):
    return (x * rope_ref[2] + pltpu.roll(x, 96, 1) * rope_ref[3]
            + pltpu.roll(x, 32, 1) * rope_ref[4])


def _rmsnorm_kernel(x_ref, g_ref, o_ref):
    o_ref[...] = _rmsnorm_rows(x_ref[...], g_ref[...]).astype(o_ref.dtype)


def _rmsnorm(x, g, tm):
    n, d = x.shape
    return pl.pallas_call(
        _rmsnorm_kernel,
        grid=(n // tm,),
        in_specs=[pl.BlockSpec((tm, d), lambda i: (i, 0)),
                  pl.BlockSpec((1, d), lambda i: (0, 0))],
        out_specs=pl.BlockSpec((tm, d), lambda i: (i, 0)),
        out_shape=jax.ShapeDtypeStruct((n, d), BF16),
        compiler_params=_cparams("parallel"),
        name="rmsnorm",
    )(x, g.reshape(1, d))


def _proj_main_kernel(h_ref, w_ref, rope_ref, o_ref, *, groups, tn):
    j = pl.program_id(1)
    for lo, hi, mode, scale in groups:
        @pl.when(jnp.logical_and(j >= lo, j < hi))
        def _(mode=mode, scale=scale):
            for b in range(tn // MXU_COLS):
                acc = _dot(h_ref[...], w_ref[:, b * MXU_COLS:(b + 1) * MXU_COLS])
                for c in range(MXU_COLS // LANES):
                    x = acc[:, c * LANES:(c + 1) * LANES]
                    if mode == "rope128":
                        y = _rope128(x, rope_ref)
                    elif mode == "rope64":
                        y = _rope64(x, rope_ref)
                    elif mode == "sigmoid":
                        y = 1.0 / (1.0 + jnp.exp(-x))
                    elif mode == "silu":
                        y = x / (1.0 + jnp.exp(-x))
                    else:
                        y = x
                    if scale != 1.0:
                        y = y * scale
                    col = b * MXU_COLS + c * LANES
                    o_ref[:, col:col + LANES] = y.astype(o_ref.dtype)


def _proj_main(h, w_main, rope, groups, tm, tn):
    n, d = h.shape
    p = w_main.shape[1]
    n_pos_tiles = rope.shape[1] // tm
    return pl.pallas_call(
        functools.partial(_proj_main_kernel, groups=groups, tn=tn),
        grid=(n // tm, p // tn),
        in_specs=[pl.BlockSpec((tm, d), lambda i, j: (i, 0)),
                  pl.BlockSpec((d, tn), lambda i, j: (0, j)),
                  pl.BlockSpec((5, tm, LANES), lambda i, j: (0, i % n_pos_tiles, 0))],
        out_specs=pl.BlockSpec((tm, tn), lambda i, j: (i, j)),
        out_shape=jax.ShapeDtypeStruct((n, p), BF16),
        compiler_params=_cparams("parallel", "arbitrary"),
        name="proj_main",
    )(h, w_main, rope)


def _proj_kv_kernel(h_ref, w_ref, rope_ref, ka_ref, va_ref, ki_ref, wt_ref,
                    kab_ref, vtb_ref, kia_ref, kib_ref, acc_ref, *, nk, idx_scale):
    acc_ref[...] = _dot(h_ref[...], w_ref[...])
    for c in range(nk // LANES):
        sl = slice(c * LANES, (c + 1) * LANES)
        y = _rope128(acc_ref[:, sl], rope_ref)
        ka_ref[:, sl] = y
        kab_ref[:, sl] = y.astype(BF16)
    va_ref[...] = acc_ref[:, nk:2 * nk]
    tm = acc_ref.shape[0]
    ones_rows = jnp.where(lax.broadcasted_iota(jnp.int32, (VT_ROWS - HEAD_DIM_A, tm), 0) == 0, 1.0, 0.0)
    for g in range(N_KV_A):
        vg = acc_ref[:, nk + g * HEAD_DIM_A:nk + (g + 1) * HEAD_DIM_A]
        vtb_ref[g * VT_ROWS:g * VT_ROWS + HEAD_DIM_A, :] = vg.T.astype(BF16)
        vtb_ref[g * VT_ROWS + HEAD_DIM_A:(g + 1) * VT_ROWS, :] = ones_rows.astype(BF16)
    z = acc_ref[:, 2 * nk:2 * nk + LANES]
    y = _rope64(z, rope_ref)
    ki_ref[...] = y[:, :HEAD_DIM_IDX]
    lane = lax.broadcasted_iota(jnp.int32, y.shape, 1)
    ya = jnp.where(lane < HEAD_DIM_IDX, y, 0.0)
    kia_ref[...] = ya.astype(BF16)
    kib_ref[...] = pltpu.roll(ya, HEAD_DIM_IDX, 1).astype(BF16)
    wt_ref[...] = (z * idx_scale).T[HEAD_DIM_IDX:HEAD_DIM_IDX + N_HEADS_IDX, :]


def _proj_kv(h, w_kv, rope, tm):
    n, d = h.shape
    nk = N_KV_A * HEAD_DIM_A
    pw = w_kv.shape[1]
    n_pos_tiles = rope.shape[1] // tm
    idx_scale = (HEAD_DIM_IDX ** -0.5) * (N_HEADS_IDX ** -0.5)
    row = lambda i: (i, 0)
    return pl.pallas_call(
        functools.partial(_proj_kv_kernel, nk=nk, idx_scale=idx_scale),
        grid=(n // tm,),
        in_specs=[pl.BlockSpec((tm, d), row),
                  pl.BlockSpec((d, pw), lambda i: (0, 0)),
                  pl.BlockSpec((5, tm, LANES), lambda i: (0, i % n_pos_tiles, 0))],
        out_specs=[pl.BlockSpec((tm, nk), row), pl.BlockSpec((tm, nk), row),
                   pl.BlockSpec((tm, HEAD_DIM_IDX), row),
                   pl.BlockSpec((N_HEADS_IDX, tm), lambda i: (0, i)),
                   pl.BlockSpec((tm, nk), row),
                   pl.BlockSpec((N_KV_A * VT_ROWS, tm), lambda i: (0, i)),
                   pl.BlockSpec((tm, LANES), row), pl.BlockSpec((tm, LANES), row)],
        out_shape=[jax.ShapeDtypeStruct((n, nk), F32), jax.ShapeDtypeStruct((n, nk), F32),
                   jax.ShapeDtypeStruct((n, HEAD_DIM_IDX), F32),
                   jax.ShapeDtypeStruct((N_HEADS_IDX, n), F32),
                   jax.ShapeDtypeStruct((n, nk), BF16),
                   jax.ShapeDtypeStruct((N_KV_A * VT_ROWS, n), BF16),
                   jax.ShapeDtypeStruct((n, LANES), BF16), jax.ShapeDtypeStruct((n, LANES), BF16)],
        scratch_shapes=[pltpu.VMEM((tm, pw), F32)],
        compiler_params=_cparams("parallel"),
        name="proj_kv",
    )(h, w_kv, rope)


def _index_kernel(qi_ref, wt_ref, kia_ref, kib_ref, m_ref, s_ref, *,
                  tq, ck, n_ck_total, l_valid, pos_base, causal, top_k):
    i = pl.program_id(1)
    if causal:
        n_c = (i * tq + tq + ck - 1) // ck
    else:
        n_c = n_ck_total
    qpos = pos_base + i * tq + lax.broadcasted_iota(jnp.int32, (1, tq), 1)
    lim = jnp.minimum(qpos - lax.rem(qpos, CHUNK) + CHUNK, l_valid)
    limf = lim.astype(F32)
    kprime = jnp.minimum(float(top_k), limf)
    inf = jnp.float32(jnp.inf)

    def key_index(off):
        return off + lax.broadcasted_iota(jnp.int32, (ck, tq), 0)

    def score_chunk(c, carry):
        rmax, rmin = carry
        off = pl.multiple_of(c * ck, ck)
        ka = kia_ref[pl.ds(off, ck), :]
        kb = kib_ref[pl.ds(off, ck), :]
        acc = jnp.zeros((ck, tq), F32)
        for p in range(N_HEADS_IDX // 2):
            qp = qi_ref[:, p * LANES:(p + 1) * LANES]
            acc += jnp.maximum(_dot_nt(ka, qp), 0.0) * wt_ref[2 * p:2 * p + 1, :]
            acc += jnp.maximum(_dot_nt(kb, qp), 0.0) * wt_ref[2 * p + 1:2 * p + 2, :]
        adm = key_index(off) < lim
        s_ref[pl.ds(off, ck), :] = jnp.where(adm, acc, -inf)
        rmax = jnp.maximum(rmax, jnp.max(jnp.where(adm, acc, -inf), axis=0, keepdims=True))
        rmin = jnp.minimum(rmin, jnp.min(jnp.where(adm, acc, inf), axis=0, keepdims=True))
        return rmax, rmin

    rmax, rmin = lax.fori_loop(0, n_c, score_chunk,
                               (jnp.full((1, tq), -inf, F32), jnp.full((1, tq), inf, F32)))

    fold_rows = min(ck, 64)

    def fold(ind):
        return ind.reshape(ck // fold_rows, fold_rows, tq).sum(axis=0)

    def count_ge(th):
        def body(c, acc):
            off = pl.multiple_of(c * ck, ck)
            return acc + fold(jnp.where(s_ref[pl.ds(off, ck), :] >= th, 1.0, 0.0))
        acc = lax.fori_loop(0, n_c, body, jnp.zeros((fold_rows, tq), F32))
        return acc.sum(axis=0, keepdims=True)

    def cond(st):
        it, lo, hi, cnt, stuck = st
        active = jnp.where(jnp.logical_and(cnt > kprime, stuck < 0.5), 1.0, 0.0)
        return jnp.logical_and(it < MAX_BISECT_ITERS, jnp.max(active) > 0.5)

    def body(st):
        it, lo, hi, cnt, stuck = st
        open_top = hi == inf
        mid = lo + 0.5 * (jnp.where(open_top, rmax, hi) - lo)
        mid = jnp.where(jnp.logical_and(open_top, mid <= lo), rmax, mid)
        c = count_ge(mid)
        ge = c >= kprime
        no_progress = jnp.logical_or(mid <= lo, mid >= hi)
        return (it + 1, jnp.where(ge, mid, lo), jnp.where(ge, hi, mid), jnp.where(ge, c, cnt),
                jnp.where(no_progress, 1.0, stuck))

    _, lo, hi, cnt, _ = lax.while_loop(
        cond, body, (jnp.int32(0), rmin, jnp.full((1, tq), inf, F32), limf, jnp.zeros((1, tq), F32)))

    unresolved = jnp.max(jnp.where(cnt > kprime, 1.0, 0.0)) > 0.5

    @pl.when(jnp.logical_not(unresolved))
    def _():
        def body(c, _):
            off = pl.multiple_of(c * ck, ck)
            m_ref[pl.ds(off, ck), :] = jnp.where(s_ref[pl.ds(off, ck), :] >= lo, 0.0, -inf).astype(m_ref.dtype)
            return 0
        lax.fori_loop(0, n_c, body, 0)

    @pl.when(unresolved)
    def _():
        need = kprime - count_ge(hi)

        def in_tie(blk):
            return jnp.logical_and(blk >= lo, blk < hi)

        def count_tie_below(jcut):
            def body(c, acc):
                off = pl.multiple_of(c * ck, ck)
                blk = s_ref[pl.ds(off, ck), :]
                e = jnp.logical_and(in_tie(blk), key_index(off).astype(F32) < jcut)
                return acc + fold(jnp.where(e, 1.0, 0.0))
            acc = lax.fori_loop(0, n_c, body, jnp.zeros((fold_rows, tq), F32))
            return acc.sum(axis=0, keepdims=True)

        def jbody(_, st):
            jlo, jhi = st
            mid = jnp.floor(0.5 * (jlo + jhi))
            ok = count_tie_below(mid) >= need
            return jnp.where(ok, jlo, mid + 1.0), jnp.where(ok, mid, jhi)

        n_total = n_ck_total * ck
        _, jcut = lax.fori_loop(0, int(math.ceil(math.log2(n_total + 1))), jbody,
                                (jnp.zeros((1, tq), F32), jnp.full((1, tq), float(n_total), F32)))

        def body(c, _):
            off = pl.multiple_of(c * ck, ck)
            blk = s_ref[pl.ds(off, ck), :]
            e = jnp.logical_and(in_tie(blk), key_index(off).astype(F32) < jcut)
            keep = jnp.logical_or(blk >= hi, e)
            m_ref[pl.ds(off, ck), :] = jnp.where(keep, 0.0, -inf).astype(m_ref.dtype)
            return 0
        lax.fori_loop(0, n_c, body, 0)

    def zero_body(c, _):
        off = pl.multiple_of(c * ck, ck)
        m_ref[pl.ds(off, ck), :] = jnp.full((ck, tq), -inf, m_ref.dtype)
        return 0
    lax.fori_loop(n_c, n_ck_total, zero_body, 0)


def _index_mask(p_main, qi_col_block, wt, kia, kib, *, tq, ck, l_valid, pos_base, causal, top_k):
    b, lp, _ = kia.shape
    nq = p_main.shape[0] // (b * tq)
    qw = N_HEADS_IDX * HEAD_DIM_IDX
    return pl.pallas_call(
        functools.partial(_index_kernel, tq=tq, ck=ck, n_ck_total=lp // ck, l_valid=l_valid,
                          pos_base=pos_base, causal=causal, top_k=top_k),
        grid=(b, nq),
        in_specs=[pl.BlockSpec((tq, qw), lambda bi, i: (bi * nq + i, qi_col_block)),
                  pl.BlockSpec((N_HEADS_IDX, tq), lambda bi, i: (0, bi * nq + i)),
                  pl.BlockSpec((None, lp, LANES), lambda bi, i: (bi, 0, 0)),
                  pl.BlockSpec((None, lp, LANES), lambda bi, i: (bi, 0, 0))],
        out_specs=pl.BlockSpec((None, lp, tq), lambda bi, i: (bi, 0, i)),
        out_shape=jax.ShapeDtypeStruct((b, lp, nq * tq), BF16),
        scratch_shapes=[pltpu.VMEM((lp, tq), F32)],
        compiler_params=_cparams("parallel", "parallel"),
        name="index_mask",
    )(p_main, wt, kia, kib)


def _attn_kernel(q_ref, k_ref, vt_ref, m_ref, o_ref, acc_ref, sa_ref, sb_ref, *,
                 tq, tk, n_kt_total, causal, group, kvb):
    i = pl.program_id(1)
    heads = kvb * group
    if causal:
        n_kt = (i * tq + tq + tk - 1) // tk
    else:
        n_kt = n_kt_total
    acc_ref[...] = jnp.zeros_like(acc_ref)

    def scores(kt, s_ref):
        off = pl.multiple_of(kt * tk, tk)
        for h in range(heads):
            kv = h // group
            k = k_ref[pl.ds(off, tk), kv * HEAD_DIM_A:(kv + 1) * HEAD_DIM_A]
            s_ref[h] = _dot_nt(k, q_ref[:, h * HEAD_DIM_A:(h + 1) * HEAD_DIM_A])

    def softmax_pv(kt, s_ref, ms):
        off = pl.multiple_of(kt * tk, tk)
        bias = m_ref[pl.ds(off, tk), :].astype(F32)
        out = []
        for h in range(heads):
            kv = h // group
            vt = vt_ref[kv * VT_ROWS:(kv + 1) * VT_ROWS, pl.ds(off, tk)]
            s = s_ref[h] + bias
            m_new = jnp.maximum(ms[h], jnp.max(s, axis=0, keepdims=True))
            alpha = jnp.exp2(ms[h] - m_new)
            p = jnp.exp2(s - m_new).astype(BF16)
            acc_ref[h] = acc_ref[h] * alpha + _dot(vt, p)
            out.append(m_new)
        return tuple(out)

    def body(j, ms):
        scores(2 * j + 1, sb_ref)
        ms = softmax_pv(2 * j, sa_ref, ms)
        scores(2 * j + 2, sa_ref)
        return softmax_pv(2 * j + 1, sb_ref, ms)

    scores(0, sa_ref)
    n_pairs = (n_kt - 1) // 2
    ms = lax.fori_loop(0, n_pairs, body, tuple(jnp.full((1, tq), NEG_BIG, F32) for _ in range(heads)))
    even = n_kt - 1 > 2 * n_pairs

    @pl.when(even)
    def _():
        scores(2 * n_pairs + 1, sb_ref)

    ms = softmax_pv(2 * n_pairs, sa_ref, ms)

    @pl.when(even)
    def _():
        softmax_pv(2 * n_pairs + 1, sb_ref, ms)

    for h in range(heads):
        a = acc_ref[h]
        o = a[:HEAD_DIM_A] / a[HEAD_DIM_A:HEAD_DIM_A + 1]
        o_ref[:, h * HEAD_DIM_A:(h + 1) * HEAD_DIM_A] = o.T.astype(o_ref.dtype)


def _attention(p_main, k_bf, vt_bf, mask_t, *, tq, tk, causal, kvb):
    b, lp, _ = k_bf.shape
    nq = mask_t.shape[2] // tq
    group = N_HEADS_A // N_KV_A
    heads = kvb * group
    gw = heads * HEAD_DIM_A
    return pl.pallas_call(
        functools.partial(_attn_kernel, tq=tq, tk=tk, n_kt_total=lp // tk, causal=causal, group=group,
                          kvb=kvb),
        grid=(b, nq, N_KV_A // kvb),
        in_specs=[pl.BlockSpec((tq, gw), lambda bi, i, g: (bi * nq + i, g)),
                  pl.BlockSpec((None, lp, kvb * HEAD_DIM_A), lambda bi, i, g: (bi, 0, g)),
                  pl.BlockSpec((kvb * VT_ROWS, lp), lambda bi, i, g: (g, bi)),
                  pl.BlockSpec((None, lp, tq), lambda bi, i, g: (bi, 0, i))],
        out_specs=pl.BlockSpec((tq, gw), lambda bi, i, g: (bi * nq + i, g)),
        out_shape=jax.ShapeDtypeStruct((b * nq * tq, N_HEADS_A * HEAD_DIM_A), BF16),
        scratch_shapes=[pltpu.VMEM((heads, VT_ROWS, tq), F32),
                        pltpu.VMEM((heads, tk, tq), F32), pltpu.VMEM((heads, tk, tq), F32)],
        compiler_params=_cparams("parallel", "parallel", "arbitrary"),
        name="attention",
    )(p_main, k_bf, vt_bf, mask_t)


def _retention_tables(c):
    log_gamma = jnp.log1p(-(2.0 ** (-5.0 - jnp.arange(N_HEADS_R, dtype=F32))))
    idx = jnp.arange(c, dtype=F32)
    diff = idx[:, None] - idx[None, :]
    decay = jnp.where(diff[None] >= 0,
                      jnp.exp(jnp.maximum(diff, 0.0)[None] * log_gamma[:, None, None]), 0.0)
    cross = jnp.exp((idx + 1.0)[None, :] * log_gamma[:, None])
    kdec = jnp.exp((c - 1.0 - idx)[None, :] * log_gamma[:, None])
    full = jnp.exp(c * log_gamma)
    bc = lambda a, w: jnp.broadcast_to(a[..., None], a.shape + (w,))
    return decay, bc(cross, KEY_DIM_R), bc(kdec, KEY_DIM_R), bc(full[:, None], VAL_DIM_R)


def _retention_kernel(q_ref, k_ref, v_ref, g_ref, dec_ref, qsc_ref, ksc_ref, gc_ref, s0_ref,
                      o_ref, st_ref):
    @pl.when(pl.program_id(1) == 0)
    def _():
        st_ref[...] = s0_ref[...]

    for h in range(N_HEADS_R):
        ks = slice(h * KEY_DIM_R, (h + 1) * KEY_DIM_R)
        vs = slice(h * VAL_DIM_R, (h + 1) * VAL_DIM_R)
        q = q_ref[:, ks]
        k = k_ref[:, ks]
        v = v_ref[:, vs]
        st = st_ref[h]
        inner = _dot_nt(q, k) * dec_ref[h]
        qd = (q.astype(F32) * qsc_ref[h]).astype(BF16)
        o = _dot(inner.astype(BF16), v) + _dot(qd, st.astype(BF16))
        kdt = (k.astype(F32) * ksc_ref[h]).T.astype(BF16)
        st_ref[h] = st * gc_ref[h] + _dot(kdt, v)
        o = o * lax.rsqrt(jnp.mean(o * o, axis=-1, keepdims=True) + EPS)
        o_ref[:, vs] = (o * g_ref[:, vs].astype(F32)).astype(o_ref.dtype)


def _retention(p_main, cols, state0, c):
    b = state0.shape[0]
    nc = p_main.shape[0] // (b * c)
    kw = N_HEADS_R * KEY_DIM_R
    vw = N_HEADS_R * VAL_DIM_R
    dec, qsc, ksc, gcs = _retention_tables(c)
    qb, kb, vb, gb = cols
    const3 = lambda bi, ci: (0, 0, 0)
    st_spec = pl.BlockSpec((None, N_HEADS_R, KEY_DIM_R, VAL_DIM_R), lambda bi, ci: (bi, 0, 0, 0))
    return pl.pallas_call(
        _retention_kernel,
        grid=(b, nc),
        in_specs=[pl.BlockSpec((c, kw), lambda bi, ci: (bi * nc + ci, qb)),
                  pl.BlockSpec((c, kw), lambda bi, ci: (bi * nc + ci, kb)),
                  pl.BlockSpec((c, vw), lambda bi, ci: (bi * nc + ci, vb)),
                  pl.BlockSpec((c, vw), lambda bi, ci: (bi * nc + ci, gb)),
                  pl.BlockSpec(dec.shape, const3), pl.BlockSpec(qsc.shape, const3),
                  pl.BlockSpec(ksc.shape, const3), pl.BlockSpec(gcs.shape, const3),
                  st_spec],
        out_specs=[pl.BlockSpec((c, vw), lambda bi, ci: (bi * nc + ci, 0)), st_spec],
        out_shape=[jax.ShapeDtypeStruct((b * nc * c, vw), BF16),
                   jax.ShapeDtypeStruct(state0.shape, F32)],
        compiler_params=_cparams("parallel", "arbitrary"),
        name="retention",
    )(p_main, p_main, p_main, p_main, dec, qsc, ksc, gcs, state0)


def _merge_kernel(oa_ref, ob_ref, wa_ref, wb_ref, sa_ref, sb_ref, o_ref):
    for b in range(o_ref.shape[1] // MXU_COLS):
        sl = slice(b * MXU_COLS, (b + 1) * MXU_COLS)
        m = (_dot(oa_ref[...], wa_ref[:, sl]) * sa_ref[:, sl].astype(F32)
             + _dot(ob_ref[...], wb_ref[:, sl]) * sb_ref[:, sl].astype(F32))
        o_ref[:, sl] = m.astype(o_ref.dtype)


def _merge(o_a, o_b, w_pa, w_pb, p_main, ga_off, gb_off, tm, tn):
    n, wa = o_a.shape
    wb = o_b.shape[1]
    d = w_pa.shape[1]
    return pl.pallas_call(
        _merge_kernel,
        grid=(n // tm, d // tn),
        in_specs=[pl.BlockSpec((tm, wa), lambda i, j: (i, 0)),
                  pl.BlockSpec((tm, wb), lambda i, j: (i, 0)),
                  pl.BlockSpec((wa, tn), lambda i, j: (0, j)),
                  pl.BlockSpec((wb, tn), lambda i, j: (0, j)),
                  pl.BlockSpec((tm, tn), lambda i, j: (i, ga_off // tn + j)),
                  pl.BlockSpec((tm, tn), lambda i, j: (i, gb_off // tn + j))],
        out_specs=pl.BlockSpec((tm, tn), lambda i, j: (i, j)),
        out_shape=jax.ShapeDtypeStruct((n, d), BF16),
        compiler_params=_cparams("parallel", "arbitrary"),
        name="merge",
    )(o_a, o_b, w_pa, w_pb, p_main, p_main)


def _out_proj_kernel(x_ref, m_ref, w_ref, x1_ref):
    x1_ref[...] = x_ref[...] + _dot(m_ref[...], w_ref[...])


def _out_proj(x, merged, w_o, tm):
    n, d = x.shape
    row = lambda i: (i, 0)
    return pl.pallas_call(
        _out_proj_kernel,
        grid=(n // tm,),
        in_specs=[pl.BlockSpec((tm, d), row), pl.BlockSpec((tm, d), row),
                  pl.BlockSpec((d, d), lambda i: (0, 0))],
        out_specs=pl.BlockSpec((tm, d), row),
        out_shape=jax.ShapeDtypeStruct((n, d), F32),
        compiler_params=_cparams("parallel"),
        name="out_proj",
    )(x, merged, w_o)


def _ffn_kernel(x1_ref, g2_ref, wg_ref, wu_ref, wd_ref, g_ref, y_ref, h_ref, *, final_norm):
    c = pl.program_id(1)

    @pl.when(c == 0)
    def _():
        x1 = x1_ref[...]
        y_ref[...] = x1
        h_ref[...] = _rmsnorm_rows(x1, g2_ref[...]).astype(h_ref.dtype)

    acts = []
    for b in range(wg_ref.shape[1] // MXU_COLS):
        sl = slice(b * MXU_COLS, (b + 1) * MXU_COLS)
        a = _dot(h_ref[...], wg_ref[:, sl])
        u = _dot(h_ref[...], wu_ref[:, sl])
        acts.append((a / (1.0 + jnp.exp(-a)) * u).astype(BF16))
    upd = _dot(acts[0], wd_ref[0:MXU_COLS, :])
    for b in range(1, len(acts)):
        upd += _dot(acts[b], wd_ref[b * MXU_COLS:(b + 1) * MXU_COLS, :])
    y_ref[...] += upd

    if final_norm:
        @pl.when(c == pl.num_programs(1) - 1)
        def _():
            y_ref[...] = _rmsnorm_rows(y_ref[...], g_ref[...])


def _ffn(x1, g2, wg, wu, wd, gf, tm, tc, final_norm):
    n, d = x1.shape
    f = wg.shape[1]
    row = lambda i, c: (i, 0)
    vec = pl.BlockSpec((1, d), lambda i, c: (0, 0))
    return pl.pallas_call(
        functools.partial(_ffn_kernel, final_norm=final_norm),
        grid=(n // tm, f // tc),
        in_specs=[pl.BlockSpec((tm, d), row, pipeline_mode=pl.Buffered(1)), vec,
                  pl.BlockSpec((d, tc), lambda i, c: (0, c)),
                  pl.BlockSpec((d, tc), lambda i, c: (0, c)),
                  pl.BlockSpec((tc, d), lambda i, c: (c, 0)), vec],
        out_specs=pl.BlockSpec((tm, d), row),
        out_shape=jax.ShapeDtypeStruct((n, d), F32),
        scratch_shapes=[pltpu.VMEM((tm, d), BF16)],
        compiler_params=_cparams("parallel", "arbitrary"),
        name="ffn",
    )(x1, g2.reshape(1, d), wg, wu, wd, gf.reshape(1, d))


class _MainLayout:
    def __init__(self, d_model):
        wa = N_HEADS_A * HEAD_DIM_A
        wb = N_HEADS_R * VAL_DIM_R
        kr = N_HEADS_R * KEY_DIM_R
        qi = N_HEADS_IDX * HEAD_DIM_IDX
        order = [("qa", wa, "rope128", HEAD_DIM_A ** -0.5 * LOG2E), ("vr", wb, "plain", 1.0),
                 ("gr", wb, "silu", 1.0), ("ga", d_model, "sigmoid", 1.0), ("gb", d_model, "sigmoid", 1.0),
                 ("qi", qi, "rope64", 1.0), ("qr", kr, "rope128", 1.0),
                 ("kr", kr, "rope128", KEY_DIM_R ** -0.5)]
        self.off, self.width = {}, {}
        self.order = order
        o = 0
        for name, w, _, _ in order:
            self.off[name], self.width[name] = o, w
            o += w
        self.total = o

    def groups(self, tn):
        out = []
        for name, w, mode, scale in self.order:
            assert self.off[name] % tn == 0 and w % tn == 0
            out.append((self.off[name] // tn, (self.off[name] + w) // tn, mode, scale))
        return tuple(out)

    def block(self, name, width=None):
        width = width or self.width[name]
        assert self.off[name] % width == 0
        return self.off[name] // width


def _split_w_in(w, d_model):
    wa = N_HEADS_A * HEAD_DIM_A
    nk = N_KV_A * HEAD_DIM_A
    wb = N_HEADS_R * VAL_DIM_R
    kr = N_HEADS_R * KEY_DIM_R
    names = ("qa", "ka", "va", "qi", "ki", "wi", "qr", "kr", "vr", "gr", "ga", "gb")
    widths = (wa, nk, nk, N_HEADS_IDX * HEAD_DIM_IDX, HEAD_DIM_IDX, N_HEADS_IDX, kr, kr, wb, wb,
              d_model, d_model)
    parts, o = {}, 0
    for name, wd in zip(names, widths):
        parts[name] = w[:, o:o + wd]
        o += wd
    assert o == w.shape[1]
    return parts


def _tile(n, pref):
    t = min(n, pref)
    assert n % t == 0
    return t


def _dense_pre(x2, pos_rows, lw, lay):
    n = x2.shape[0]
    tm = _tile(n, 1024)
    rope = _rope_tables(pos_rows)
    h = _rmsnorm(x2, lw["g1"], tm)
    tn = 1024
    p_main = _proj_main(h, lw["w_main"], rope, lay.groups(tn), tm, tn)
    kv = _proj_kv(h, lw["w_kv"], rope, _tile(n, 512))
    return p_main, kv


def _dense_post(x2, o_a, o_b, p_main, lw, lay, gf, final_norm):
    n = x2.shape[0]
    tm = _tile(n, 1024)
    merged = _merge(o_a, o_b, lw["w_pa"], lw["w_pb"], p_main, lay.off["ga"], lay.off["gb"], tm, 512)
    x1 = _out_proj(x2, merged, lw["w_o"], _tile(n, 512))
    return _ffn(x1, lw["g2"], lw["wg"], lw["wu"], lw["wd"], gf, tm, 512, final_norm)


def _prompt_layer(xp, lw, lay, gf, final_norm):
    b, t, d = xp.shape
    x2 = xp.reshape(b * t, d)
    p_main, (ka, va, ki, wt, ka_bf, vt_bf, kia, kib) = _dense_pre(x2, jnp.arange(t, dtype=jnp.int32), lw, lay)
    top_k = min(TOPK_MAX, t // 4)
    tq, ck = _tile(t, 256), _tile(t, 512)
    mask_t = _index_mask(p_main, lay.block("qi"), wt, kia.reshape(b, t, LANES), kib.reshape(b, t, LANES),
                         tq=tq, ck=ck, l_valid=t, pos_base=0, causal=True, top_k=top_k)
    nk = N_KV_A * HEAD_DIM_A
    o_a = _attention(p_main, ka_bf.reshape(b, t, nk), vt_bf, mask_t, tq=tq, tk=ck, causal=True, kvb=2)
    state0 = jnp.zeros((b, N_HEADS_R, KEY_DIM_R, VAL_DIM_R), F32)
    cols = (lay.block("qr"), lay.block("kr"), lay.block("vr"), lay.block("gr"))
    o_b, st = _retention(p_main, cols, state0, _tile(t, 256))
    y = _dense_post(x2, o_a, o_b, p_main, lw, lay, gf, final_norm)
    return (y.reshape(b, t, d), ka.reshape(b, t, N_KV_A, HEAD_DIM_A), va.reshape(b, t, N_KV_A, HEAD_DIM_A),
            ki.reshape(b, t, HEAD_DIM_IDX), st)


def _sample_layer(xs, cache_k, cache_v, cache_i, state, lw, lay, gf, final_norm):
    b, t, d = xs.shape
    past = cache_k.shape[1]
    nk = N_KV_A * HEAD_DIM_A
    x2 = xs.reshape(b * t, d)
    pos = jnp.tile(past + jnp.arange(t, dtype=jnp.int32), b)
    p_main, (ka, va, ki, wt, ka_bf, vt_bf, kia, kib) = _dense_pre(x2, pos, lw, lay)

    tq = LANES
    l_valid = past + t
    tk = 3 * LANES
    lp = -(-l_valid // tk) * tk
    pq = jnp.pad(p_main.reshape(b, t, -1), ((0, 0), (0, tq - t), (0, 0))).reshape(b * tq, -1)
    wtq = jnp.pad(wt.reshape(N_HEADS_IDX, b, t), ((0, 0), (0, 0), (0, tq - t))).reshape(N_HEADS_IDX, b * tq)
    kpad = ((0, 0), (0, lp - l_valid), (0, 0))
    ci = cache_i.astype(BF16)
    zi = jnp.zeros_like(ci)
    kia_all = jnp.pad(jnp.concatenate([jnp.concatenate([ci, zi], -1), kia.reshape(b, t, LANES)], 1), kpad)
    kib_all = jnp.pad(jnp.concatenate([jnp.concatenate([zi, ci], -1), kib.reshape(b, t, LANES)], 1), kpad)
    k_all = jnp.pad(jnp.concatenate([cache_k.reshape(b, past, nk).astype(BF16), ka_bf.reshape(b, t, nk)], 1), kpad)
    vt_new = vt_bf.reshape(N_KV_A, VT_ROWS, b, t)
    vt_cache = cache_v.reshape(b, past, N_KV_A, HEAD_DIM_A).transpose(2, 3, 0, 1).astype(BF16)
    ones_rows = jnp.zeros((N_KV_A, VT_ROWS - HEAD_DIM_A, b, past), BF16).at[:, 0].set(1.0)
    vt_all = jnp.concatenate([jnp.concatenate([vt_cache, ones_rows], 1), vt_new], 3)
    vt_all = jnp.pad(vt_all, ((0, 0), (0, 0), (0, 0), (0, lp - l_valid))).reshape(N_KV_A * VT_ROWS, b * lp)

    top_k = min(TOPK_MAX, l_valid // 4)
    mask_t = _index_mask(pq, lay.block("qi"), wtq, kia_all, kib_all, tq=tq, ck=tk, l_valid=l_valid,
                         pos_base=past, causal=False, top_k=top_k)
    o_a = _attention(pq, k_all, vt_all, mask_t, tq=tq, tk=tk, causal=False, kvb=1)
    o_a = o_a.reshape(b, tq, -1)[:, :t].reshape(b * t, -1)
    cols = (lay.block("qr"), lay.block("kr"), lay.block("vr"), lay.block("gr"))
    o_b, st = _retention(p_main, cols, state.astype(F32), t)
    y = _dense_post(x2, o_a, o_b, p_main, lw, lay, gf, final_norm)
    return (y.reshape(b, t, d), ka.reshape(b, t, N_KV_A, HEAD_DIM_A), va.reshape(b, t, N_KV_A, HEAD_DIM_A),
            ki.reshape(b, t, HEAD_DIM_IDX), st)


def kernel(x_prompt, x_sample, cache_k, cache_v, cache_idx_k, state_ret, norm1_g, w_in, w_pa, w_pb, w_o,
           norm2_g, w_ffn_gate, w_ffn_up, w_ffn_down, norm_f_g):
    assert HEAD_DIM_A == LANES and HEAD_DIM_IDX * 2 == LANES and N_HEADS_IDX % 2 == 0
    depth, d_model = norm1_g.shape
    lay = _MainLayout(d_model)
    xp, xs = x_prompt, x_sample
    outs = [[] for _ in range(8)]
    for l in range(depth):
        parts = _split_w_in(w_in[l], d_model)
        pad = jnp.zeros((d_model, LANES - HEAD_DIM_IDX - N_HEADS_IDX), F32)
        lw = {
            "g1": norm1_g[l], "g2": norm2_g[l],
            "w_main": jnp.concatenate([parts[name] for name, _, _, _ in lay.order], axis=1).astype(BF16),
            "w_kv": jnp.concatenate([parts["ka"], parts["va"], parts["ki"], parts["wi"], pad], axis=1).astype(BF16),
            "w_pa": w_pa[l].astype(BF16), "w_pb": w_pb[l].astype(BF16), "w_o": w_o[l].astype(BF16),
            "wg": w_ffn_gate[l].astype(BF16), "wu": w_ffn_up[l].astype(BF16), "wd": w_ffn_down[l].astype(BF16),
        }
        last = l == depth - 1
        xp, kp, vp, ip, sp = _prompt_layer(xp, lw, lay, norm_f_g, last)
        xs, ks, vs, isl, ss = _sample_layer(xs, cache_k[l], cache_v[l], cache_idx_k[l], state_ret[l],
                                            lw, lay, norm_f_g, last)
        for lst, val in zip(outs, (kp, vp, ip, sp, ks, vs, isl, ss)):
            lst.append(val)
    stacked = [jnp.stack(o) for o in outs]
    return (xp, xs, stacked[0], stacked[1], stacked[2], stacked[3].astype(x_prompt.dtype),
            stacked[4], stacked[5], stacked[6], stacked[7].astype(state_ret.dtype))
```

```python
import functools
import math

import jax
import jax.numpy as jnp
from jax import lax
from jax.experimental import pallas as pl
from jax.experimental.pallas import tpu as pltpu

F32 = jnp.float32
BF16 = jnp.bfloat16

CHUNK = 64
EPS = 1e-6
ROPE_THETA = 10000.0
N_HEADS_A = 16
N_KV_A = 4
HEAD_DIM_A = 128
N_HEADS_IDX = 16
HEAD_DIM_IDX = 64
TOPK_MAX = 256
N_HEADS_R = 8
KEY_DIM_R = 128
VAL_DIM_R = 256

LANES = 128
MXU_COLS = 256
V7X_VMEM_BYTES = 64 * 1024 * 1024
VMEM_LIMIT = V7X_VMEM_BYTES - 8 * 1024 * 1024
BF16_SUBLANES = 16
VT_ROWS = HEAD_DIM_A + BF16_SUBLANES
LOG2E = math.log2(math.e)
MAX_BISECT_ITERS = 64
NEG_BIG = -1e30


def _cparams(*sem):
    return pltpu.CompilerParams(dimension_semantics=sem, vmem_limit_bytes=VMEM_LIMIT)


def _dot(a, b):
    return jnp.dot(a, b, preferred_element_type=F32)


def _dot_nt(a, b):
    return lax.dot_general(a, b, (((1,), (1,)), ((), ())), preferred_element_type=F32)


def _rmsnorm_rows(x, g):
    return x * lax.rsqrt(jnp.mean(x * x, axis=-1, keepdims=True) + EPS) * g


def _rope_tables(pos):
    posf = pos.astype(F32)[:, None]

    def cs(half):
        inv_freq = ROPE_THETA ** (-jnp.arange(half, dtype=F32) / half)
        ang = posf * inv_freq[None, :]
        return jnp.cos(ang), jnp.sin(ang)

    c, s = cs(HEAD_DIM_A // 2)
    c128 = jnp.concatenate([c, c], axis=1)
    s128 = jnp.concatenate([-s, s], axis=1)
    c, s = cs(HEAD_DIM_IDX // 2)
    z = jnp.zeros_like(s)
    c64 = jnp.tile(c, (1, 4))
    a64 = jnp.tile(jnp.concatenate([-s, z], axis=1), (1, 2))
    b64 = jnp.tile(jnp.concatenate([z, s], axis=1), (1, 2))
    return jnp.stack([c128, s128, c64, a64, b64])


def _rope128(x, rope_ref):
    return x * rope_ref[0] + pltpu.roll(x, 64, 1) * rope_ref[1]


def _rope64(x, rope_ref):
    return (x * rope_ref[2] + pltpu.roll(x, 96, 1) * rope_ref[3]
            + pltpu.roll(x, 32, 1) * rope_ref[4])


def _rmsnorm_kernel(x_ref, g_ref, o_ref):
    o_ref[...] = _rmsnorm_rows(x_ref[...], g_ref[...]).astype(o_ref.dtype)


def _rmsnorm(x, g, tm):
    n, d = x.shape
    return pl.pallas_call(
        _rmsnorm_kernel,
        grid=(n // tm,),
        in_specs=[pl.BlockSpec((tm, d), lambda i: (i, 0)),
                  pl.BlockSpec((1, d), lambda i: (0, 0))],
        out_specs=pl.BlockSpec((tm, d), lambda i: (i, 0)),
        out_shape=jax.ShapeDtypeStruct((n, d), BF16),
        compiler_params=_cparams("parallel"),
        name="rmsnorm",
    )(x, g.reshape(1, d))


def _proj_main_kernel(h_ref, w_ref, rope_ref, o_ref, *, groups, tn):
    j = pl.program_id(1)
    for lo, hi, mode, scale in groups:
        @pl.when(jnp.logical_and(j >= lo, j < hi))
        def _(mode=mode, scale=scale):
            for b in range(tn // MXU_COLS):
                acc = _dot(h_ref[...], w_ref[:, b * MXU_COLS:(b + 1) * MXU_COLS])
                for c in range(MXU_COLS // LANES):
                    x = acc[:, c * LANES:(c + 1) * LANES]
                    if mode == "rope128":
                        y = _rope128(x, rope_ref)
                    elif mode == "rope64":
                        y = _rope64(x, rope_ref)
                    elif mode == "sigmoid":
                        y = 1.0 / (1.0 + jnp.exp(-x))
                    elif mode == "silu":
                        y = x / (1.0 + jnp.exp(-x))
                    else:
                        y = x
                    if scale != 1.0:
                        y = y * scale
                    col = b * MXU_COLS + c * LANES
                    o_ref[:, col:col + LANES] = y.astype(o_ref.dtype)


def _proj_main(h, w_main, rope, groups, tm, tn):
    n, d = h.shape
    p = w_main.shape[1]
    n_pos_tiles = rope.shape[1] // tm
    return pl.pallas_call(
        functools.partial(_proj_main_kernel, groups=groups, tn=tn),
        grid=(n // tm, p // tn),
        in_specs=[pl.BlockSpec((tm, d), lambda i, j: (i, 0)),
                  pl.BlockSpec((d, tn), lambda i, j: (0, j)),
                  pl.BlockSpec((5, tm, LANES), lambda i, j: (0, i % n_pos_tiles, 0))],
        out_specs=pl.BlockSpec((tm, tn), lambda i, j: (i, j)),
        out_shape=jax.ShapeDtypeStruct((n, p), BF16),
        compiler_params=_cparams("parallel", "arbitrary"),
        name="proj_main",
    )(h, w_main, rope)


def _proj_kv_kernel(h_ref, w_ref, rope_ref, ka_ref, va_ref, ki_ref, wt_ref,
                    kab_ref, vtb_ref, kia_ref, kib_ref, acc_ref, *, nk, idx_scale):
    acc_ref[...] = _dot(h_ref[...], w_ref[...])
    tm = acc_ref.shape[0]
    for c in range(N_KV_A):
        sl = slice(c * LANES, (c + 1) * LANES)
        y = _rope128(acc_ref[:, sl], rope_ref)
        ka_ref[pl.ds(c, tm, stride=N_KV_A), :] = y
        kab_ref[:, sl] = y.astype(BF16)
        va_ref[pl.ds(c, tm, stride=N_KV_A), :] = acc_ref[:, nk + c * LANES:nk + (c + 1) * LANES]
    ones_rows = jnp.where(lax.broadcasted_iota(jnp.int32, (VT_ROWS - HEAD_DIM_A, tm), 0) == 0, 1.0, 0.0)
    for g in range(N_KV_A):
        vg = acc_ref[:, nk + g * HEAD_DIM_A:nk + (g + 1) * HEAD_DIM_A]
        vtb_ref[g * VT_ROWS:g * VT_ROWS + HEAD_DIM_A, :] = vg.T.astype(BF16)
        vtb_ref[g * VT_ROWS + HEAD_DIM_A:(g + 1) * VT_ROWS, :] = ones_rows.astype(BF16)
    z = acc_ref[:, 2 * nk:2 * nk + LANES]
    y = _rope64(z, rope_ref)
    ki_ref[...] = y[:, :HEAD_DIM_IDX]
    lane = lax.broadcasted_iota(jnp.int32, y.shape, 1)
    ya = jnp.where(lane < HEAD_DIM_IDX, y, 0.0)
    kia_ref[...] = ya.astype(BF16)
    kib_ref[...] = pltpu.roll(ya, HEAD_DIM_IDX, 1).astype(BF16)
    wt_ref[...] = (z * idx_scale).T[HEAD_DIM_IDX:HEAD_DIM_IDX + N_HEADS_IDX, :]


def _proj_kv(h, w_kv, rope, tm):
    n, d = h.shape
    nk = N_KV_A * HEAD_DIM_A
    pw = w_kv.shape[1]
    n_pos_tiles = rope.shape[1] // tm
    idx_scale = (HEAD_DIM_IDX ** -0.5) * (N_HEADS_IDX ** -0.5)
    row = lambda i: (i, 0)
    return pl.pallas_call(
        functools.partial(_proj_kv_kernel, nk=nk, idx_scale=idx_scale),
        grid=(n // tm,),
        in_specs=[pl.BlockSpec((tm, d), row),
                  pl.BlockSpec((d, pw), lambda i: (0, 0)),
                  pl.BlockSpec((5, tm, LANES), lambda i: (0, i % n_pos_tiles, 0))],
        out_specs=[pl.BlockSpec((tm * N_KV_A, HEAD_DIM_A), row), pl.BlockSpec((tm * N_KV_A, HEAD_DIM_A), row),
                   pl.BlockSpec((tm, HEAD_DIM_IDX), row),
                   pl.BlockSpec((N_HEADS_IDX, tm), lambda i: (0, i)),
                   pl.BlockSpec((tm, nk), row),
                   pl.BlockSpec((N_KV_A * VT_ROWS, tm), lambda i: (0, i)),
                   pl.BlockSpec((tm, LANES), row), pl.BlockSpec((tm, LANES), row)],
        out_shape=[jax.ShapeDtypeStruct((n * N_KV_A, HEAD_DIM_A), F32),
                   jax.ShapeDtypeStruct((n * N_KV_A, HEAD_DIM_A), F32),
                   jax.ShapeDtypeStruct((n, HEAD_DIM_IDX), F32),
                   jax.ShapeDtypeStruct((N_HEADS_IDX, n), F32),
                   jax.ShapeDtypeStruct((n, nk), BF16),
                   jax.ShapeDtypeStruct((N_KV_A * VT_ROWS, n), BF16),
                   jax.ShapeDtypeStruct((n, LANES), BF16), jax.ShapeDtypeStruct((n, LANES), BF16)],
        scratch_shapes=[pltpu.VMEM((tm, pw), F32)],
        compiler_params=_cparams("parallel"),
        name="proj_kv",
    )(h, w_kv, rope)


def _index_kernel(qi_ref, wt_ref, kia_ref, kib_ref, m_ref, s_ref, *,
                  tq, ck, n_ck_total, l_valid, pos_base, causal, top_k):
    i = pl.program_id(1)
    if causal:
        n_c = (i * tq + tq + ck - 1) // ck
    else:
        n_c = n_ck_total
    qpos = pos_base + i * tq + lax.broadcasted_iota(jnp.int32, (1, tq), 1)
    lim = jnp.minimum(qpos - lax.rem(qpos, CHUNK) + CHUNK, l_valid)
    limf = lim.astype(F32)
    kprime = jnp.minimum(float(top_k), limf)
    inf = jnp.float32(jnp.inf)

    def key_index(off):
        return off + lax.broadcasted_iota(jnp.int32, (ck, tq), 0)

    def score_chunk(c, carry):
        rmax, rmin = carry
        off = pl.multiple_of(c * ck, ck)
        ka = kia_ref[pl.ds(off, ck), :]
        kb = kib_ref[pl.ds(off, ck), :]
        acc = jnp.zeros((ck, tq), F32)
        for p in range(N_HEADS_IDX // 2):
            qp = qi_ref[:, p * LANES:(p + 1) * LANES]
            acc += jnp.maximum(_dot_nt(ka, qp), 0.0) * wt_ref[2 * p:2 * p + 1, :]
            acc += jnp.maximum(_dot_nt(kb, qp), 0.0) * wt_ref[2 * p + 1:2 * p + 2, :]
        adm = key_index(off) < lim
        s_ref[pl.ds(off, ck), :] = jnp.where(adm, acc, -inf)
        rmax = jnp.maximum(rmax, jnp.max(jnp.where(adm, acc, -inf), axis=0, keepdims=True))
        rmin = jnp.minimum(rmin, jnp.min(jnp.where(adm, acc, inf), axis=0, keepdims=True))
        return rmax, rmin

    rmax, rmin = lax.fori_loop(0, n_c, score_chunk,
                               (jnp.full((1, tq), -inf, F32), jnp.full((1, tq), inf, F32)))

    fold_rows = min(ck, 64)

    def fold(ind):
        return ind.reshape(ck // fold_rows, fold_rows, tq).sum(axis=0)

    def count_ge(th):
        def body(c, acc):
            off = pl.multiple_of(c * ck, ck)
            return acc + fold(jnp.where(s_ref[pl.ds(off, ck), :] >= th, 1.0, 0.0))
        acc = lax.fori_loop(0, n_c, body, jnp.zeros((fold_rows, tq), F32))
        return acc.sum(axis=0, keepdims=True)

    def cond(st):
        it, lo, hi, cnt, stuck = st
        active = jnp.where(jnp.logical_and(cnt > kprime, stuck < 0.5), 1.0, 0.0)
        return jnp.logical_and(it < MAX_BISECT_ITERS, jnp.max(active) > 0.5)

    def body(st):
        it, lo, hi, cnt, stuck = st
        open_top = hi == inf
        mid = lo + 0.5 * (jnp.where(open_top, rmax, hi) - lo)
        mid = jnp.where(jnp.logical_and(open_top, mid <= lo), rmax, mid)
        c = count_ge(mid)
        ge = c >= kprime
        no_progress = jnp.logical_or(mid <= lo, mid >= hi)
        return (it + 1, jnp.where(ge, mid, lo), jnp.where(ge, hi, mid), jnp.where(ge, c, cnt),
                jnp.where(no_progress, 1.0, stuck))

    _, lo, hi, cnt, _ = lax.while_loop(
        cond, body, (jnp.int32(0), rmin, jnp.full((1, tq), inf, F32), limf, jnp.zeros((1, tq), F32)))

    unresolved = jnp.max(jnp.where(cnt > kprime, 1.0, 0.0)) > 0.5

    @pl.when(jnp.logical_not(unresolved))
    def _():
        def body(c, _):
            off = pl.multiple_of(c * ck, ck)
            m_ref[pl.ds(off, ck), :] = jnp.where(s_ref[pl.ds(off, ck), :] >= lo, 0.0, -inf).astype(m_ref.dtype)
            return 0
        lax.fori_loop(0, n_c, body, 0)

    @pl.when(unresolved)
    def _():
        need = kprime - count_ge(hi)

        def in_tie(blk):
            return jnp.logical_and(blk >= lo, blk < hi)

        def count_tie_below(jcut):
            def body(c, acc):
                off = pl.multiple_of(c * ck, ck)
                blk = s_ref[pl.ds(off, ck), :]
                e = jnp.logical_and(in_tie(blk), key_index(off).astype(F32) < jcut)
                return acc + fold(jnp.where(e, 1.0, 0.0))
            acc = lax.fori_loop(0, n_c, body, jnp.zeros((fold_rows, tq), F32))
            return acc.sum(axis=0, keepdims=True)

        def jbody(_, st):
            jlo, jhi = st
            mid = jnp.floor(0.5 * (jlo + jhi))
            ok = count_tie_below(mid) >= need
            return jnp.where(ok, jlo, mid + 1.0), jnp.where(ok, mid, jhi)

        n_total = n_ck_total * ck
        _, jcut = lax.fori_loop(0, int(math.ceil(math.log2(n_total + 1))), jbody,
                                (jnp.zeros((1, tq), F32), jnp.full((1, tq), float(n_total), F32)))

        def body(c, _):
            off = pl.multiple_of(c * ck, ck)
            blk = s_ref[pl.ds(off, ck), :]
            e = jnp.logical_and(in_tie(blk), key_index(off).astype(F32) < jcut)
            keep = jnp.logical_or(blk >= hi, e)
            m_ref[pl.ds(off, ck), :] = jnp.where(keep, 0.0, -inf).astype(m_ref.dtype)
            return 0
        lax.fori_loop(0, n_c, body, 0)

    def zero_body(c, _):
        off = pl.multiple_of(c * ck, ck)
        m_ref[pl.ds(off, ck), :] = jnp.full((ck, tq), -inf, m_ref.dtype)
        return 0
    lax.fori_loop(n_c, n_ck_total, zero_body, 0)


def _index_mask(p_main, qi_col_block, wt, kia, kib, *, tq, ck, l_valid, pos_base, causal, top_k):
    b, lp, _ = kia.shape
    nq = p_main.shape[0] // (b * tq)
    qw = N_HEADS_IDX * HEAD_DIM_IDX
    return pl.pallas_call(
        functools.partial(_index_kernel, tq=tq, ck=ck, n_ck_total=lp // ck, l_valid=l_valid,
                          pos_base=pos_base, causal=causal, top_k=top_k),
        grid=(b, nq),
        in_specs=[pl.BlockSpec((tq, qw), lambda bi, i: (bi * nq + i, qi_col_block)),
                  pl.BlockSpec((N_HEADS_IDX, tq), lambda bi, i: (0, bi * nq + i)),
                  pl.BlockSpec((None, lp, LANES), lambda bi, i: (bi, 0, 0)),
                  pl.BlockSpec((None, lp, LANES), lambda bi, i: (bi, 0, 0))],
        out_specs=pl.BlockSpec((None, lp, tq), lambda bi, i: (bi, 0, i)),
        out_shape=jax.ShapeDtypeStruct((b, lp, nq * tq), BF16),
        scratch_shapes=[pltpu.VMEM((lp, tq), F32)],
        compiler_params=_cparams("parallel", "parallel"),
        name="index_mask",
    )(p_main, wt, kia, kib)


def _attn_kernel(q_ref, k_ref, vt_ref, m_ref, o_ref, acc_ref, sa_ref, sb_ref, *,
                 tq, tk, n_kt_total, causal, group, kvb):
    i = pl.program_id(1)
    heads = kvb * group
    if causal:
        n_kt = (i * tq + tq + tk - 1) // tk
    else:
        n_kt = n_kt_total
    acc_ref[...] = jnp.zeros_like(acc_ref)

    def scores(kt, s_ref):
        off = pl.multiple_of(kt * tk, tk)
        for h in range(heads):
            kv = h // group
            k = k_ref[pl.ds(off, tk), kv * HEAD_DIM_A:(kv + 1) * HEAD_DIM_A]
            s_ref[h] = _dot_nt(k, q_ref[:, h * HEAD_DIM_A:(h + 1) * HEAD_DIM_A])

    def softmax_pv(kt, s_ref, ms):
        off = pl.multiple_of(kt * tk, tk)
        bias = m_ref[pl.ds(off, tk), :].astype(F32)
        out = []
        for h in range(heads):
            kv = h // group
            vt = vt_ref[kv * VT_ROWS:(kv + 1) * VT_ROWS, pl.ds(off, tk)]
            s = s_ref[h] + bias
            m_new = jnp.maximum(ms[h], jnp.max(s, axis=0, keepdims=True))
            alpha = jnp.exp2(ms[h] - m_new)
            p = jnp.exp2(s - m_new).astype(BF16)
            acc_ref[h] = acc_ref[h] * alpha + _dot(vt, p)
            out.append(m_new)
        return tuple(out)

    def body(j, ms):
        scores(2 * j + 1, sb_ref)
        ms = softmax_pv(2 * j, sa_ref, ms)
        scores(2 * j + 2, sa_ref)
        return softmax_pv(2 * j + 1, sb_ref, ms)

    scores(0, sa_ref)
    n_pairs = (n_kt - 1) // 2
    ms = lax.fori_loop(0, n_pairs, body, tuple(jnp.full((1, tq), NEG_BIG, F32) for _ in range(heads)))
    even = n_kt - 1 > 2 * n_pairs

    @pl.when(even)
    def _():
        scores(2 * n_pairs + 1, sb_ref)

    ms = softmax_pv(2 * n_pairs, sa_ref, ms)

    @pl.when(even)
    def _():
        softmax_pv(2 * n_pairs + 1, sb_ref, ms)

    for h in range(heads):
        a = acc_ref[h]
        o = a[:HEAD_DIM_A] / a[HEAD_DIM_A:HEAD_DIM_A + 1]
        o_ref[:, h * HEAD_DIM_A:(h + 1) * HEAD_DIM_A] = o.T.astype(o_ref.dtype)


def _attention(p_main, k_bf, vt_bf, mask_t, *, tq, tk, causal, kvb):
    b, lp, _ = k_bf.shape
    nq = mask_t.shape[2] // tq
    group = N_HEADS_A // N_KV_A
    heads = kvb * group
    gw = heads * HEAD_DIM_A
    return pl.pallas_call(
        functools.partial(_attn_kernel, tq=tq, tk=tk, n_kt_total=lp // tk, causal=causal, group=group,
                          kvb=kvb),
        grid=(b, nq, N_KV_A // kvb),
        in_specs=[pl.BlockSpec((tq, gw), lambda bi, i, g: (bi * nq + i, g)),
                  pl.BlockSpec((None, lp, kvb * HEAD_DIM_A), lambda bi, i, g: (bi, 0, g)),
                  pl.BlockSpec((kvb * VT_ROWS, lp), lambda bi, i, g: (g, bi)),
                  pl.BlockSpec((None, lp, tq), lambda bi, i, g: (bi, 0, i))],
        out_specs=pl.BlockSpec((tq, gw), lambda bi, i, g: (bi * nq + i, g)),
        out_shape=jax.ShapeDtypeStruct((b * nq * tq, N_HEADS_A * HEAD_DIM_A), BF16),
        scratch_shapes=[pltpu.VMEM((heads, VT_ROWS, tq), F32),
                        pltpu.VMEM((heads, tk, tq), F32), pltpu.VMEM((heads, tk, tq), F32)],
        compiler_params=_cparams("parallel", "parallel", "arbitrary"),
        name="attention",
    )(p_main, k_bf, vt_bf, mask_t)


def _retention_tables(c):
    log_gamma = jnp.log1p(-(2.0 ** (-5.0 - jnp.arange(N_HEADS_R, dtype=F32))))
    idx = jnp.arange(c, dtype=F32)
    diff = idx[:, None] - idx[None, :]
    decay = jnp.where(diff[None] >= 0,
                      jnp.exp(jnp.maximum(diff, 0.0)[None] * log_gamma[:, None, None]), 0.0)
    cross = jnp.exp((idx + 1.0)[None, :] * log_gamma[:, None])
    kdec = jnp.exp((c - 1.0 - idx)[None, :] * log_gamma[:, None])
    full = jnp.exp(c * log_gamma)
    bc = lambda a, w: jnp.broadcast_to(a[..., None], a.shape + (w,))
    return decay, bc(cross, KEY_DIM_R), bc(kdec, KEY_DIM_R), bc(full[:, None], VAL_DIM_R)


def _retention_kernel(q_ref, k_ref, v_ref, g_ref, dec_ref, qsc_ref, ksc_ref, gc_ref, s0_ref,
                      o_ref, st_ref):
    @pl.when(pl.program_id(1) == 0)
    def _():
        st_ref[...] = s0_ref[...]

    for h in range(N_HEADS_R):
        ks = slice(h * KEY_DIM_R, (h + 1) * KEY_DIM_R)
        vs = slice(h * VAL_DIM_R, (h + 1) * VAL_DIM_R)
        q = q_ref[:, ks]
        k = k_ref[:, ks]
        v = v_ref[:, vs]
        st = st_ref[h]
        inner = _dot_nt(q, k) * dec_ref[h]
        qd = (q.astype(F32) * qsc_ref[h]).astype(BF16)
        o = _dot(inner.astype(BF16), v) + _dot(qd, st.astype(BF16))
        kdt = (k.astype(F32) * ksc_ref[h]).T.astype(BF16)
        st_ref[h] = st * gc_ref[h] + _dot(kdt, v)
        o = o * lax.rsqrt(jnp.mean(o * o, axis=-1, keepdims=True) + EPS)
        o_ref[:, vs] = (o * g_ref[:, vs].astype(F32)).astype(o_ref.dtype)


def _retention(p_main, cols, state0, c):
    b = state0.shape[0]
    nc = p_main.shape[0] // (b * c)
    kw = N_HEADS_R * KEY_DIM_R
    vw = N_HEADS_R * VAL_DIM_R
    dec, qsc, ksc, gcs = _retention_tables(c)
    qb, kb, vb, gb = cols
    const3 = lambda bi, ci: (0, 0, 0)
    st_spec = pl.BlockSpec((None, N_HEADS_R, KEY_DIM_R, VAL_DIM_R), lambda bi, ci: (bi, 0, 0, 0))
    return pl.pallas_call(
        _retention_kernel,
        grid=(b, nc),
        in_specs=[pl.BlockSpec((c, kw), lambda bi, ci: (bi * nc + ci, qb)),
                  pl.BlockSpec((c, kw), lambda bi, ci: (bi * nc + ci, kb)),
                  pl.BlockSpec((c, vw), lambda bi, ci: (bi * nc + ci, vb)),
                  pl.BlockSpec((c, vw), lambda bi, ci: (bi * nc + ci, gb)),
                  pl.BlockSpec(dec.shape, const3), pl.BlockSpec(qsc.shape, const3),
                  pl.BlockSpec(ksc.shape, const3), pl.BlockSpec(gcs.shape, const3),
                  st_spec],
        out_specs=[pl.BlockSpec((c, vw), lambda bi, ci: (bi * nc + ci, 0)), st_spec],
        out_shape=[jax.ShapeDtypeStruct((b * nc * c, vw), BF16),
                   jax.ShapeDtypeStruct(state0.shape, F32)],
        compiler_params=_cparams("parallel", "arbitrary"),
        name="retention",
    )(p_main, p_main, p_main, p_main, dec, qsc, ksc, gcs, state0)


def _merge_kernel(oa_ref, ob_ref, wa_ref, wb_ref, sa_ref, sb_ref, o_ref):
    for b in range(o_ref.shape[1] // MXU_COLS):
        sl = slice(b * MXU_COLS, (b + 1) * MXU_COLS)
        m = (_dot(oa_ref[...], wa_ref[:, sl]) * sa_ref[:, sl].astype(F32)
             + _dot(ob_ref[...], wb_ref[:, sl]) * sb_ref[:, sl].astype(F32))
        o_ref[:, sl] = m.astype(o_ref.dtype)


def _merge(o_a, o_b, w_pa, w_pb, p_main, ga_off, gb_off, tm, tn):
    n, wa = o_a.shape
    wb = o_b.shape[1]
    d = w_pa.shape[1]
    return pl.pallas_call(
        _merge_kernel,
        grid=(n // tm, d // tn),
        in_specs=[pl.BlockSpec((tm, wa), lambda i, j: (i, 0)),
                  pl.BlockSpec((tm, wb), lambda i, j: (i, 0)),
                  pl.BlockSpec((wa, tn), lambda i, j: (0, j)),
                  pl.BlockSpec((wb, tn), lambda i, j: (0, j)),
                  pl.BlockSpec((tm, tn), lambda i, j: (i, ga_off // tn + j)),
                  pl.BlockSpec((tm, tn), lambda i, j: (i, gb_off // tn + j))],
        out_specs=pl.BlockSpec((tm, tn), lambda i, j: (i, j)),
        out_shape=jax.ShapeDtypeStruct((n, d), BF16),
        compiler_params=_cparams("parallel", "arbitrary"),
        name="merge",
    )(o_a, o_b, w_pa, w_pb, p_main, p_main)


def _out_proj_kernel(x_ref, m_ref, w_ref, x1_ref):
    x1_ref[...] = x_ref[...] + _dot(m_ref[...], w_ref[...])


def _out_proj(x, merged, w_o, tm):
    n, d = x.shape
    row = lambda i: (i, 0)
    return pl.pallas_call(
        _out_proj_kernel,
        grid=(n // tm,),
        in_specs=[pl.BlockSpec((tm, d), row), pl.BlockSpec((tm, d), row),
                  pl.BlockSpec((d, d), lambda i: (0, 0))],
        out_specs=pl.BlockSpec((tm, d), row),
        out_shape=jax.ShapeDtypeStruct((n, d), F32),
        compiler_params=_cparams("parallel"),
        name="out_proj",
    )(x, merged, w_o)


def _ffn_kernel(x1_ref, g2_ref, wg_ref, wu_ref, wd_ref, g_ref, y_ref, h_ref, *, final_norm):
    c = pl.program_id(1)

    @pl.when(c == 0)
    def _():
        x1 = x1_ref[...]
        y_ref[...] = x1
        h_ref[...] = _rmsnorm_rows(x1, g2_ref[...]).astype(h_ref.dtype)

    acts = []
    for b in range(wg_ref.shape[1] // MXU_COLS):
        sl = slice(b * MXU_COLS, (b + 1) * MXU_COLS)
        a = _dot(h_ref[...], wg_ref[:, sl])
        u = _dot(h_ref[...], wu_ref[:, sl])
        acts.append((a / (1.0 + jnp.exp(-a)) * u).astype(BF16))
    upd = _dot(acts[0], wd_ref[0:MXU_COLS, :])
    for b in range(1, len(acts)):
        upd += _dot(acts[b], wd_ref[b * MXU_COLS:(b + 1) * MXU_COLS, :])
    y_ref[...] += upd

    if final_norm:
        @pl.when(c == pl.num_programs(1) - 1)
        def _():
            y_ref[...] = _rmsnorm_rows(y_ref[...], g_ref[...])


def _ffn(x1, g2, wg, wu, wd, gf, tm, tc, final_norm):
    n, d = x1.shape
    f = wg.shape[1]
    row = lambda i, c: (i, 0)
    vec = pl.BlockSpec((1, d), lambda i, c: (0, 0))
    return pl.pallas_call(
        functools.partial(_ffn_kernel, final_norm=final_norm),
        grid=(n // tm, f // tc),
        in_specs=[pl.BlockSpec((tm, d), row, pipeline_mode=pl.Buffered(1)), vec,
                  pl.BlockSpec((d, tc), lambda i, c: (0, c)),
                  pl.BlockSpec((d, tc), lambda i, c: (0, c)),
                  pl.BlockSpec((tc, d), lambda i, c: (c, 0)), vec],
        out_specs=pl.BlockSpec((tm, d), row),
        out_shape=jax.ShapeDtypeStruct((n, d), F32),
        scratch_shapes=[pltpu.VMEM((tm, d), BF16)],
        compiler_params=_cparams("parallel", "arbitrary"),
        name="ffn",
    )(x1, g2.reshape(1, d), wg, wu, wd, gf.reshape(1, d))


class _MainLayout:
    def __init__(self, d_model):
        wa = N_HEADS_A * HEAD_DIM_A
        wb = N_HEADS_R * VAL_DIM_R
        kr = N_HEADS_R * KEY_DIM_R
        qi = N_HEADS_IDX * HEAD_DIM_IDX
        order = [("qa", wa, "rope128", HEAD_DIM_A ** -0.5 * LOG2E), ("vr", wb, "plain", 1.0),
                 ("gr", wb, "silu", 1.0), ("ga", d_model, "sigmoid", 1.0), ("gb", d_model, "sigmoid", 1.0),
                 ("qi", qi, "rope64", 1.0), ("qr", kr, "rope128", 1.0),
                 ("kr", kr, "rope128", KEY_DIM_R ** -0.5)]
        self.off, self.width = {}, {}
        self.order = order
        o = 0
        for name, w, _, _ in order:
            self.off[name], self.width[name] = o, w
            o += w
        self.total = o

    def groups(self, tn):
        out = []
        for name, w, mode, scale in self.order:
            assert self.off[name] % tn == 0 and w % tn == 0
            out.append((self.off[name] // tn, (self.off[name] + w) // tn, mode, scale))
        return tuple(out)

    def block(self, name, width=None):
        width = width or self.width[name]
        assert self.off[name] % width == 0
        return self.off[name] // width


def _split_w_in(w, d_model):
    wa = N_HEADS_A * HEAD_DIM_A
    nk = N_KV_A * HEAD_DIM_A
    wb = N_HEADS_R * VAL_DIM_R
    kr = N_HEADS_R * KEY_DIM_R
    names = ("qa", "ka", "va", "qi", "ki", "wi", "qr", "kr", "vr", "gr", "ga", "gb")
    widths = (wa, nk, nk, N_HEADS_IDX * HEAD_DIM_IDX, HEAD_DIM_IDX, N_HEADS_IDX, kr, kr, wb, wb,
              d_model, d_model)
    parts, o = {}, 0
    for name, wd in zip(names, widths):
        parts[name] = w[:, o:o + wd]
        o += wd
    assert o == w.shape[1]
    return parts


def _tile(n, pref):
    t = min(n, pref)
    assert n % t == 0
    return t


def _dense_pre(x2, pos_rows, lw, lay):
    n = x2.shape[0]
    tm = _tile(n, 1024)
    rope = _rope_tables(pos_rows)
    h = _rmsnorm(x2, lw["g1"], tm)
    tn = 1024
    p_main = _proj_main(h, lw["w_main"], rope, lay.groups(tn), tm, tn)
    kv = _proj_kv(h, lw["w_kv"], rope, _tile(n, 512))
    return p_main, kv


def _dense_post(x2, o_a, o_b, p_main, lw, lay, gf, final_norm):
    n = x2.shape[0]
    tm = _tile(n, 1024)
    merged = _merge(o_a, o_b, lw["w_pa"], lw["w_pb"], p_main, lay.off["ga"], lay.off["gb"], tm, 512)
    x1 = _out_proj(x2, merged, lw["w_o"], _tile(n, 512))
    return _ffn(x1, lw["g2"], lw["wg"], lw["wu"], lw["wd"], gf, tm, 512, final_norm)


def _prompt_layer(xp, lw, lay, gf, final_norm):
    b, t, d = xp.shape
    x2 = xp.reshape(b * t, d)
    p_main, (ka, va, ki, wt, ka_bf, vt_bf, kia, kib) = _dense_pre(x2, jnp.arange(t, dtype=jnp.int32), lw, lay)
    top_k = min(TOPK_MAX, t // 4)
    tq, ck = _tile(t, 256), _tile(t, 512)
    mask_t = _index_mask(p_main, lay.block("qi"), wt, kia.reshape(b, t, LANES), kib.reshape(b, t, LANES),
                         tq=tq, ck=ck, l_valid=t, pos_base=0, causal=True, top_k=top_k)
    nk = N_KV_A * HEAD_DIM_A
    o_a = _attention(p_main, ka_bf.reshape(b, t, nk), vt_bf, mask_t, tq=tq, tk=ck, causal=True, kvb=2)
    state0 = jnp.zeros((b, N_HEADS_R, KEY_DIM_R, VAL_DIM_R), F32)
    cols = (lay.block("qr"), lay.block("kr"), lay.block("vr"), lay.block("gr"))
    o_b, st = _retention(p_main, cols, state0, _tile(t, 256))
    y = _dense_post(x2, o_a, o_b, p_main, lw, lay, gf, final_norm)
    return (y.reshape(b, t, d), ka.reshape(b, t, N_KV_A, HEAD_DIM_A), va.reshape(b, t, N_KV_A, HEAD_DIM_A),
            ki.reshape(b, t, HEAD_DIM_IDX), st)


def _sample_layer(xs, cache_k, cache_v, cache_i, state, lw, lay, gf, final_norm):
    b, t, d = xs.shape
    past = cache_k.shape[1]
    nk = N_KV_A * HEAD_DIM_A
    x2 = xs.reshape(b * t, d)
    pos = jnp.tile(past + jnp.arange(t, dtype=jnp.int32), b)
    p_main, (ka, va, ki, wt, ka_bf, vt_bf, kia, kib) = _dense_pre(x2, pos, lw, lay)

    tq = LANES
    l_valid = past + t
    tk = 3 * LANES
    lp = -(-l_valid // tk) * tk
    pq = jnp.pad(p_main.reshape(b, t, -1), ((0, 0), (0, tq - t), (0, 0)), mode="edge").reshape(b * tq, -1)
    wtq = jnp.pad(wt.reshape(N_HEADS_IDX, b, t), ((0, 0), (0, 0), (0, tq - t)),
                  mode="edge").reshape(N_HEADS_IDX, b * tq)
    kpad = ((0, 0), (0, lp - l_valid), (0, 0))
    ci = cache_i.astype(BF16)
    zi = jnp.zeros_like(ci)
    kia_all = jnp.pad(jnp.concatenate([jnp.concatenate([ci, zi], -1), kia.reshape(b, t, LANES)], 1), kpad)
    kib_all = jnp.pad(jnp.concatenate([jnp.concatenate([zi, ci], -1), kib.reshape(b, t, LANES)], 1), kpad)
    k_all = jnp.pad(jnp.concatenate([cache_k.reshape(b, past, nk).astype(BF16), ka_bf.reshape(b, t, nk)], 1), kpad)
    vt_new = vt_bf.reshape(N_KV_A, VT_ROWS, b, t)
    vt_cache = cache_v.reshape(b, past, N_KV_A, HEAD_DIM_A).transpose(2, 3, 0, 1).astype(BF16)
    ones_rows = jnp.zeros((N_KV_A, VT_ROWS - HEAD_DIM_A, b, past), BF16).at[:, 0].set(1.0)
    vt_all = jnp.concatenate([jnp.concatenate([vt_cache, ones_rows], 1), vt_new], 3)
    vt_all = jnp.pad(vt_all, ((0, 0), (0, 0), (0, 0), (0, lp - l_valid))).reshape(N_KV_A * VT_ROWS, b * lp)

    top_k = min(TOPK_MAX, l_valid // 4)
    mask_t = _index_mask(pq, lay.block("qi"), wtq, kia_all, kib_all, tq=tq, ck=tk, l_valid=l_valid,
                         pos_base=past, causal=False, top_k=top_k)
    o_a = _attention(pq, k_all, vt_all, mask_t, tq=tq, tk=tk, causal=False, kvb=1)
    o_a = o_a.reshape(b, tq, -1)[:, :t].reshape(b * t, -1)
    cols = (lay.block("qr"), lay.block("kr"), lay.block("vr"), lay.block("gr"))
    o_b, st = _retention(p_main, cols, state.astype(F32), t)
    y = _dense_post(x2, o_a, o_b, p_main, lw, lay, gf, final_norm)
    return (y.reshape(b, t, d), ka.reshape(b, t, N_KV_A, HEAD_DIM_A), va.reshape(b, t, N_KV_A, HEAD_DIM_A),
            ki.reshape(b, t, HEAD_DIM_IDX), st)


def kernel(x_prompt, x_sample, cache_k, cache_v, cache_idx_k, state_ret, norm1_g, w_in, w_pa, w_pb, w_o,
           norm2_g, w_ffn_gate, w_ffn_up, w_ffn_down, norm_f_g):
    assert HEAD_DIM_A == LANES and HEAD_DIM_IDX * 2 == LANES and N_HEADS_IDX % 2 == 0
    depth, d_model = norm1_g.shape
    lay = _MainLayout(d_model)
    xp, xs = x_prompt, x_sample
    outs = [[] for _ in range(8)]
    for l in range(depth):
        parts = _split_w_in(w_in[l], d_model)
        pad = jnp.zeros((d_model, LANES - HEAD_DIM_IDX - N_HEADS_IDX), F32)
        lw = {
            "g1": norm1_g[l], "g2": norm2_g[l],
            "w_main": jnp.concatenate([parts[name] for name, _, _, _ in lay.order], axis=1).astype(BF16),
            "w_kv": jnp.concatenate([parts["ka"], parts["va"], parts["ki"], parts["wi"], pad], axis=1).astype(BF16),
            "w_pa": w_pa[l].astype(BF16), "w_pb": w_pb[l].astype(BF16), "w_o": w_o[l].astype(BF16),
            "wg": w_ffn_gate[l].astype(BF16), "wu": w_ffn_up[l].astype(BF16), "wd": w_ffn_down[l].astype(BF16),
        }
        last = l == depth - 1
        xp, kp, vp, ip, sp = _prompt_layer(xp, lw, lay, norm_f_g, last)
        xs, ks, vs, isl, ss = _sample_layer(xs, cache_k[l], cache_v[l], cache_idx_k[l], state_ret[l],
                                            lw, lay, norm_f_g, last)
        for lst, val in zip(outs, (kp, vp, ip, sp, ks, vs, isl, ss)):
            lst.append(val)
    stacked = [jnp.stack(o) for o in outs]
    return (xp, xs, stacked[0], stacked[1], stacked[2], stacked[3].astype(x_prompt.dtype),
            stacked[4], stacked[5], stacked[6], stacked[7].astype(state_ret.dtype))
```

```python
import functools
import math

import jax
import jax.numpy as jnp
from jax import lax
from jax.experimental import pallas as pl
from jax.experimental.pallas import tpu as pltpu

F32 = jnp.float32
BF16 = jnp.bfloat16

CHUNK = 64
EPS = 1e-6
ROPE_THETA = 10000.0
N_HEADS_A = 16
N_KV_A = 4
HEAD_DIM_A = 128
N_HEADS_IDX = 16
HEAD_DIM_IDX = 64
TOPK_MAX = 256
N_HEADS_R = 8
KEY_DIM_R = 128
VAL_DIM_R = 256

LANES = 128
MXU_COLS = 256
V7X_VMEM_BYTES = 64 * 1024 * 1024
VMEM_LIMIT = V7X_VMEM_BYTES - 8 * 1024 * 1024
BF16_SUBLANES = 16
VT_ROWS = HEAD_DIM_A + BF16_SUBLANES
LOG2E = math.log2(math.e)
MAX_BISECT_ITERS = 64
NEG_BIG = -1e30


def _cparams(*sem):
    return pltpu.CompilerParams(dimension_semantics=sem, vmem_limit_bytes=VMEM_LIMIT)


def _dot(a, b):
    return jnp.dot(a, b, preferred_element_type=F32)


def _dot_nt(a, b):
    return lax.dot_general(a, b, (((1,), (1,)), ((), ())), preferred_element_type=F32)


def _rmsnorm_rows(x, g):
    return x * lax.rsqrt(jnp.mean(x * x, axis=-1, keepdims=True) + EPS) * g


def _rope_tables(pos):
    posf = pos.astype(F32)[:, None]

    def cs(half):
        inv_freq = ROPE_THETA ** (-jnp.arange(half, dtype=F32) / half)
        ang = posf * inv_freq[None, :]
        return jnp.cos(ang), jnp.sin(ang)

    c, s = cs(HEAD_DIM_A // 2)
    c128 = jnp.concatenate([c, c], axis=1)
    s128 = jnp.concatenate([-s, s], axis=1)
    c, s = cs(HEAD_DIM_IDX // 2)
    z = jnp.zeros_like(s)
    c64 = jnp.tile(c, (1, 4))
    a64 = jnp.tile(jnp.concatenate([-s, z], axis=1), (1, 2))
    b64 = jnp.tile(jnp.concatenate([z, s], axis=1), (1, 2))
    return jnp.stack([c128, s128, c64, a64, b64])


def _rope128(x, rope_ref):
    return x * rope_ref[0] + pltpu.roll(x, 64, 1) * rope_ref[1]


def _rope64(x, rope_ref):
    return (x * rope_ref[2] + pltpu.roll(x, 96, 1) * rope_ref[3]
            + pltpu.roll(x, 32, 1) * rope_ref[4])


def _proj_main_kernel(h_ref, w_ref, rope_ref, o_ref, *, groups, tn):
    j = pl.program_id(1)
    for lo, hi, mode, scale in groups:
        @pl.when(jnp.logical_and(j >= lo, j < hi))
        def _(mode=mode, scale=scale):
            for b in range(tn // MXU_COLS):
                acc = _dot(h_ref[...], w_ref[:, b * MXU_COLS:(b + 1) * MXU_COLS])
                for c in range(MXU_COLS // LANES):
                    x = acc[:, c * LANES:(c + 1) * LANES]
                    if mode == "rope128":
                        y = _rope128(x, rope_ref)
                    elif mode == "rope64":
                        y = _rope64(x, rope_ref)
                    elif mode == "sigmoid":
                        y = 1.0 / (1.0 + jnp.exp(-x))
                    elif mode == "silu":
                        y = x / (1.0 + jnp.exp(-x))
                    else:
                        y = x
                    if scale != 1.0:
                        y = y * scale
                    col = b * MXU_COLS + c * LANES
                    o_ref[:, col:col + LANES] = y.astype(o_ref.dtype)


def _proj_main(h, w_main, rope, groups, tm, tn):
    n, d = h.shape
    p = w_main.shape[1]
    n_pos_tiles = rope.shape[1] // tm
    return pl.pallas_call(
        functools.partial(_proj_main_kernel, groups=groups, tn=tn),
        grid=(n // tm, p // tn),
        in_specs=[pl.BlockSpec((tm, d), lambda i, j: (i, 0)),
                  pl.BlockSpec((d, tn), lambda i, j: (0, j)),
                  pl.BlockSpec((5, tm, LANES), lambda i, j: (0, i % n_pos_tiles, 0))],
        out_specs=pl.BlockSpec((tm, tn), lambda i, j: (i, j)),
        out_shape=jax.ShapeDtypeStruct((n, p), BF16),
        compiler_params=_cparams("parallel", "arbitrary"),
        name="proj_main",
    )(h, w_main, rope)


def _proj_kv_kernel(x_ref, g_ref, w_ref, rope_ref, h_ref, ka_ref, va_ref, ki_ref, wt_ref,
                    kab_ref, vtb_ref, kia_ref, kib_ref, acc_ref, *, nk, idx_scale):
    h_ref[...] = _rmsnorm_rows(x_ref[...], g_ref[...]).astype(h_ref.dtype)
    acc_ref[...] = _dot(h_ref[...], w_ref[...])
    tm = acc_ref.shape[0]
    for c in range(N_KV_A):
        sl = slice(c * LANES, (c + 1) * LANES)
        y = _rope128(acc_ref[:, sl], rope_ref)
        ka_ref[pl.ds(c, tm, stride=N_KV_A), :] = y
        kab_ref[:, sl] = y.astype(BF16)
        va_ref[pl.ds(c, tm, stride=N_KV_A), :] = acc_ref[:, nk + c * LANES:nk + (c + 1) * LANES]
    ones_rows = jnp.where(lax.broadcasted_iota(jnp.int32, (VT_ROWS - HEAD_DIM_A, tm), 0) == 0, 1.0, 0.0)
    for g in range(N_KV_A):
        vg = acc_ref[:, nk + g * HEAD_DIM_A:nk + (g + 1) * HEAD_DIM_A]
        vtb_ref[g * VT_ROWS:g * VT_ROWS + HEAD_DIM_A, :] = vg.T.astype(BF16)
        vtb_ref[g * VT_ROWS + HEAD_DIM_A:(g + 1) * VT_ROWS, :] = ones_rows.astype(BF16)
    z = acc_ref[:, 2 * nk:2 * nk + LANES]
    y = _rope64(z, rope_ref)
    ki_ref[...] = y[:, :HEAD_DIM_IDX]
    lane = lax.broadcasted_iota(jnp.int32, y.shape, 1)
    ya = jnp.where(lane < HEAD_DIM_IDX, y, 0.0)
    kia_ref[...] = ya.astype(BF16)
    kib_ref[...] = pltpu.roll(ya, HEAD_DIM_IDX, 1).astype(BF16)
    wt_ref[...] = (z * idx_scale).T[HEAD_DIM_IDX:HEAD_DIM_IDX + N_HEADS_IDX, :]


def _proj_kv(x, g1, w_kv, rope, tm):
    n, d = x.shape
    nk = N_KV_A * HEAD_DIM_A
    pw = w_kv.shape[1]
    n_pos_tiles = rope.shape[1] // tm
    idx_scale = (HEAD_DIM_IDX ** -0.5) * (N_HEADS_IDX ** -0.5)
    row = lambda i: (i, 0)
    return pl.pallas_call(
        functools.partial(_proj_kv_kernel, nk=nk, idx_scale=idx_scale),
        grid=(n // tm,),
        in_specs=[pl.BlockSpec((tm, d), row),
                  pl.BlockSpec((1, d), lambda i: (0, 0)),
                  pl.BlockSpec((d, pw), lambda i: (0, 0)),
                  pl.BlockSpec((5, tm, LANES), lambda i: (0, i % n_pos_tiles, 0))],
        out_specs=[pl.BlockSpec((tm, d), row),
                   pl.BlockSpec((tm * N_KV_A, HEAD_DIM_A), row), pl.BlockSpec((tm * N_KV_A, HEAD_DIM_A), row),
                   pl.BlockSpec((tm, HEAD_DIM_IDX), row),
                   pl.BlockSpec((N_HEADS_IDX, tm), lambda i: (0, i)),
                   pl.BlockSpec((tm, nk), row),
                   pl.BlockSpec((N_KV_A * VT_ROWS, tm), lambda i: (0, i)),
                   pl.BlockSpec((tm, LANES), row), pl.BlockSpec((tm, LANES), row)],
        out_shape=[jax.ShapeDtypeStruct((n, d), BF16),
                   jax.ShapeDtypeStruct((n * N_KV_A, HEAD_DIM_A), F32),
                   jax.ShapeDtypeStruct((n * N_KV_A, HEAD_DIM_A), F32),
                   jax.ShapeDtypeStruct((n, HEAD_DIM_IDX), F32),
                   jax.ShapeDtypeStruct((N_HEADS_IDX, n), F32),
                   jax.ShapeDtypeStruct((n, nk), BF16),
                   jax.ShapeDtypeStruct((N_KV_A * VT_ROWS, n), BF16),
                   jax.ShapeDtypeStruct((n, LANES), BF16), jax.ShapeDtypeStruct((n, LANES), BF16)],
        scratch_shapes=[pltpu.VMEM((tm, pw), F32)],
        compiler_params=_cparams("parallel"),
        name="proj_kv",
    )(x, g1.reshape(1, d), w_kv, rope)


def _index_kernel(qi_ref, wt_ref, kia_ref, kib_ref, m_ref, s_ref, *,
                  tq, ck, n_ck_total, l_valid, pos_base, causal, top_k):
    i = pl.program_id(1)
    if causal:
        n_c = (i * tq + tq + ck - 1) // ck
    else:
        n_c = n_ck_total
    qpos = pos_base + i * tq + lax.broadcasted_iota(jnp.int32, (1, tq), 1)
    lim = jnp.minimum(qpos - lax.rem(qpos, CHUNK) + CHUNK, l_valid)
    limf = lim.astype(F32)
    kprime = jnp.minimum(float(top_k), limf)
    inf = jnp.float32(jnp.inf)

    def key_index(off):
        return off + lax.broadcasted_iota(jnp.int32, (ck, tq), 0)

    def score_chunk(c, carry):
        rmax, rmin = carry
        off = pl.multiple_of(c * ck, ck)
        ka = kia_ref[pl.ds(off, ck), :]
        kb = kib_ref[pl.ds(off, ck), :]
        acc = jnp.zeros((ck, tq), F32)
        for p in range(N_HEADS_IDX // 2):
            qp = qi_ref[:, p * LANES:(p + 1) * LANES]
            acc += jnp.maximum(_dot_nt(ka, qp), 0.0) * wt_ref[2 * p:2 * p + 1, :]
            acc += jnp.maximum(_dot_nt(kb, qp), 0.0) * wt_ref[2 * p + 1:2 * p + 2, :]
        adm = key_index(off) < lim
        s_ref[pl.ds(off, ck), :] = jnp.where(adm, acc, -inf)
        rmax = jnp.maximum(rmax, jnp.max(jnp.where(adm, acc, -inf), axis=0, keepdims=True))
        rmin = jnp.minimum(rmin, jnp.min(jnp.where(adm, acc, inf), axis=0, keepdims=True))
        return rmax, rmin

    rmax, rmin = lax.fori_loop(0, n_c, score_chunk,
                               (jnp.full((1, tq), -inf, F32), jnp.full((1, tq), inf, F32)))

    fold_rows = min(ck, 64)

    def fold(ind):
        return ind.reshape(ck // fold_rows, fold_rows, tq).sum(axis=0)

    def count_ge(th):
        def body(c, acc):
            off = pl.multiple_of(c * ck, ck)
            return acc + fold(jnp.where(s_ref[pl.ds(off, ck), :] >= th, 1.0, 0.0))
        acc = lax.fori_loop(0, n_c, body, jnp.zeros((fold_rows, tq), F32))
        return acc.sum(axis=0, keepdims=True)

    def cond(st):
        it, lo, hi, cnt, stuck = st
        active = jnp.where(jnp.logical_and(cnt > kprime, stuck < 0.5), 1.0, 0.0)
        return jnp.logical_and(it < MAX_BISECT_ITERS, jnp.max(active) > 0.5)

    def body(st):
        it, lo, hi, cnt, stuck = st
        open_top = hi == inf
        mid = lo + 0.5 * (jnp.where(open_top, rmax, hi) - lo)
        mid = jnp.where(jnp.logical_and(open_top, mid <= lo), rmax, mid)
        c = count_ge(mid)
        ge = c >= kprime
        no_progress = jnp.logical_or(mid <= lo, mid >= hi)
        return (it + 1, jnp.where(ge, mid, lo), jnp.where(ge, hi, mid), jnp.where(ge, c, cnt),
                jnp.where(no_progress, 1.0, stuck))

    _, lo, hi, cnt, _ = lax.while_loop(
        cond, body, (jnp.int32(0), rmin, jnp.full((1, tq), inf, F32), limf, jnp.zeros((1, tq), F32)))

    unresolved = jnp.max(jnp.where(cnt > kprime, 1.0, 0.0)) > 0.5

    @pl.when(jnp.logical_not(unresolved))
    def _():
        def body(c, _):
            off = pl.multiple_of(c * ck, ck)
            m_ref[pl.ds(off, ck), :] = jnp.where(s_ref[pl.ds(off, ck), :] >= lo, 0.0, -inf).astype(m_ref.dtype)
            return 0
        lax.fori_loop(0, n_c, body, 0)

    @pl.when(unresolved)
    def _():
        need = kprime - count_ge(hi)

        def in_tie(blk):
            return jnp.logical_and(blk >= lo, blk < hi)

        def count_tie_below(jcut):
            def body(c, acc):
                off = pl.multiple_of(c * ck, ck)
                blk = s_ref[pl.ds(off, ck), :]
                e = jnp.logical_and(in_tie(blk), key_index(off).astype(F32) < jcut)
                return acc + fold(jnp.where(e, 1.0, 0.0))
            acc = lax.fori_loop(0, n_c, body, jnp.zeros((fold_rows, tq), F32))
            return acc.sum(axis=0, keepdims=True)

        def jbody(_, st):
            jlo, jhi = st
            mid = jnp.floor(0.5 * (jlo + jhi))
            ok = count_tie_below(mid) >= need
            return jnp.where(ok, jlo, mid + 1.0), jnp.where(ok, mid, jhi)

        n_total = n_ck_total * ck
        _, jcut = lax.fori_loop(0, int(math.ceil(math.log2(n_total + 1))), jbody,
                                (jnp.zeros((1, tq), F32), jnp.full((1, tq), float(n_total), F32)))

        def body(c, _):
            off = pl.multiple_of(c * ck, ck)
            blk = s_ref[pl.ds(off, ck), :]
            e = jnp.logical_and(in_tie(blk), key_index(off).astype(F32) < jcut)
            keep = jnp.logical_or(blk >= hi, e)
            m_ref[pl.ds(off, ck), :] = jnp.where(keep, 0.0, -inf).astype(m_ref.dtype)
            return 0
        lax.fori_loop(0, n_c, body, 0)

    def zero_body(c, _):
        off = pl.multiple_of(c * ck, ck)
        m_ref[pl.ds(off, ck), :] = jnp.full((ck, tq), -inf, m_ref.dtype)
        return 0
    lax.fori_loop(n_c, n_ck_total, zero_body, 0)


def _index_mask(p_main, qi_col_block, wt, kia, kib, *, tq, ck, l_valid, pos_base, causal, top_k):
    b, lp, _ = kia.shape
    nq = p_main.shape[0] // (b * tq)
    qw = N_HEADS_IDX * HEAD_DIM_IDX
    return pl.pallas_call(
        functools.partial(_index_kernel, tq=tq, ck=ck, n_ck_total=lp // ck, l_valid=l_valid,
                          pos_base=pos_base, causal=causal, top_k=top_k),
        grid=(b, nq),
        in_specs=[pl.BlockSpec((tq, qw), lambda bi, i: (bi * nq + i, qi_col_block)),
                  pl.BlockSpec((N_HEADS_IDX, tq), lambda bi, i: (0, bi * nq + i)),
                  pl.BlockSpec((None, lp, LANES), lambda bi, i: (bi, 0, 0)),
                  pl.BlockSpec((None, lp, LANES), lambda bi, i: (bi, 0, 0))],
        out_specs=pl.BlockSpec((None, lp, tq), lambda bi, i: (bi, 0, i)),
        out_shape=jax.ShapeDtypeStruct((b, lp, nq * tq), BF16),
        scratch_shapes=[pltpu.VMEM((lp, tq), F32)],
        compiler_params=_cparams("parallel", "parallel"),
        name="index_mask",
    )(p_main, wt, kia, kib)


def _attn_kernel(q_ref, k_ref, vt_ref, m_ref, o_ref, acc_ref, sa_ref, sb_ref, *,
                 tq, tk, n_kt_total, causal, group, kvb):
    i = pl.program_id(1)
    heads = kvb * group
    if causal:
        n_kt = (i * tq + tq + tk - 1) // tk
    else:
        n_kt = n_kt_total
    acc_ref[...] = jnp.zeros_like(acc_ref)

    def scores(kt, s_ref):
        off = pl.multiple_of(kt * tk, tk)
        for h in range(heads):
            kv = h // group
            k = k_ref[pl.ds(off, tk), kv * HEAD_DIM_A:(kv + 1) * HEAD_DIM_A]
            s_ref[h] = _dot_nt(k, q_ref[:, h * HEAD_DIM_A:(h + 1) * HEAD_DIM_A])

    def softmax_pv(kt, s_ref, ms):
        off = pl.multiple_of(kt * tk, tk)
        bias = m_ref[pl.ds(off, tk), :].astype(F32)
        out = []
        for h in range(heads):
            kv = h // group
            vt = vt_ref[kv * VT_ROWS:(kv + 1) * VT_ROWS, pl.ds(off, tk)]
            s = s_ref[h] + bias
            m_new = jnp.maximum(ms[h], jnp.max(s, axis=0, keepdims=True))
            alpha = jnp.exp2(ms[h] - m_new)
            p = jnp.exp2(s - m_new).astype(BF16)
            acc_ref[h] = acc_ref[h] * alpha + _dot(vt, p)
            out.append(m_new)
        return tuple(out)

    def body(j, ms):
        scores(2 * j + 1, sb_ref)
        ms = softmax_pv(2 * j, sa_ref, ms)
        scores(2 * j + 2, sa_ref)
        return softmax_pv(2 * j + 1, sb_ref, ms)

    scores(0, sa_ref)
    n_pairs = (n_kt - 1) // 2
    ms = lax.fori_loop(0, n_pairs, body, tuple(jnp.full((1, tq), NEG_BIG, F32) for _ in range(heads)))
    even = n_kt - 1 > 2 * n_pairs

    @pl.when(even)
    def _():
        scores(2 * n_pairs + 1, sb_ref)

    ms = softmax_pv(2 * n_pairs, sa_ref, ms)

    @pl.when(even)
    def _():
        softmax_pv(2 * n_pairs + 1, sb_ref, ms)

    for h in range(heads):
        a = acc_ref[h]
        o = a[:HEAD_DIM_A] / a[HEAD_DIM_A:HEAD_DIM_A + 1]
        o_ref[:, h * HEAD_DIM_A:(h + 1) * HEAD_DIM_A] = o.T.astype(o_ref.dtype)


def _attention(p_main, k_bf, vt_bf, mask_t, *, tq, tk, causal, kvb):
    b, lp, _ = k_bf.shape
    nq = mask_t.shape[2] // tq
    group = N_HEADS_A // N_KV_A
    heads = kvb * group
    gw = heads * HEAD_DIM_A
    return pl.pallas_call(
        functools.partial(_attn_kernel, tq=tq, tk=tk, n_kt_total=lp // tk, causal=causal, group=group,
                          kvb=kvb),
        grid=(b, nq, N_KV_A // kvb),
        in_specs=[pl.BlockSpec((tq, gw), lambda bi, i, g: (bi * nq + i, g)),
                  pl.BlockSpec((None, lp, kvb * HEAD_DIM_A), lambda bi, i, g: (bi, 0, g)),
                  pl.BlockSpec((kvb * VT_ROWS, lp), lambda bi, i, g: (g, bi)),
                  pl.BlockSpec((None, lp, tq), lambda bi, i, g: (bi, 0, i))],
        out_specs=pl.BlockSpec((tq, gw), lambda bi, i, g: (bi * nq + i, g)),
        out_shape=jax.ShapeDtypeStruct((b * nq * tq, N_HEADS_A * HEAD_DIM_A), BF16),
        scratch_shapes=[pltpu.VMEM((heads, VT_ROWS, tq), F32),
                        pltpu.VMEM((heads, tk, tq), F32), pltpu.VMEM((heads, tk, tq), F32)],
        compiler_params=_cparams("parallel", "parallel", "arbitrary"),
        name="attention",
    )(p_main, k_bf, vt_bf, mask_t)


def _retention_tables(c):
    log_gamma = jnp.log1p(-(2.0 ** (-5.0 - jnp.arange(N_HEADS_R, dtype=F32))))
    idx = jnp.arange(c, dtype=F32)
    diff = idx[:, None] - idx[None, :]
    decay = jnp.where(diff[None] >= 0,
                      jnp.exp(jnp.maximum(diff, 0.0)[None] * log_gamma[:, None, None]), 0.0)
    cross = jnp.exp((idx + 1.0)[None, :] * log_gamma[:, None])
    kdec = jnp.exp((c - 1.0 - idx)[None, :] * log_gamma[:, None])
    full = jnp.exp(c * log_gamma)
    bc = lambda a, w: jnp.broadcast_to(a[..., None], a.shape + (w,))
    return decay, bc(cross, KEY_DIM_R), bc(kdec, KEY_DIM_R), bc(full[:, None], VAL_DIM_R)


def _retention_kernel(q_ref, k_ref, v_ref, g_ref, dec_ref, qsc_ref, ksc_ref, gc_ref, s0_ref,
                      o_ref, st_ref):
    @pl.when(pl.program_id(1) == 0)
    def _():
        st_ref[...] = s0_ref[...]

    for h in range(N_HEADS_R):
        ks = slice(h * KEY_DIM_R, (h + 1) * KEY_DIM_R)
        vs = slice(h * VAL_DIM_R, (h + 1) * VAL_DIM_R)
        q = q_ref[:, ks]
        k = k_ref[:, ks]
        v = v_ref[:, vs]
        st = st_ref[h]
        inner = _dot_nt(q, k) * dec_ref[h]
        qd = (q.astype(F32) * qsc_ref[h]).astype(BF16)
        o = _dot(inner.astype(BF16), v) + _dot(qd, st.astype(BF16))
        kdt = (k.astype(F32) * ksc_ref[h]).T.astype(BF16)
        st_ref[h] = st * gc_ref[h] + _dot(kdt, v)
        o = o * lax.rsqrt(jnp.mean(o * o, axis=-1, keepdims=True) + EPS)
        o_ref[:, vs] = (o * g_ref[:, vs].astype(F32)).astype(o_ref.dtype)


def _retention(p_main, cols, state0, c):
    b = state0.shape[0]
    nc = p_main.shape[0] // (b * c)
    kw = N_HEADS_R * KEY_DIM_R
    vw = N_HEADS_R * VAL_DIM_R
    dec, qsc, ksc, gcs = _retention_tables(c)
    qb, kb, vb, gb = cols
    const3 = lambda bi, ci: (0, 0, 0)
    st_spec = pl.BlockSpec((None, N_HEADS_R, KEY_DIM_R, VAL_DIM_R), lambda bi, ci: (bi, 0, 0, 0))
    return pl.pallas_call(
        _retention_kernel,
        grid=(b, nc),
        in_specs=[pl.BlockSpec((c, kw), lambda bi, ci: (bi * nc + ci, qb)),
                  pl.BlockSpec((c, kw), lambda bi, ci: (bi * nc + ci, kb)),
                  pl.BlockSpec((c, vw), lambda bi, ci: (bi * nc + ci, vb)),
                  pl.BlockSpec((c, vw), lambda bi, ci: (bi * nc + ci, gb)),
                  pl.BlockSpec(dec.shape, const3), pl.BlockSpec(qsc.shape, const3),
                  pl.BlockSpec(ksc.shape, const3), pl.BlockSpec(gcs.shape, const3),
                  st_spec],
        out_specs=[pl.BlockSpec((c, vw), lambda bi, ci: (bi * nc + ci, 0)), st_spec],
        out_shape=[jax.ShapeDtypeStruct((b * nc * c, vw), BF16),
                   jax.ShapeDtypeStruct(state0.shape, F32)],
        compiler_params=_cparams("parallel", "arbitrary"),
        name="retention",
    )(p_main, p_main, p_main, p_main, dec, qsc, ksc, gcs, state0)


def _merge_kernel(oa_ref, ob_ref, wa_ref, wb_ref, sa_ref, sb_ref, o_ref):
    for b in range(o_ref.shape[1] // MXU_COLS):
        sl = slice(b * MXU_COLS, (b + 1) * MXU_COLS)
        m = (_dot(oa_ref[...], wa_ref[:, sl]) * sa_ref[:, sl].astype(F32)
             + _dot(ob_ref[...], wb_ref[:, sl]) * sb_ref[:, sl].astype(F32))
        o_ref[:, sl] = m.astype(o_ref.dtype)


def _merge(o_a, o_b, w_pa, w_pb, p_main, ga_off, gb_off, tm, tn):
    n, wa = o_a.shape
    wb = o_b.shape[1]
    d = w_pa.shape[1]
    return pl.pallas_call(
        _merge_kernel,
        grid=(n // tm, d // tn),
        in_specs=[pl.BlockSpec((tm, wa), lambda i, j: (i, 0)),
                  pl.BlockSpec((tm, wb), lambda i, j: (i, 0)),
                  pl.BlockSpec((wa, tn), lambda i, j: (0, j)),
                  pl.BlockSpec((wb, tn), lambda i, j: (0, j)),
                  pl.BlockSpec((tm, tn), lambda i, j: (i, ga_off // tn + j)),
                  pl.BlockSpec((tm, tn), lambda i, j: (i, gb_off // tn + j))],
        out_specs=pl.BlockSpec((tm, tn), lambda i, j: (i, j)),
        out_shape=jax.ShapeDtypeStruct((n, d), BF16),
        compiler_params=_cparams("parallel", "arbitrary"),
        name="merge",
    )(o_a, o_b, w_pa, w_pb, p_main, p_main)


def _out_proj_kernel(x_ref, m_ref, w_ref, x1_ref):
    x1_ref[...] = x_ref[...] + _dot(m_ref[...], w_ref[...])


def _out_proj(x, merged, w_o, tm):
    n, d = x.shape
    row = lambda i: (i, 0)
    return pl.pallas_call(
        _out_proj_kernel,
        grid=(n // tm,),
        in_specs=[pl.BlockSpec((tm, d), row), pl.BlockSpec((tm, d), row),
                  pl.BlockSpec((d, d), lambda i: (0, 0))],
        out_specs=pl.BlockSpec((tm, d), row),
        out_shape=jax.ShapeDtypeStruct((n, d), F32),
        compiler_params=_cparams("parallel"),
        name="out_proj",
    )(x, merged, w_o)


def _ffn_kernel(x1_ref, g2_ref, wg_ref, wu_ref, wd_ref, g_ref, y_ref, h_ref, *, final_norm):
    c = pl.program_id(1)

    @pl.when(c == 0)
    def _():
        x1 = x1_ref[...]
        y_ref[...] = x1
        h_ref[...] = _rmsnorm_rows(x1, g2_ref[...]).astype(h_ref.dtype)

    acts = []
    for b in range(wg_ref.shape[1] // MXU_COLS):
        sl = slice(b * MXU_COLS, (b + 1) * MXU_COLS)
        a = _dot(h_ref[...], wg_ref[:, sl])
        u = _dot(h_ref[...], wu_ref[:, sl])
        acts.append((a / (1.0 + jnp.exp(-a)) * u).astype(BF16))
    upd = _dot(acts[0], wd_ref[0:MXU_COLS, :])
    for b in range(1, len(acts)):
        upd += _dot(acts[b], wd_ref[b * MXU_COLS:(b + 1) * MXU_COLS, :])
    y_ref[...] += upd

    if final_norm:
        @pl.when(c == pl.num_programs(1) - 1)
        def _():
            y_ref[...] = _rmsnorm_rows(y_ref[...], g_ref[...])


def _ffn(x1, g2, wg, wu, wd, gf, tm, tc, final_norm):
    n, d = x1.shape
    f = wg.shape[1]
    row = lambda i, c: (i, 0)
    vec = pl.BlockSpec((1, d), lambda i, c: (0, 0))
    return pl.pallas_call(
        functools.partial(_ffn_kernel, final_norm=final_norm),
        grid=(n // tm, f // tc),
        in_specs=[pl.BlockSpec((tm, d), row, pipeline_mode=pl.Buffered(1)), vec,
                  pl.BlockSpec((d, tc), lambda i, c: (0, c)),
                  pl.BlockSpec((d, tc), lambda i, c: (0, c)),
                  pl.BlockSpec((tc, d), lambda i, c: (c, 0)), vec],
        out_specs=pl.BlockSpec((tm, d), row),
        out_shape=jax.ShapeDtypeStruct((n, d), F32),
        scratch_shapes=[pltpu.VMEM((tm, d), BF16)],
        compiler_params=_cparams("parallel", "arbitrary"),
        name="ffn",
    )(x1, g2.reshape(1, d), wg, wu, wd, gf.reshape(1, d))


class _MainLayout:
    def __init__(self, d_model):
        wa = N_HEADS_A * HEAD_DIM_A
        wb = N_HEADS_R * VAL_DIM_R
        kr = N_HEADS_R * KEY_DIM_R
        qi = N_HEADS_IDX * HEAD_DIM_IDX
        order = [("qa", wa, "rope128", HEAD_DIM_A ** -0.5 * LOG2E), ("vr", wb, "plain", 1.0),
                 ("gr", wb, "silu", 1.0), ("ga", d_model, "sigmoid", 1.0), ("gb", d_model, "sigmoid", 1.0),
                 ("qi", qi, "rope64", 1.0), ("qr", kr, "rope128", 1.0),
                 ("kr", kr, "rope128", KEY_DIM_R ** -0.5)]
        self.off, self.width = {}, {}
        self.order = order
        o = 0
        for name, w, _, _ in order:
            self.off[name], self.width[name] = o, w
            o += w
        self.total = o

    def groups(self, tn):
        out = []
        for name, w, mode, scale in self.order:
            assert self.off[name] % tn == 0 and w % tn == 0
            out.append((self.off[name] // tn, (self.off[name] + w) // tn, mode, scale))
        return tuple(out)

    def block(self, name, width=None):
        width = width or self.width[name]
        assert self.off[name] % width == 0
        return self.off[name] // width


def _split_w_in(w, d_model):
    wa = N_HEADS_A * HEAD_DIM_A
    nk = N_KV_A * HEAD_DIM_A
    wb = N_HEADS_R * VAL_DIM_R
    kr = N_HEADS_R * KEY_DIM_R
    names = ("qa", "ka", "va", "qi", "ki", "wi", "qr", "kr", "vr", "gr", "ga", "gb")
    widths = (wa, nk, nk, N_HEADS_IDX * HEAD_DIM_IDX, HEAD_DIM_IDX, N_HEADS_IDX, kr, kr, wb, wb,
              d_model, d_model)
    parts, o = {}, 0
    for name, wd in zip(names, widths):
        parts[name] = w[:, o:o + wd]
        o += wd
    assert o == w.shape[1]
    return parts


def _tile(n, pref):
    t = min(n, pref)
    assert n % t == 0
    return t


def _dense_pre(x2, pos_rows, lw, lay):
    n = x2.shape[0]
    rope = _rope_tables(pos_rows)
    h, *kv = _proj_kv(x2, lw["g1"], lw["w_kv"], rope, _tile(n, 512))
    tn = 1024
    p_main = _proj_main(h, lw["w_main"], rope, lay.groups(tn), _tile(n, 1024), tn)
    return p_main, kv


def _dense_post(x2, o_a, o_b, p_main, lw, lay, gf, final_norm):
    n = x2.shape[0]
    tm = _tile(n, 1024)
    merged = _merge(o_a, o_b, lw["w_pa"], lw["w_pb"], p_main, lay.off["ga"], lay.off["gb"], tm, 512)
    x1 = _out_proj(x2, merged, lw["w_o"], _tile(n, 512))
    return _ffn(x1, lw["g2"], lw["wg"], lw["wu"], lw["wd"], gf, tm, 512, final_norm)


def _prompt_layer(xp, lw, lay, gf, final_norm):
    b, t, d = xp.shape
    x2 = xp.reshape(b * t, d)
    p_main, (ka, va, ki, wt, ka_bf, vt_bf, kia, kib) = _dense_pre(x2, jnp.arange(t, dtype=jnp.int32), lw, lay)
    top_k = min(TOPK_MAX, t // 4)
    tq, ck = _tile(t, 256), _tile(t, 512)
    mask_t = _index_mask(p_main, lay.block("qi"), wt, kia.reshape(b, t, LANES), kib.reshape(b, t, LANES),
                         tq=tq, ck=ck, l_valid=t, pos_base=0, causal=True, top_k=top_k)
    nk = N_KV_A * HEAD_DIM_A
    o_a = _attention(p_main, ka_bf.reshape(b, t, nk), vt_bf, mask_t, tq=tq, tk=ck, causal=True, kvb=2)
    state0 = jnp.zeros((b, N_HEADS_R, KEY_DIM_R, VAL_DIM_R), F32)
    cols = (lay.block("qr"), lay.block("kr"), lay.block("vr"), lay.block("gr"))
    o_b, st = _retention(p_main, cols, state0, _tile(t, 256))
    y = _dense_post(x2, o_a, o_b, p_main, lw, lay, gf, final_norm)
    return (y.reshape(b, t, d), ka.reshape(b, t, N_KV_A, HEAD_DIM_A), va.reshape(b, t, N_KV_A, HEAD_DIM_A),
            ki.reshape(b, t, HEAD_DIM_IDX), st)


def _sample_layer(xs, cache_k, cache_v, cache_i, state, lw, lay, gf, final_norm):
    b, t, d = xs.shape
    past = cache_k.shape[1]
    nk = N_KV_A * HEAD_DIM_A
    x2 = xs.reshape(b * t, d)
    pos = jnp.tile(past + jnp.arange(t, dtype=jnp.int32), b)
    p_main, (ka, va, ki, wt, ka_bf, vt_bf, kia, kib) = _dense_pre(x2, pos, lw, lay)

    tq = LANES
    l_valid = past + t
    tk = 3 * LANES
    lp = -(-l_valid // tk) * tk
    pq = jnp.pad(p_main.reshape(b, t, -1), ((0, 0), (0, tq - t), (0, 0)), mode="edge").reshape(b * tq, -1)
    wtq = jnp.pad(wt.reshape(N_HEADS_IDX, b, t), ((0, 0), (0, 0), (0, tq - t)),
                  mode="edge").reshape(N_HEADS_IDX, b * tq)
    kpad = ((0, 0), (0, lp - l_valid), (0, 0))
    ci = cache_i.astype(BF16)
    zi = jnp.zeros_like(ci)
    kia_all = jnp.pad(jnp.concatenate([jnp.concatenate([ci, zi], -1), kia.reshape(b, t, LANES)], 1), kpad)
    kib_all = jnp.pad(jnp.concatenate([jnp.concatenate([zi, ci], -1), kib.reshape(b, t, LANES)], 1), kpad)
    k_all = jnp.pad(jnp.concatenate([cache_k.reshape(b, past, nk).astype(BF16), ka_bf.reshape(b, t, nk)], 1), kpad)
    vt_new = vt_bf.reshape(N_KV_A, VT_ROWS, b, t)
    vt_cache = cache_v.reshape(b, past, N_KV_A, HEAD_DIM_A).transpose(2, 3, 0, 1).astype(BF16)
    ones_rows = jnp.zeros((N_KV_A, VT_ROWS - HEAD_DIM_A, b, past), BF16).at[:, 0].set(1.0)
    vt_all = jnp.concatenate([jnp.concatenate([vt_cache, ones_rows], 1), vt_new], 3)
    vt_all = jnp.pad(vt_all, ((0, 0), (0, 0), (0, 0), (0, lp - l_valid))).reshape(N_KV_A * VT_ROWS, b * lp)

    top_k = min(TOPK_MAX, l_valid // 4)
    mask_t = _index_mask(pq, lay.block("qi"), wtq, kia_all, kib_all, tq=tq, ck=tk, l_valid=l_valid,
                         pos_base=past, causal=False, top_k=top_k)
    o_a = _attention(pq, k_all, vt_all, mask_t, tq=tq, tk=tk, causal=False, kvb=1)
    o_a = o_a.reshape(b, tq, -1)[:, :t].reshape(b * t, -1)
    cols = (lay.block("qr"), lay.block("kr"), lay.block("vr"), lay.block("gr"))
    o_b, st = _retention(p_main, cols, state.astype(F32), t)
    y = _dense_post(x2, o_a, o_b, p_main, lw, lay, gf, final_norm)
    return (y.reshape(b, t, d), ka.reshape(b, t, N_KV_A, HEAD_DIM_A), va.reshape(b, t, N_KV_A, HEAD_DIM_A),
            ki.reshape(b, t, HEAD_DIM_IDX), st)


def kernel(x_prompt, x_sample, cache_k, cache_v, cache_idx_k, state_ret, norm1_g, w_in, w_pa, w_pb, w_o,
           norm2_g, w_ffn_gate, w_ffn_up, w_ffn_down, norm_f_g):
    assert HEAD_DIM_A == LANES and HEAD_DIM_IDX * 2 == LANES and N_HEADS_IDX % 2 == 0
    depth, d_model = norm1_g.shape
    lay = _MainLayout(d_model)
    xp, xs = x_prompt, x_sample
    outs = [[] for _ in range(8)]
    for l in range(depth):
        parts = _split_w_in(w_in[l], d_model)
        pad = jnp.zeros((d_model, LANES - HEAD_DIM_IDX - N_HEADS_IDX), F32)
        lw = {
            "g1": norm1_g[l], "g2": norm2_g[l],
            "w_main": jnp.concatenate([parts[name] for name, _, _, _ in lay.order], axis=1).astype(BF16),
            "w_kv": jnp.concatenate([parts["ka"], parts["va"], parts["ki"], parts["wi"], pad], axis=1).astype(BF16),
            "w_pa": w_pa[l].astype(BF16), "w_pb": w_pb[l].astype(BF16), "w_o": w_o[l].astype(BF16),
            "wg": w_ffn_gate[l].astype(BF16), "wu": w_ffn_up[l].astype(BF16), "wd": w_ffn_down[l].astype(BF16),
        }
        last = l == depth - 1
        xp, kp, vp, ip, sp = _prompt_layer(xp, lw, lay, norm_f_g, last)
        xs, ks, vs, isl, ss = _sample_layer(xs, cache_k[l], cache_v[l], cache_idx_k[l], state_ret[l],
                                            lw, lay, norm_f_g, last)
        for lst, val in zip(outs, (kp, vp, ip, sp, ks, vs, isl, ss)):
            lst.append(val)
    stacked = [jnp.stack(o) for o in outs]
    return (xp, xs, stacked[0], stacked[1], stacked[2], stacked[3].astype(x_prompt.dtype),
            stacked[4], stacked[5], stacked[6], stacked[7].astype(state_ret.dtype))
```

```python
import functools
import math

import jax
import jax.numpy as jnp
from jax import lax
from jax.experimental import pallas as pl
from jax.experimental.pallas import tpu as pltpu

F32 = jnp.float32
BF16 = jnp.bfloat16

CHUNK = 64
EPS = 1e-6
ROPE_THETA = 10000.0
N_HEADS_A = 16
N_KV_A = 4
HEAD_DIM_A = 128
N_HEADS_IDX = 16
HEAD_DIM_IDX = 64
TOPK_MAX = 256
N_HEADS_R = 8
KEY_DIM_R = 128
VAL_DIM_R = 256

LANES = 128
MXU_COLS = 256
V7X_VMEM_BYTES = 64 * 1024 * 1024
VMEM_LIMIT = V7X_VMEM_BYTES - 8 * 1024 * 1024
BF16_SUBLANES = 16
VT_ROWS = HEAD_DIM_A + BF16_SUBLANES
LOG2E = math.log2(math.e)
MAX_BISECT_ITERS = 300
NEG_BIG = float(jnp.finfo(jnp.float32).min)


def _cparams(*sem):
    return pltpu.CompilerParams(dimension_semantics=sem, vmem_limit_bytes=VMEM_LIMIT)


def _dot(a, b):
    return jnp.dot(a, b, preferred_element_type=F32)


def _dot_nt(a, b):
    return lax.dot_general(a, b, (((1,), (1,)), ((), ())), preferred_element_type=F32)


def _rmsnorm_rows(x, g):
    return x * lax.rsqrt(jnp.mean(x * x, axis=-1, keepdims=True) + EPS) * g


def _rope_tables(pos):
    posf = pos.astype(F32)[:, None]

    def cs(half):
        inv_freq = ROPE_THETA ** (-jnp.arange(half, dtype=F32) / half)
        ang = posf * inv_freq[None, :]
        return jnp.cos(ang), jnp.sin(ang)

    c, s = cs(HEAD_DIM_A // 2)
    c128 = jnp.concatenate([c, c], axis=1)
    s128 = jnp.concatenate([-s, s], axis=1)
    c, s = cs(HEAD_DIM_IDX // 2)
    z = jnp.zeros_like(s)
    c64 = jnp.tile(c, (1, 4))
    a64 = jnp.tile(jnp.concatenate([-s, z], axis=1), (1, 2))
    b64 = jnp.tile(jnp.concatenate([z, s], axis=1), (1, 2))
    return jnp.stack([c128, s128, c64, a64, b64])


def _rope128(x, rope_ref):
    return x * rope_ref[0] + pltpu.roll(x, 64, 1) * rope_ref[1]


def _rope64(x, rope_ref):
    return (x * rope_ref[2] + pltpu.roll(x, 96, 1) * rope_ref[3]
            + pltpu.roll(x, 32, 1) * rope_ref[4])


def _proj_main_kernel(h_ref, w_ref, rope_ref, o_ref, *, groups, tn):
    j = pl.program_id(1)
    for lo, hi, mode, scale in groups:
        @pl.when(jnp.logical_and(j >= lo, j < hi))
        def _(mode=mode, scale=scale):
            for b in range(tn // MXU_COLS):
                acc = _dot(h_ref[...], w_ref[:, b * MXU_COLS:(b + 1) * MXU_COLS])
                for c in range(MXU_COLS // LANES):
                    x = acc[:, c * LANES:(c + 1) * LANES]
                    if mode == "rope128":
                        y = _rope128(x, rope_ref)
                    elif mode == "rope64":
                        y = _rope64(x, rope_ref)
                    elif mode == "sigmoid":
                        y = 1.0 / (1.0 + jnp.exp(-x))
                    elif mode == "silu":
                        y = x / (1.0 + jnp.exp(-x))
                    else:
                        y = x
                    if scale != 1.0:
                        y = y * scale
                    col = b * MXU_COLS + c * LANES
                    o_ref[:, col:col + LANES] = y.astype(o_ref.dtype)


def _proj_main(h, w_main, rope, groups, tm, tn):
    n, d = h.shape
    p = w_main.shape[1]
    n_pos_tiles = rope.shape[1] // tm
    return pl.pallas_call(
        functools.partial(_proj_main_kernel, groups=groups, tn=tn),
        grid=(n // tm, p // tn),
        in_specs=[pl.BlockSpec((tm, d), lambda i, j: (i, 0)),
                  pl.BlockSpec((d, tn), lambda i, j: (0, j)),
                  pl.BlockSpec((5, tm, LANES), lambda i, j: (0, i % n_pos_tiles, 0))],
        out_specs=pl.BlockSpec((tm, tn), lambda i, j: (i, j)),
        out_shape=jax.ShapeDtypeStruct((n, p), BF16),
        compiler_params=_cparams("parallel", "arbitrary"),
        name="proj_main",
    )(h, w_main, rope)


def _proj_kv_kernel(x_ref, g_ref, w_ref, rope_ref, h_ref, ka_ref, va_ref, ki_ref, wt_ref,
                    kab_ref, vtb_ref, kia_ref, kib_ref, acc_ref, *, nk, idx_scale):
    h_ref[...] = _rmsnorm_rows(x_ref[...], g_ref[...]).astype(h_ref.dtype)
    acc_ref[...] = _dot(h_ref[...], w_ref[...])
    tm = acc_ref.shape[0]
    for c in range(N_KV_A):
        sl = slice(c * LANES, (c + 1) * LANES)
        y = _rope128(acc_ref[:, sl], rope_ref)
        ka_ref[pl.ds(c, tm, stride=N_KV_A), :] = y
        kab_ref[:, sl] = y.astype(BF16)
        va_ref[pl.ds(c, tm, stride=N_KV_A), :] = acc_ref[:, nk + c * LANES:nk + (c + 1) * LANES]
    ones_rows = jnp.where(lax.broadcasted_iota(jnp.int32, (VT_ROWS - HEAD_DIM_A, tm), 0) == 0, 1.0, 0.0)
    for g in range(N_KV_A):
        vg = acc_ref[:, nk + g * HEAD_DIM_A:nk + (g + 1) * HEAD_DIM_A]
        vtb_ref[g * VT_ROWS:g * VT_ROWS + HEAD_DIM_A, :] = vg.T.astype(BF16)
        vtb_ref[g * VT_ROWS + HEAD_DIM_A:(g + 1) * VT_ROWS, :] = ones_rows.astype(BF16)
    z = acc_ref[:, 2 * nk:2 * nk + LANES]
    y = _rope64(z, rope_ref)
    ki_ref[...] = y[:, :HEAD_DIM_IDX]
    lane = lax.broadcasted_iota(jnp.int32, y.shape, 1)
    ya = jnp.where(lane < HEAD_DIM_IDX, y, 0.0)
    kia_ref[...] = ya.astype(BF16)
    kib_ref[...] = pltpu.roll(ya, HEAD_DIM_IDX, 1).astype(BF16)
    wt_ref[...] = (z * idx_scale).T[HEAD_DIM_IDX:HEAD_DIM_IDX + N_HEADS_IDX, :]


def _proj_kv(x, g1, w_kv, rope, tm):
    n, d = x.shape
    nk = N_KV_A * HEAD_DIM_A
    pw = w_kv.shape[1]
    n_pos_tiles = rope.shape[1] // tm
    idx_scale = (HEAD_DIM_IDX ** -0.5) * (N_HEADS_IDX ** -0.5)
    row = lambda i: (i, 0)
    return pl.pallas_call(
        functools.partial(_proj_kv_kernel, nk=nk, idx_scale=idx_scale),
        grid=(n // tm,),
        in_specs=[pl.BlockSpec((tm, d), row),
                  pl.BlockSpec((1, d), lambda i: (0, 0)),
                  pl.BlockSpec((d, pw), lambda i: (0, 0)),
                  pl.BlockSpec((5, tm, LANES), lambda i: (0, i % n_pos_tiles, 0))],
        out_specs=[pl.BlockSpec((tm, d), row),
                   pl.BlockSpec((tm * N_KV_A, HEAD_DIM_A), row), pl.BlockSpec((tm * N_KV_A, HEAD_DIM_A), row),
                   pl.BlockSpec((tm, HEAD_DIM_IDX), row),
                   pl.BlockSpec((N_HEADS_IDX, tm), lambda i: (0, i)),
                   pl.BlockSpec((tm, nk), row),
                   pl.BlockSpec((N_KV_A * VT_ROWS, tm), lambda i: (0, i)),
                   pl.BlockSpec((tm, LANES), row), pl.BlockSpec((tm, LANES), row)],
        out_shape=[jax.ShapeDtypeStruct((n, d), BF16),
                   jax.ShapeDtypeStruct((n * N_KV_A, HEAD_DIM_A), F32),
                   jax.ShapeDtypeStruct((n * N_KV_A, HEAD_DIM_A), F32),
                   jax.ShapeDtypeStruct((n, HEAD_DIM_IDX), F32),
                   jax.ShapeDtypeStruct((N_HEADS_IDX, n), F32),
                   jax.ShapeDtypeStruct((n, nk), BF16),
                   jax.ShapeDtypeStruct((N_KV_A * VT_ROWS, n), BF16),
                   jax.ShapeDtypeStruct((n, LANES), BF16), jax.ShapeDtypeStruct((n, LANES), BF16)],
        scratch_shapes=[pltpu.VMEM((tm, pw), F32)],
        compiler_params=_cparams("parallel"),
        name="proj_kv",
    )(x, g1.reshape(1, d), w_kv, rope)


def _index_kernel(qi_ref, wt_ref, kia_ref, kib_ref, m_ref, s_ref, *,
                  tq, ck, n_ck_total, l_valid, pos_base, causal, top_k):
    i = pl.program_id(1)
    if causal:
        n_c = (i * tq + tq + ck - 1) // ck
    else:
        n_c = n_ck_total
    qpos = pos_base + i * tq + lax.broadcasted_iota(jnp.int32, (1, tq), 1)
    lim = jnp.minimum(qpos - lax.rem(qpos, CHUNK) + CHUNK, l_valid)
    limf = lim.astype(F32)
    kprime = jnp.minimum(float(top_k), limf)
    inf = jnp.float32(jnp.inf)

    def key_index(off):
        return off + lax.broadcasted_iota(jnp.int32, (ck, tq), 0)

    def score_chunk(c, carry):
        rmax, rmin = carry
        off = pl.multiple_of(c * ck, ck)
        ka = kia_ref[pl.ds(off, ck), :]
        kb = kib_ref[pl.ds(off, ck), :]
        acc = jnp.zeros((ck, tq), F32)
        for p in range(N_HEADS_IDX // 2):
            qp = qi_ref[:, p * LANES:(p + 1) * LANES]
            acc += jnp.maximum(_dot_nt(ka, qp), 0.0) * wt_ref[2 * p:2 * p + 1, :]
            acc += jnp.maximum(_dot_nt(kb, qp), 0.0) * wt_ref[2 * p + 1:2 * p + 2, :]
        adm = key_index(off) < lim
        s_ref[pl.ds(off, ck), :] = jnp.where(adm, acc, -inf)
        rmax = jnp.maximum(rmax, jnp.max(jnp.where(adm, acc, -inf), axis=0, keepdims=True))
        rmin = jnp.minimum(rmin, jnp.min(jnp.where(adm, acc, inf), axis=0, keepdims=True))
        return rmax, rmin

    rmax, rmin = lax.fori_loop(0, n_c, score_chunk,
                               (jnp.full((1, tq), -inf, F32), jnp.full((1, tq), inf, F32)))

    fold_rows = min(ck, 64)

    def fold(ind):
        return ind.reshape(ck // fold_rows, fold_rows, tq).sum(axis=0)

    def count_ge(th):
        def body(c, acc):
            off = pl.multiple_of(c * ck, ck)
            return acc + fold(jnp.where(s_ref[pl.ds(off, ck), :] >= th, 1.0, 0.0))
        acc = lax.fori_loop(0, n_c, body, jnp.zeros((fold_rows, tq), F32))
        return acc.sum(axis=0, keepdims=True)

    def cond(st):
        it, lo, hi, cnt, stuck = st
        active = jnp.where(jnp.logical_and(cnt > kprime, stuck < 0.5), 1.0, 0.0)
        return jnp.logical_and(it < MAX_BISECT_ITERS, jnp.max(active) > 0.5)

    def body(st):
        it, lo, hi, cnt, stuck = st
        open_top = hi == inf
        mid = lo + 0.5 * (jnp.where(open_top, rmax, hi) - lo)
        mid = jnp.where(jnp.logical_and(open_top, mid <= lo), rmax, mid)
        c = count_ge(mid)
        ge = c >= kprime
        no_progress = jnp.logical_or(mid <= lo, mid >= hi)
        return (it + 1, jnp.where(ge, mid, lo), jnp.where(ge, hi, mid), jnp.where(ge, c, cnt),
                jnp.where(no_progress, 1.0, stuck))

    _, lo, hi, cnt, _ = lax.while_loop(
        cond, body, (jnp.int32(0), rmin, jnp.full((1, tq), inf, F32), limf, jnp.zeros((1, tq), F32)))

    unresolved = jnp.max(jnp.where(cnt > kprime, 1.0, 0.0)) > 0.5

    @pl.when(jnp.logical_not(unresolved))
    def _():
        def body(c, _):
            off = pl.multiple_of(c * ck, ck)
            m_ref[pl.ds(off, ck), :] = jnp.where(s_ref[pl.ds(off, ck), :] >= lo, 0.0, -inf).astype(m_ref.dtype)
            return 0
        lax.fori_loop(0, n_c, body, 0)

    @pl.when(unresolved)
    def _():
        need = kprime - count_ge(hi)

        def in_tie(blk):
            return jnp.logical_and(blk >= lo, blk < hi)

        def count_tie_below(jcut):
            def body(c, acc):
                off = pl.multiple_of(c * ck, ck)
                blk = s_ref[pl.ds(off, ck), :]
                e = jnp.logical_and(in_tie(blk), key_index(off).astype(F32) < jcut)
                return acc + fold(jnp.where(e, 1.0, 0.0))
            acc = lax.fori_loop(0, n_c, body, jnp.zeros((fold_rows, tq), F32))
            return acc.sum(axis=0, keepdims=True)

        def jbody(_, st):
            jlo, jhi = st
            mid = jnp.floor(0.5 * (jlo + jhi))
            ok = count_tie_below(mid) >= need
            return jnp.where(ok, jlo, mid + 1.0), jnp.where(ok, mid, jhi)

        n_total = n_ck_total * ck
        _, jcut = lax.fori_loop(0, int(math.ceil(math.log2(n_total + 1))), jbody,
                                (jnp.zeros((1, tq), F32), jnp.full((1, tq), float(n_total), F32)))

        def body(c, _):
            off = pl.multiple_of(c * ck, ck)
            blk = s_ref[pl.ds(off, ck), :]
            e = jnp.logical_and(in_tie(blk), key_index(off).astype(F32) < jcut)
            keep = jnp.logical_or(blk >= hi, e)
            m_ref[pl.ds(off, ck), :] = jnp.where(keep, 0.0, -inf).astype(m_ref.dtype)
            return 0
        lax.fori_loop(0, n_c, body, 0)

    def zero_body(c, _):
        off = pl.multiple_of(c * ck, ck)
        m_ref[pl.ds(off, ck), :] = jnp.full((ck, tq), -inf, m_ref.dtype)
        return 0
    lax.fori_loop(n_c, n_ck_total, zero_body, 0)


def _index_mask(p_main, qi_col_block, wt, kia, kib, *, tq, ck, l_valid, pos_base, causal, top_k):
    b, lp, _ = kia.shape
    nq = p_main.shape[0] // (b * tq)
    qw = N_HEADS_IDX * HEAD_DIM_IDX
    return pl.pallas_call(
        functools.partial(_index_kernel, tq=tq, ck=ck, n_ck_total=lp // ck, l_valid=l_valid,
                          pos_base=pos_base, causal=causal, top_k=top_k),
        grid=(b, nq),
        in_specs=[pl.BlockSpec((tq, qw), lambda bi, i: (bi * nq + i, qi_col_block)),
                  pl.BlockSpec((N_HEADS_IDX, tq), lambda bi, i: (0, bi * nq + i)),
                  pl.BlockSpec((None, lp, LANES), lambda bi, i: (bi, 0, 0)),
                  pl.BlockSpec((None, lp, LANES), lambda bi, i: (bi, 0, 0))],
        out_specs=pl.BlockSpec((None, lp, tq), lambda bi, i: (bi, 0, i)),
        out_shape=jax.ShapeDtypeStruct((b, lp, nq * tq), BF16),
        scratch_shapes=[pltpu.VMEM((lp, tq), F32)],
        compiler_params=_cparams("parallel", "parallel"),
        name="index_mask",
    )(p_main, wt, kia, kib)


def _attn_kernel(q_ref, k_ref, vt_ref, m_ref, o_ref, acc_ref, sa_ref, sb_ref, *,
                 tq, tk, n_kt_total, causal, group, kvb):
    i = pl.program_id(1)
    heads = kvb * group
    if causal:
        n_kt = (i * tq + tq + tk - 1) // tk
    else:
        n_kt = n_kt_total
    acc_ref[...] = jnp.zeros_like(acc_ref)

    def scores(kt, s_ref):
        off = pl.multiple_of(kt * tk, tk)
        for h in range(heads):
            kv = h // group
            k = k_ref[pl.ds(off, tk), kv * HEAD_DIM_A:(kv + 1) * HEAD_DIM_A]
            s_ref[h] = _dot_nt(k, q_ref[:, h * HEAD_DIM_A:(h + 1) * HEAD_DIM_A])

    def softmax_pv(kt, s_ref, ms):
        off = pl.multiple_of(kt * tk, tk)
        bias = m_ref[pl.ds(off, tk), :].astype(F32)
        out = []
        for h in range(heads):
            kv = h // group
            vt = vt_ref[kv * VT_ROWS:(kv + 1) * VT_ROWS, pl.ds(off, tk)]
            s = s_ref[h] + bias
            m_new = jnp.maximum(ms[h], jnp.max(s, axis=0, keepdims=True))
            alpha = jnp.exp2(ms[h] - m_new)
            p = jnp.exp2(s - m_new).astype(BF16)
            acc_ref[h] = acc_ref[h] * alpha + _dot(vt, p)
            out.append(m_new)
        return tuple(out)

    def body(j, ms):
        scores(2 * j + 1, sb_ref)
        ms = softmax_pv(2 * j, sa_ref, ms)
        scores(2 * j + 2, sa_ref)
        return softmax_pv(2 * j + 1, sb_ref, ms)

    scores(0, sa_ref)
    n_pairs = (n_kt - 1) // 2
    ms = lax.fori_loop(0, n_pairs, body, tuple(jnp.full((1, tq), NEG_BIG, F32) for _ in range(heads)))
    even = n_kt - 1 > 2 * n_pairs

    @pl.when(even)
    def _():
        scores(2 * n_pairs + 1, sb_ref)
        softmax_pv(2 * n_pairs + 1, sb_ref, softmax_pv(2 * n_pairs, sa_ref, ms))

    @pl.when(jnp.logical_not(even))
    def _():
        softmax_pv(2 * n_pairs, sa_ref, ms)

    for h in range(heads):
        a = acc_ref[h]
        o = a[:HEAD_DIM_A] / a[HEAD_DIM_A:HEAD_DIM_A + 1]
        o_ref[:, h * HEAD_DIM_A:(h + 1) * HEAD_DIM_A] = o.T.astype(o_ref.dtype)


def _attention(p_main, k_bf, vt_bf, mask_t, *, tq, tk, causal, kvb):
    b, lp, _ = k_bf.shape
    nq = mask_t.shape[2] // tq
    group = N_HEADS_A // N_KV_A
    heads = kvb * group
    gw = heads * HEAD_DIM_A
    return pl.pallas_call(
        functools.partial(_attn_kernel, tq=tq, tk=tk, n_kt_total=lp // tk, causal=causal, group=group,
                          kvb=kvb),
        grid=(b, nq, N_KV_A // kvb),
        in_specs=[pl.BlockSpec((tq, gw), lambda bi, i, g: (bi * nq + i, g)),
                  pl.BlockSpec((None, lp, kvb * HEAD_DIM_A), lambda bi, i, g: (bi, 0, g)),
                  pl.BlockSpec((kvb * VT_ROWS, lp), lambda bi, i, g: (g, bi)),
                  pl.BlockSpec((None, lp, tq), lambda bi, i, g: (bi, 0, i))],
        out_specs=pl.BlockSpec((tq, gw), lambda bi, i, g: (bi * nq + i, g)),
        out_shape=jax.ShapeDtypeStruct((b * nq * tq, N_HEADS_A * HEAD_DIM_A), BF16),
        scratch_shapes=[pltpu.VMEM((heads, VT_ROWS, tq), F32),
                        pltpu.VMEM((heads, tk, tq), F32), pltpu.VMEM((heads, tk, tq), F32)],
        compiler_params=_cparams("parallel", "parallel", "arbitrary"),
        name="attention",
    )(p_main, k_bf, vt_bf, mask_t)


def _retention_tables(c):
    log_gamma = jnp.log1p(-(2.0 ** (-5.0 - jnp.arange(N_HEADS_R, dtype=F32))))
    idx = jnp.arange(c, dtype=F32)
    diff = idx[:, None] - idx[None, :]
    decay = jnp.where(diff[None] >= 0,
                      jnp.exp(jnp.maximum(diff, 0.0)[None] * log_gamma[:, None, None]), 0.0)
    cross = jnp.exp((idx + 1.0)[None, :] * log_gamma[:, None])
    kdec = jnp.exp((c - 1.0 - idx)[None, :] * log_gamma[:, None])
    full = jnp.exp(c * log_gamma)
    bc = lambda a, w: jnp.broadcast_to(a[..., None], a.shape + (w,))
    return decay, bc(cross, KEY_DIM_R), bc(kdec, KEY_DIM_R), bc(full[:, None], VAL_DIM_R)


def _retention_kernel(q_ref, k_ref, v_ref, g_ref, dec_ref, qsc_ref, ksc_ref, gc_ref, s0_ref,
                      o_ref, st_ref):
    @pl.when(pl.program_id(1) == 0)
    def _():
        st_ref[...] = s0_ref[...]

    for h in range(N_HEADS_R):
        ks = slice(h * KEY_DIM_R, (h + 1) * KEY_DIM_R)
        vs = slice(h * VAL_DIM_R, (h + 1) * VAL_DIM_R)
        q = q_ref[:, ks]
        k = k_ref[:, ks]
        v = v_ref[:, vs]
        st = st_ref[h]
        inner = _dot_nt(q, k) * dec_ref[h]
        qd = (q.astype(F32) * qsc_ref[h]).astype(BF16)
        o = _dot(inner.astype(BF16), v) + _dot(qd, st.astype(BF16))
        kdt = (k.astype(F32) * ksc_ref[h]).T.astype(BF16)
        st_ref[h] = st * gc_ref[h] + _dot(kdt, v)
        o = o * lax.rsqrt(jnp.mean(o * o, axis=-1, keepdims=True) + EPS)
        o_ref[:, vs] = (o * g_ref[:, vs].astype(F32)).astype(o_ref.dtype)


def _retention(p_main, cols, state0, c):
    b = state0.shape[0]
    nc = p_main.shape[0] // (b * c)
    kw = N_HEADS_R * KEY_DIM_R
    vw = N_HEADS_R * VAL_DIM_R
    dec, qsc, ksc, gcs = _retention_tables(c)
    qb, kb, vb, gb = cols
    const3 = lambda bi, ci: (0, 0, 0)
    st_spec = pl.BlockSpec((None, N_HEADS_R, KEY_DIM_R, VAL_DIM_R), lambda bi, ci: (bi, 0, 0, 0))
    return pl.pallas_call(
        _retention_kernel,
        grid=(b, nc),
        in_specs=[pl.BlockSpec((c, kw), lambda bi, ci: (bi * nc + ci, qb)),
                  pl.BlockSpec((c, kw), lambda bi, ci: (bi * nc + ci, kb)),
                  pl.BlockSpec((c, vw), lambda bi, ci: (bi * nc + ci, vb)),
                  pl.BlockSpec((c, vw), lambda bi, ci: (bi * nc + ci, gb)),
                  pl.BlockSpec(dec.shape, const3), pl.BlockSpec(qsc.shape, const3),
                  pl.BlockSpec(ksc.shape, const3), pl.BlockSpec(gcs.shape, const3),
                  st_spec],
        out_specs=[pl.BlockSpec((c, vw), lambda bi, ci: (bi * nc + ci, 0)), st_spec],
        out_shape=[jax.ShapeDtypeStruct((b * nc * c, vw), BF16),
                   jax.ShapeDtypeStruct(state0.shape, F32)],
        compiler_params=_cparams("parallel", "arbitrary"),
        name="retention",
    )(p_main, p_main, p_main, p_main, dec, qsc, ksc, gcs, state0)


def _merge_kernel(oa_ref, ob_ref, wa_ref, wb_ref, sa_ref, sb_ref, o_ref):
    for b in range(o_ref.shape[1] // MXU_COLS):
        sl = slice(b * MXU_COLS, (b + 1) * MXU_COLS)
        m = (_dot(oa_ref[...], wa_ref[:, sl]) * sa_ref[:, sl].astype(F32)
             + _dot(ob_ref[...], wb_ref[:, sl]) * sb_ref[:, sl].astype(F32))
        o_ref[:, sl] = m.astype(o_ref.dtype)


def _merge(o_a, o_b, w_pa, w_pb, p_main, ga_off, gb_off, tm, tn):
    n, wa = o_a.shape
    wb = o_b.shape[1]
    d = w_pa.shape[1]
    return pl.pallas_call(
        _merge_kernel,
        grid=(n // tm, d // tn),
        in_specs=[pl.BlockSpec((tm, wa), lambda i, j: (i, 0)),
                  pl.BlockSpec((tm, wb), lambda i, j: (i, 0)),
                  pl.BlockSpec((wa, tn), lambda i, j: (0, j)),
                  pl.BlockSpec((wb, tn), lambda i, j: (0, j)),
                  pl.BlockSpec((tm, tn), lambda i, j: (i, ga_off // tn + j)),
                  pl.BlockSpec((tm, tn), lambda i, j: (i, gb_off // tn + j))],
        out_specs=pl.BlockSpec((tm, tn), lambda i, j: (i, j)),
        out_shape=jax.ShapeDtypeStruct((n, d), BF16),
        compiler_params=_cparams("parallel", "arbitrary"),
        name="merge",
    )(o_a, o_b, w_pa, w_pb, p_main, p_main)


def _out_proj_kernel(x_ref, m_ref, w_ref, x1_ref):
    x1_ref[...] = x_ref[...] + _dot(m_ref[...], w_ref[...])


def _out_proj(x, merged, w_o, tm):
    n, d = x.shape
    row = lambda i: (i, 0)
    return pl.pallas_call(
        _out_proj_kernel,
        grid=(n // tm,),
        in_specs=[pl.BlockSpec((tm, d), row), pl.BlockSpec((tm, d), row),
                  pl.BlockSpec((d, d), lambda i: (0, 0))],
        out_specs=pl.BlockSpec((tm, d), row),
        out_shape=jax.ShapeDtypeStruct((n, d), F32),
        compiler_params=_cparams("parallel"),
        name="out_proj",
    )(x, merged, w_o)


def _ffn_kernel(x1_ref, g2_ref, wg_ref, wu_ref, wd_ref, g_ref, y_ref, h_ref, *, final_norm):
    c = pl.program_id(1)

    @pl.when(c == 0)
    def _():
        x1 = x1_ref[...]
        y_ref[...] = x1
        h_ref[...] = _rmsnorm_rows(x1, g2_ref[...]).astype(h_ref.dtype)

    acts = []
    for b in range(wg_ref.shape[1] // MXU_COLS):
        sl = slice(b * MXU_COLS, (b + 1) * MXU_COLS)
        a = _dot(h_ref[...], wg_ref[:, sl])
        u = _dot(h_ref[...], wu_ref[:, sl])
        acts.append((a / (1.0 + jnp.exp(-a)) * u).astype(BF16))
    upd = _dot(acts[0], wd_ref[0:MXU_COLS, :])
    for b in range(1, len(acts)):
        upd += _dot(acts[b], wd_ref[b * MXU_COLS:(b + 1) * MXU_COLS, :])
    y_ref[...] += upd

    if final_norm:
        @pl.when(c == pl.num_programs(1) - 1)
        def _():
            y_ref[...] = _rmsnorm_rows(y_ref[...], g_ref[...])


def _ffn(x1, g2, wg, wu, wd, gf, tm, tc, final_norm):
    n, d = x1.shape
    f = wg.shape[1]
    row = lambda i, c: (i, 0)
    vec = pl.BlockSpec((1, d), lambda i, c: (0, 0))
    return pl.pallas_call(
        functools.partial(_ffn_kernel, final_norm=final_norm),
        grid=(n // tm, f // tc),
        in_specs=[pl.BlockSpec((tm, d), row, pipeline_mode=pl.Buffered(1)), vec,
                  pl.BlockSpec((d, tc), lambda i, c: (0, c)),
                  pl.BlockSpec((d, tc), lambda i, c: (0, c)),
                  pl.BlockSpec((tc, d), lambda i, c: (c, 0)), vec],
        out_specs=pl.BlockSpec((tm, d), row),
        out_shape=jax.ShapeDtypeStruct((n, d), F32),
        scratch_shapes=[pltpu.VMEM((tm, d), BF16)],
        compiler_params=_cparams("parallel", "arbitrary"),
        name="ffn",
    )(x1, g2.reshape(1, d), wg, wu, wd, gf.reshape(1, d))


class _MainLayout:
    def __init__(self, d_model):
        wa = N_HEADS_A * HEAD_DIM_A
        wb = N_HEADS_R * VAL_DIM_R
        kr = N_HEADS_R * KEY_DIM_R
        qi = N_HEADS_IDX * HEAD_DIM_IDX
        order = [("qa", wa, "rope128", HEAD_DIM_A ** -0.5 * LOG2E), ("vr", wb, "plain", 1.0),
                 ("gr", wb, "silu", 1.0), ("ga", d_model, "sigmoid", 1.0), ("gb", d_model, "sigmoid", 1.0),
                 ("qi", qi, "rope64", 1.0), ("qr", kr, "rope128", 1.0),
                 ("kr", kr, "rope128", KEY_DIM_R ** -0.5)]
        self.off, self.width = {}, {}
        self.order = order
        o = 0
        for name, w, _, _ in order:
            self.off[name], self.width[name] = o, w
            o += w
        self.total = o

    def groups(self, tn):
        out = []
        for name, w, mode, scale in self.order:
            assert self.off[name] % tn == 0 and w % tn == 0
            out.append((self.off[name] // tn, (self.off[name] + w) // tn, mode, scale))
        return tuple(out)

    def block(self, name, width=None):
        width = width or self.width[name]
        assert self.off[name] % width == 0
        return self.off[name] // width


def _split_w_in(w, d_model):
    wa = N_HEADS_A * HEAD_DIM_A
    nk = N_KV_A * HEAD_DIM_A
    wb = N_HEADS_R * VAL_DIM_R
    kr = N_HEADS_R * KEY_DIM_R
    names = ("qa", "ka", "va", "qi", "ki", "wi", "qr", "kr", "vr", "gr", "ga", "gb")
    widths = (wa, nk, nk, N_HEADS_IDX * HEAD_DIM_IDX, HEAD_DIM_IDX, N_HEADS_IDX, kr, kr, wb, wb,
              d_model, d_model)
    parts, o = {}, 0
    for name, wd in zip(names, widths):
        parts[name] = w[:, o:o + wd]
        o += wd
    assert o == w.shape[1]
    return parts


def _tile(n, pref):
    t = min(n, pref)
    assert n % t == 0
    return t


def _dense_pre(x2, pos_rows, lw, lay):
    n = x2.shape[0]
    rope = _rope_tables(pos_rows)
    h, *kv = _proj_kv(x2, lw["g1"], lw["w_kv"], rope, _tile(n, 512))
    tn = 1024
    p_main = _proj_main(h, lw["w_main"], rope, lay.groups(tn), _tile(n, 1024), tn)
    return p_main, kv


def _dense_post(x2, o_a, o_b, p_main, lw, lay, gf, final_norm):
    n = x2.shape[0]
    tm = _tile(n, 1024)
    merged = _merge(o_a, o_b, lw["w_pa"], lw["w_pb"], p_main, lay.off["ga"], lay.off["gb"], tm, 512)
    x1 = _out_proj(x2, merged, lw["w_o"], _tile(n, 512))
    return _ffn(x1, lw["g2"], lw["wg"], lw["wu"], lw["wd"], gf, tm, 512, final_norm)


def _prompt_layer(xp, lw, lay, gf, final_norm):
    b, t, d = xp.shape
    x2 = xp.reshape(b * t, d)
    p_main, (ka, va, ki, wt, ka_bf, vt_bf, kia, kib) = _dense_pre(x2, jnp.arange(t, dtype=jnp.int32), lw, lay)
    top_k = min(TOPK_MAX, t // 4)
    tq, ck = _tile(t, 256), _tile(t, 512)
    mask_t = _index_mask(p_main, lay.block("qi"), wt, kia.reshape(b, t, LANES), kib.reshape(b, t, LANES),
                         tq=tq, ck=ck, l_valid=t, pos_base=0, causal=True, top_k=top_k)
    nk = N_KV_A * HEAD_DIM_A
    o_a = _attention(p_main, ka_bf.reshape(b, t, nk), vt_bf, mask_t, tq=tq, tk=ck, causal=True, kvb=2)
    state0 = jnp.zeros((b, N_HEADS_R, KEY_DIM_R, VAL_DIM_R), F32)
    cols = (lay.block("qr"), lay.block("kr"), lay.block("vr"), lay.block("gr"))
    o_b, st = _retention(p_main, cols, state0, _tile(t, 256))
    y = _dense_post(x2, o_a, o_b, p_main, lw, lay, gf, final_norm)
    return (y.reshape(b, t, d), ka.reshape(b, t, N_KV_A, HEAD_DIM_A), va.reshape(b, t, N_KV_A, HEAD_DIM_A),
            ki.reshape(b, t, HEAD_DIM_IDX), st)


def _sample_layer(xs, cache_k, cache_v, cache_i, state, lw, lay, gf, final_norm):
    b, t, d = xs.shape
    past = cache_k.shape[1]
    nk = N_KV_A * HEAD_DIM_A
    x2 = xs.reshape(b * t, d)
    pos = jnp.tile(past + jnp.arange(t, dtype=jnp.int32), b)
    p_main, (ka, va, ki, wt, ka_bf, vt_bf, kia, kib) = _dense_pre(x2, pos, lw, lay)

    tq = LANES
    l_valid = past + t
    tk = 3 * LANES
    lp = -(-l_valid // tk) * tk
    pq = jnp.pad(p_main.reshape(b, t, -1), ((0, 0), (0, tq - t), (0, 0)), mode="edge").reshape(b * tq, -1)
    wtq = jnp.pad(wt.reshape(N_HEADS_IDX, b, t), ((0, 0), (0, 0), (0, tq - t)),
                  mode="edge").reshape(N_HEADS_IDX, b * tq)
    kpad = ((0, 0), (0, lp - l_valid), (0, 0))
    ci = cache_i.astype(BF16)
    zi = jnp.zeros_like(ci)
    kia_all = jnp.pad(jnp.concatenate([jnp.concatenate([ci, zi], -1), kia.reshape(b, t, LANES)], 1), kpad)
    kib_all = jnp.pad(jnp.concatenate([jnp.concatenate([zi, ci], -1), kib.reshape(b, t, LANES)], 1), kpad)
    k_all = jnp.pad(jnp.concatenate([cache_k.reshape(b, past, nk).astype(BF16), ka_bf.reshape(b, t, nk)], 1), kpad)
    vt_new = vt_bf.reshape(N_KV_A, VT_ROWS, b, t)
    vt_cache = cache_v.reshape(b, past, N_KV_A, HEAD_DIM_A).transpose(2, 3, 0, 1).astype(BF16)
    ones_rows = jnp.zeros((N_KV_A, VT_ROWS - HEAD_DIM_A, b, past), BF16).at[:, 0].set(1.0)
    vt_all = jnp.concatenate([jnp.concatenate([vt_cache, ones_rows], 1), vt_new], 3)
    vt_all = jnp.pad(vt_all, ((0, 0), (0, 0), (0, 0), (0, lp - l_valid))).reshape(N_KV_A * VT_ROWS, b * lp)

    top_k = min(TOPK_MAX, l_valid // 4)
    mask_t = _index_mask(pq, lay.block("qi"), wtq, kia_all, kib_all, tq=tq, ck=tk, l_valid=l_valid,
                         pos_base=past, causal=False, top_k=top_k)
    o_a = _attention(pq, k_all, vt_all, mask_t, tq=tq, tk=tk, causal=False, kvb=1)
    o_a = o_a.reshape(b, tq, -1)[:, :t].reshape(b * t, -1)
    cols = (lay.block("qr"), lay.block("kr"), lay.block("vr"), lay.block("gr"))
    o_b, st = _retention(p_main, cols, state.astype(F32), t)
    y = _dense_post(x2, o_a, o_b, p_main, lw, lay, gf, final_norm)
    return (y.reshape(b, t, d), ka.reshape(b, t, N_KV_A, HEAD_DIM_A), va.reshape(b, t, N_KV_A, HEAD_DIM_A),
            ki.reshape(b, t, HEAD_DIM_IDX), st)


def kernel(x_prompt, x_sample, cache_k, cache_v, cache_idx_k, state_ret, norm1_g, w_in, w_pa, w_pb, w_o,
           norm2_g, w_ffn_gate, w_ffn_up, w_ffn_down, norm_f_g):
    assert HEAD_DIM_A == LANES and HEAD_DIM_IDX * 2 == LANES and N_HEADS_IDX % 2 == 0
    depth, d_model = norm1_g.shape
    lay = _MainLayout(d_model)
    xp, xs = x_prompt, x_sample
    outs = [[] for _ in range(8)]
    for l in range(depth):
        parts = _split_w_in(w_in[l], d_model)
        pad = jnp.zeros((d_model, LANES - HEAD_DIM_IDX - N_HEADS_IDX), F32)
        lw = {
            "g1": norm1_g[l], "g2": norm2_g[l],
            "w_main": jnp.concatenate([parts[name] for name, _, _, _ in lay.order], axis=1).astype(BF16),
            "w_kv": jnp.concatenate([parts["ka"], parts["va"], parts["ki"], parts["wi"], pad], axis=1).astype(BF16),
            "w_pa": w_pa[l].astype(BF16), "w_pb": w_pb[l].astype(BF16), "w_o": w_o[l].astype(BF16),
            "wg": w_ffn_gate[l].astype(BF16), "wu": w_ffn_up[l].astype(BF16), "wd": w_ffn_down[l].astype(BF16),
        }
        last = l == depth - 1
        xp, kp, vp, ip, sp = _prompt_layer(xp, lw, lay, norm_f_g, last)
        xs, ks, vs, isl, ss = _sample_layer(xs, cache_k[l], cache_v[l], cache_idx_k[l], state_ret[l],
                                            lw, lay, norm_f_g, last)
        for lst, val in zip(outs, (kp, vp, ip, sp, ks, vs, isl, ss)):
            lst.append(val)
    stacked = [jnp.stack(o) for o in outs]
    return (xp, xs, stacked[0], stacked[1], stacked[2], stacked[3].astype(x_prompt.dtype),
            stacked[4], stacked[5], stacked[6], stacked[7].astype(state_ret.dtype))
```

```python
import functools
import math

import jax
import jax.numpy as jnp
from jax import lax
from jax.experimental import pallas as pl
from jax.experimental.pallas import tpu as pltpu

F32 = jnp.float32
BF16 = jnp.bfloat16

CHUNK = 64
EPS = 1e-6
ROPE_THETA = 10000.0
N_HEADS_A = 16
N_KV_A = 4
HEAD_DIM_A = 128
N_HEADS_IDX = 16
HEAD_DIM_IDX = 64
TOPK_MAX = 256
N_HEADS_R = 8
KEY_DIM_R = 128
VAL_DIM_R = 256

LANES = 128
MXU_COLS = 256
V7X_VMEM_BYTES = 64 * 1024 * 1024
VMEM_LIMIT = V7X_VMEM_BYTES - 8 * 1024 * 1024
BF16_SUBLANES = 16
VT_ROWS = HEAD_DIM_A + BF16_SUBLANES
LOG2E = math.log2(math.e)
MAX_BISECT_ITERS = 300
NEG_BIG = float(jnp.finfo(jnp.float32).min)


def _cparams(*sem):
    return pltpu.CompilerParams(dimension_semantics=sem, vmem_limit_bytes=VMEM_LIMIT)


def _dot(a, b):
    return jnp.dot(a, b, preferred_element_type=F32)


def _dot_nt(a, b):
    return lax.dot_general(a, b, (((1,), (1,)), ((), ())), preferred_element_type=F32)


def _rmsnorm_rows(x, g):
    return x * lax.rsqrt(jnp.mean(x * x, axis=-1, keepdims=True) + EPS) * g


def _rope_tables(pos):
    posf = pos.astype(F32)[:, None]

    def cs(half):
        inv_freq = ROPE_THETA ** (-jnp.arange(half, dtype=F32) / half)
        ang = posf * inv_freq[None, :]
        return jnp.cos(ang), jnp.sin(ang)

    c, s = cs(HEAD_DIM_A // 2)
    c128 = jnp.concatenate([c, c], axis=1)
    s128 = jnp.concatenate([-s, s], axis=1)
    c, s = cs(HEAD_DIM_IDX // 2)
    z = jnp.zeros_like(s)
    c64 = jnp.tile(c, (1, 4))
    a64 = jnp.tile(jnp.concatenate([-s, z], axis=1), (1, 2))
    b64 = jnp.tile(jnp.concatenate([z, s], axis=1), (1, 2))
    return jnp.stack([c128, s128, c64, a64, b64])


def _rope128(x, rope_ref):
    return x * rope_ref[0] + pltpu.roll(x, 64, 1) * rope_ref[1]


def _rope64(x, rope_ref):
    return (x * rope_ref[2] + pltpu.roll(x, 96, 1) * rope_ref[3]
            + pltpu.roll(x, 32, 1) * rope_ref[4])


def _proj_main_kernel(h_ref, w_ref, rope_ref, o_ref, *, groups, tn):
    j = pl.program_id(1)
    for lo, hi, mode, scale in groups:
        @pl.when(jnp.logical_and(j >= lo, j < hi))
        def _(mode=mode, scale=scale):
            for b in range(tn // MXU_COLS):
                acc = _dot(h_ref[...], w_ref[:, b * MXU_COLS:(b + 1) * MXU_COLS])
                for c in range(MXU_COLS // LANES):
                    x = acc[:, c * LANES:(c + 1) * LANES]
                    if mode == "rope128":
                        y = _rope128(x, rope_ref)
                    elif mode == "rope64":
                        y = _rope64(x, rope_ref)
                    elif mode == "sigmoid":
                        y = 1.0 / (1.0 + jnp.exp(-x))
                    elif mode == "silu":
                        y = x / (1.0 + jnp.exp(-x))
                    else:
                        y = x
                    if scale != 1.0:
                        y = y * scale
                    col = b * MXU_COLS + c * LANES
                    o_ref[:, col:col + LANES] = y.astype(o_ref.dtype)


def _proj_main(h, w_main, rope, groups, tm, tn):
    n, d = h.shape
    p = w_main.shape[1]
    n_pos_tiles = rope.shape[1] // tm
    return pl.pallas_call(
        functools.partial(_proj_main_kernel, groups=groups, tn=tn),
        grid=(n // tm, p // tn),
        in_specs=[pl.BlockSpec((tm, d), lambda i, j: (i, 0)),
                  pl.BlockSpec((d, tn), lambda i, j: (0, j)),
                  pl.BlockSpec((5, tm, LANES), lambda i, j: (0, i % n_pos_tiles, 0))],
        out_specs=pl.BlockSpec((tm, tn), lambda i, j: (i, j)),
        out_shape=jax.ShapeDtypeStruct((n, p), BF16),
        compiler_params=_cparams("parallel", "arbitrary"),
        name="proj_main",
    )(h, w_main, rope)


def _proj_kv_kernel(x_ref, g_ref, w_ref, rope_ref, h_ref, ka_ref, va_ref, ki_ref, wt_ref,
                    kab_ref, vtb_ref, kia_ref, kib_ref, acc_ref, *, nk, idx_scale):
    h_ref[...] = _rmsnorm_rows(x_ref[...], g_ref[...]).astype(h_ref.dtype)
    acc_ref[...] = _dot(h_ref[...], w_ref[...])
    tm = acc_ref.shape[0]
    for c in range(N_KV_A):
        sl = slice(c * LANES, (c + 1) * LANES)
        y = _rope128(acc_ref[:, sl], rope_ref)
        ka_ref[pl.ds(c, tm, stride=N_KV_A), :] = y
        kab_ref[:, sl] = y.astype(BF16)
        va_ref[pl.ds(c, tm, stride=N_KV_A), :] = acc_ref[:, nk + c * LANES:nk + (c + 1) * LANES]
    ones_rows = jnp.where(lax.broadcasted_iota(jnp.int32, (VT_ROWS - HEAD_DIM_A, tm), 0) == 0, 1.0, 0.0)
    for g in range(N_KV_A):
        vg = acc_ref[:, nk + g * HEAD_DIM_A:nk + (g + 1) * HEAD_DIM_A]
        vtb_ref[g * VT_ROWS:g * VT_ROWS + HEAD_DIM_A, :] = vg.T.astype(BF16)
        vtb_ref[g * VT_ROWS + HEAD_DIM_A:(g + 1) * VT_ROWS, :] = ones_rows.astype(BF16)
    z = acc_ref[:, 2 * nk:2 * nk + LANES]
    y = _rope64(z, rope_ref)
    ki_ref[...] = y[:, :HEAD_DIM_IDX]
    lane = lax.broadcasted_iota(jnp.int32, y.shape, 1)
    ya = jnp.where(lane < HEAD_DIM_IDX, y, 0.0)
    kia_ref[...] = ya.astype(BF16)
    kib_ref[...] = pltpu.roll(ya, HEAD_DIM_IDX, 1).astype(BF16)
    wt_ref[...] = (z * idx_scale).T[HEAD_DIM_IDX:HEAD_DIM_IDX + N_HEADS_IDX, :]


def _proj_kv(x, g1, w_kv, rope, tm):
    n, d = x.shape
    nk = N_KV_A * HEAD_DIM_A
    pw = w_kv.shape[1]
    n_pos_tiles = rope.shape[1] // tm
    idx_scale = (HEAD_DIM_IDX ** -0.5) * (N_HEADS_IDX ** -0.5)
    row = lambda i: (i, 0)
    return pl.pallas_call(
        functools.partial(_proj_kv_kernel, nk=nk, idx_scale=idx_scale),
        grid=(n // tm,),
        in_specs=[pl.BlockSpec((tm, d), row),
                  pl.BlockSpec((1, d), lambda i: (0, 0)),
                  pl.BlockSpec((d, pw), lambda i: (0, 0)),
                  pl.BlockSpec((5, tm, LANES), lambda i: (0, i % n_pos_tiles, 0))],
        out_specs=[pl.BlockSpec((tm, d), row),
                   pl.BlockSpec((tm * N_KV_A, HEAD_DIM_A), row), pl.BlockSpec((tm * N_KV_A, HEAD_DIM_A), row),
                   pl.BlockSpec((tm, HEAD_DIM_IDX), row),
                   pl.BlockSpec((N_HEADS_IDX, tm), lambda i: (0, i)),
                   pl.BlockSpec((tm, nk), row),
                   pl.BlockSpec((N_KV_A * VT_ROWS, tm), lambda i: (0, i)),
                   pl.BlockSpec((tm, LANES), row), pl.BlockSpec((tm, LANES), row)],
        out_shape=[jax.ShapeDtypeStruct((n, d), BF16),
                   jax.ShapeDtypeStruct((n * N_KV_A, HEAD_DIM_A), F32),
                   jax.ShapeDtypeStruct((n * N_KV_A, HEAD_DIM_A), F32),
                   jax.ShapeDtypeStruct((n, HEAD_DIM_IDX), F32),
                   jax.ShapeDtypeStruct((N_HEADS_IDX, n), F32),
                   jax.ShapeDtypeStruct((n, nk), BF16),
                   jax.ShapeDtypeStruct((N_KV_A * VT_ROWS, n), BF16),
                   jax.ShapeDtypeStruct((n, LANES), BF16), jax.ShapeDtypeStruct((n, LANES), BF16)],
        scratch_shapes=[pltpu.VMEM((tm, pw), F32)],
        compiler_params=_cparams("parallel"),
        name="proj_kv",
    )(x, g1.reshape(1, d), w_kv, rope)


def _index_kernel(qi_ref, wt_ref, kia_ref, kib_ref, m_ref, s_ref, *,
                  tq, ck, n_ck_total, l_valid, pos_base, causal, top_k):
    i = pl.program_id(1)
    if causal:
        n_c = (i * tq + tq + ck - 1) // ck
    else:
        n_c = n_ck_total
    qpos = pos_base + i * tq + lax.broadcasted_iota(jnp.int32, (1, tq), 1)
    lim = jnp.minimum(qpos - lax.rem(qpos, CHUNK) + CHUNK, l_valid)
    limf = lim.astype(F32)
    kprime = jnp.minimum(float(top_k), limf)
    inf = jnp.float32(jnp.inf)

    def key_index(off):
        return off + lax.broadcasted_iota(jnp.int32, (ck, tq), 0)

    def score_chunk(c, carry):
        rmax, rmin = carry
        off = pl.multiple_of(c * ck, ck)
        ka = kia_ref[pl.ds(off, ck), :]
        kb = kib_ref[pl.ds(off, ck), :]
        acc = jnp.zeros((ck, tq), F32)
        for p in range(N_HEADS_IDX // 2):
            qp = qi_ref[:, p * LANES:(p + 1) * LANES]
            acc += jnp.maximum(_dot_nt(ka, qp), 0.0) * wt_ref[2 * p:2 * p + 1, :]
            acc += jnp.maximum(_dot_nt(kb, qp), 0.0) * wt_ref[2 * p + 1:2 * p + 2, :]
        adm = key_index(off) < lim
        s_ref[pl.ds(off, ck), :] = jnp.where(adm, acc, -inf)
        rmax = jnp.maximum(rmax, jnp.max(jnp.where(adm, acc, -inf), axis=0, keepdims=True))
        rmin = jnp.minimum(rmin, jnp.min(jnp.where(adm, acc, inf), axis=0, keepdims=True))
        return rmax, rmin

    rmax, rmin = lax.fori_loop(0, n_c, score_chunk,
                               (jnp.full((1, tq), -inf, F32), jnp.full((1, tq), inf, F32)))

    fold_rows = min(ck, 64)

    def fold(ind):
        return ind.reshape(ck // fold_rows, fold_rows, tq).sum(axis=0)

    def count_ge(th):
        def body(c, acc):
            off = pl.multiple_of(c * ck, ck)
            return acc + fold(jnp.where(s_ref[pl.ds(off, ck), :] >= th, 1.0, 0.0))
        acc = lax.fori_loop(0, n_c, body, jnp.zeros((fold_rows, tq), F32))
        return acc.sum(axis=0, keepdims=True)

    def cond(st):
        it, lo, hi, cnt, stuck = st
        active = jnp.where(jnp.logical_and(cnt > kprime, stuck < 0.5), 1.0, 0.0)
        return jnp.logical_and(it < MAX_BISECT_ITERS, jnp.max(active) > 0.5)

    def body(st):
        it, lo, hi, cnt, stuck = st
        open_top = hi == inf
        mid = lo + 0.5 * (jnp.where(open_top, rmax, hi) - lo)
        mid = jnp.where(jnp.logical_and(open_top, mid <= lo), rmax, mid)
        c = count_ge(mid)
        ge = c >= kprime
        no_progress = jnp.logical_or(mid <= lo, mid >= hi)
        return (it + 1, jnp.where(ge, mid, lo), jnp.where(ge, hi, mid), jnp.where(ge, c, cnt),
                jnp.where(no_progress, 1.0, stuck))

    _, lo, hi, cnt, _ = lax.while_loop(
        cond, body, (jnp.int32(0), rmin, jnp.full((1, tq), inf, F32), limf, jnp.zeros((1, tq), F32)))

    unresolved = jnp.max(jnp.where(cnt > kprime, 1.0, 0.0)) > 0.5

    @pl.when(jnp.logical_not(unresolved))
    def _():
        def body(c, _):
            off = pl.multiple_of(c * ck, ck)
            m_ref[pl.ds(off, ck), :] = jnp.where(s_ref[pl.ds(off, ck), :] >= lo, 0.0, -inf).astype(m_ref.dtype)
            return 0
        lax.fori_loop(0, n_c, body, 0)

    @pl.when(unresolved)
    def _():
        need = kprime - count_ge(hi)

        def in_tie(blk):
            return jnp.logical_and(blk >= lo, blk < hi)

        def count_tie_below(jcut):
            def body(c, acc):
                off = pl.multiple_of(c * ck, ck)
                blk = s_ref[pl.ds(off, ck), :]
                e = jnp.logical_and(in_tie(blk), key_index(off).astype(F32) < jcut)
                return acc + fold(jnp.where(e, 1.0, 0.0))
            acc = lax.fori_loop(0, n_c, body, jnp.zeros((fold_rows, tq), F32))
            return acc.sum(axis=0, keepdims=True)

        def jbody(_, st):
            jlo, jhi = st
            mid = jnp.floor(0.5 * (jlo + jhi))
            ok = count_tie_below(mid) >= need
            return jnp.where(ok, jlo, mid + 1.0), jnp.where(ok, mid, jhi)

        n_total = n_ck_total * ck
        _, jcut = lax.fori_loop(0, int(math.ceil(math.log2(n_total + 1))), jbody,
                                (jnp.zeros((1, tq), F32), jnp.full((1, tq), float(n_total), F32)))

        def body(c, _):
            off = pl.multiple_of(c * ck, ck)
            blk = s_ref[pl.ds(off, ck), :]
            e = jnp.logical_and(in_tie(blk), key_index(off).astype(F32) < jcut)
            keep = jnp.logical_or(blk >= hi, e)
            m_ref[pl.ds(off, ck), :] = jnp.where(keep, 0.0, -inf).astype(m_ref.dtype)
            return 0
        lax.fori_loop(0, n_c, body, 0)

    def zero_body(c, _):
        off = pl.multiple_of(c * ck, ck)
        m_ref[pl.ds(off, ck), :] = jnp.full((ck, tq), -inf, m_ref.dtype)
        return 0
    lax.fori_loop(n_c, n_ck_total, zero_body, 0)


def _index_mask(p_main, qi_col_block, wt, kia, kib, *, tq, ck, l_valid, pos_base, causal, top_k):
    b, lp, _ = kia.shape
    nq = p_main.shape[0] // (b * tq)
    qw = N_HEADS_IDX * HEAD_DIM_IDX
    return pl.pallas_call(
        functools.partial(_index_kernel, tq=tq, ck=ck, n_ck_total=lp // ck, l_valid=l_valid,
                          pos_base=pos_base, causal=causal, top_k=top_k),
        grid=(b, nq),
        in_specs=[pl.BlockSpec((tq, qw), lambda bi, i: (bi * nq + i, qi_col_block)),
                  pl.BlockSpec((N_HEADS_IDX, tq), lambda bi, i: (0, bi * nq + i)),
                  pl.BlockSpec((None, lp, LANES), lambda bi, i: (bi, 0, 0)),
                  pl.BlockSpec((None, lp, LANES), lambda bi, i: (bi, 0, 0))],
        out_specs=pl.BlockSpec((None, lp, tq), lambda bi, i: (bi, 0, i)),
        out_shape=jax.ShapeDtypeStruct((b, lp, nq * tq), BF16),
        scratch_shapes=[pltpu.VMEM((lp, tq), F32)],
        compiler_params=_cparams("parallel", "parallel"),
        name="index_mask",
    )(p_main, wt, kia, kib)


def _attn_kernel(q_ref, k_ref, vt_ref, m_ref, o_ref, acc_ref, sa_ref, sb_ref, ma_ref, mb_ref, *,
                 tq, tk, n_kt_total, causal, group, kvb):
    i = pl.program_id(1)
    heads = kvb * group
    if causal:
        n_kt = (i * tq + tq + tk - 1) // tk
    else:
        n_kt = n_kt_total
    acc_ref[...] = jnp.zeros_like(acc_ref)

    def scores(kt, bufs):
        s_ref, mx_ref = bufs
        off = pl.multiple_of(kt * tk, tk)
        bias = m_ref[pl.ds(off, tk), :].astype(F32)
        for h in range(heads):
            kv = h // group
            k = k_ref[pl.ds(off, tk), kv * HEAD_DIM_A:(kv + 1) * HEAD_DIM_A]
            s = _dot_nt(k, q_ref[:, h * HEAD_DIM_A:(h + 1) * HEAD_DIM_A]) + bias
            s_ref[h] = s
            mx_ref[h] = jnp.max(s, axis=0, keepdims=True)

    def softmax_pv(kt, bufs, ms):
        s_ref, mx_ref = bufs
        off = pl.multiple_of(kt * tk, tk)
        out = []
        for h in range(heads):
            kv = h // group
            vt = vt_ref[kv * VT_ROWS:(kv + 1) * VT_ROWS, pl.ds(off, tk)]
            m_new = jnp.maximum(ms[h], mx_ref[h])
            alpha = jnp.exp2(ms[h] - m_new)
            p = jnp.exp2(s_ref[h] - m_new).astype(BF16)
            acc_ref[h] = acc_ref[h] * alpha + _dot(vt, p)
            out.append(m_new)
        return tuple(out)

    buf_a, buf_b = (sa_ref, ma_ref), (sb_ref, mb_ref)

    def body(j, ms):
        scores(2 * j + 1, buf_b)
        ms = softmax_pv(2 * j, buf_a, ms)
        scores(2 * j + 2, buf_a)
        return softmax_pv(2 * j + 1, buf_b, ms)

    scores(0, buf_a)
    n_pairs = (n_kt - 1) // 2
    ms = lax.fori_loop(0, n_pairs, body, tuple(jnp.full((1, tq), NEG_BIG, F32) for _ in range(heads)))
    even = n_kt - 1 > 2 * n_pairs

    @pl.when(even)
    def _():
        scores(2 * n_pairs + 1, buf_b)
        softmax_pv(2 * n_pairs + 1, buf_b, softmax_pv(2 * n_pairs, buf_a, ms))

    @pl.when(jnp.logical_not(even))
    def _():
        softmax_pv(2 * n_pairs, buf_a, ms)

    for h in range(heads):
        a = acc_ref[h]
        o = a[:HEAD_DIM_A] / a[HEAD_DIM_A:HEAD_DIM_A + 1]
        o_ref[:, h * HEAD_DIM_A:(h + 1) * HEAD_DIM_A] = o.T.astype(o_ref.dtype)


def _attention(p_main, k_bf, vt_bf, mask_t, *, tq, tk, causal, kvb):
    b, lp, _ = k_bf.shape
    nq = mask_t.shape[2] // tq
    group = N_HEADS_A // N_KV_A
    heads = kvb * group
    gw = heads * HEAD_DIM_A
    return pl.pallas_call(
        functools.partial(_attn_kernel, tq=tq, tk=tk, n_kt_total=lp // tk, causal=causal, group=group,
                          kvb=kvb),
        grid=(b, nq, N_KV_A // kvb),
        in_specs=[pl.BlockSpec((tq, gw), lambda bi, i, g: (bi * nq + i, g)),
                  pl.BlockSpec((None, lp, kvb * HEAD_DIM_A), lambda bi, i, g: (bi, 0, g)),
                  pl.BlockSpec((kvb * VT_ROWS, lp), lambda bi, i, g: (g, bi)),
                  pl.BlockSpec((None, lp, tq), lambda bi, i, g: (bi, 0, i))],
        out_specs=pl.BlockSpec((tq, gw), lambda bi, i, g: (bi * nq + i, g)),
        out_shape=jax.ShapeDtypeStruct((b * nq * tq, N_HEADS_A * HEAD_DIM_A), BF16),
        scratch_shapes=[pltpu.VMEM((heads, VT_ROWS, tq), F32),
                        pltpu.VMEM((heads, tk, tq), F32), pltpu.VMEM((heads, tk, tq), F32),
                        pltpu.VMEM((heads, 1, tq), F32), pltpu.VMEM((heads, 1, tq), F32)],
        compiler_params=_cparams("parallel", "parallel", "arbitrary"),
        name="attention",
    )(p_main, k_bf, vt_bf, mask_t)


def _retention_tables(c):
    log_gamma = jnp.log1p(-(2.0 ** (-5.0 - jnp.arange(N_HEADS_R, dtype=F32))))
    idx = jnp.arange(c, dtype=F32)
    diff = idx[:, None] - idx[None, :]
    decay = jnp.where(diff[None] >= 0,
                      jnp.exp(jnp.maximum(diff, 0.0)[None] * log_gamma[:, None, None]), 0.0)
    cross = jnp.exp((idx + 1.0)[None, :] * log_gamma[:, None])
    kdec = jnp.exp((c - 1.0 - idx)[None, :] * log_gamma[:, None])
    full = jnp.exp(c * log_gamma)
    bc = lambda a, w: jnp.broadcast_to(a[..., None], a.shape + (w,))
    return decay, bc(cross, KEY_DIM_R), bc(kdec, KEY_DIM_R), bc(full[:, None], VAL_DIM_R)


def _retention_kernel(q_ref, k_ref, v_ref, g_ref, dec_ref, qsc_ref, ksc_ref, gc_ref, s0_ref,
                      o_ref, st_ref):
    @pl.when(pl.program_id(1) == 0)
    def _():
        st_ref[...] = s0_ref[...]

    for h in range(N_HEADS_R):
        ks = slice(h * KEY_DIM_R, (h + 1) * KEY_DIM_R)
        vs = slice(h * VAL_DIM_R, (h + 1) * VAL_DIM_R)
        q = q_ref[:, ks]
        k = k_ref[:, ks]
        v = v_ref[:, vs]
        st = st_ref[h]
        inner = _dot_nt(q, k) * dec_ref[h]
        qd = (q.astype(F32) * qsc_ref[h]).astype(BF16)
        o = _dot(inner.astype(BF16), v) + _dot(qd, st.astype(BF16))
        kdt = (k.astype(F32) * ksc_ref[h]).T.astype(BF16)
        st_ref[h] = st * gc_ref[h] + _dot(kdt, v)
        o = o * lax.rsqrt(jnp.mean(o * o, axis=-1, keepdims=True) + EPS)
        o_ref[:, vs] = (o * g_ref[:, vs].astype(F32)).astype(o_ref.dtype)


def _retention(p_main, cols, state0, c):
    b = state0.shape[0]
    nc = p_main.shape[0] // (b * c)
    kw = N_HEADS_R * KEY_DIM_R
    vw = N_HEADS_R * VAL_DIM_R
    dec, qsc, ksc, gcs = _retention_tables(c)
    qb, kb, vb, gb = cols
    const3 = lambda bi, ci: (0, 0, 0)
    st_spec = pl.BlockSpec((None, N_HEADS_R, KEY_DIM_R, VAL_DIM_R), lambda bi, ci: (bi, 0, 0, 0))
    return pl.pallas_call(
        _retention_kernel,
        grid=(b, nc),
        in_specs=[pl.BlockSpec((c, kw), lambda bi, ci: (bi * nc + ci, qb)),
                  pl.BlockSpec((c, kw), lambda bi, ci: (bi * nc + ci, kb)),
                  pl.BlockSpec((c, vw), lambda bi, ci: (bi * nc + ci, vb)),
                  pl.BlockSpec((c, vw), lambda bi, ci: (bi * nc + ci, gb)),
                  pl.BlockSpec(dec.shape, const3), pl.BlockSpec(qsc.shape, const3),
                  pl.BlockSpec(ksc.shape, const3), pl.BlockSpec(gcs.shape, const3),
                  st_spec],
        out_specs=[pl.BlockSpec((c, vw), lambda bi, ci: (bi * nc + ci, 0)), st_spec],
        out_shape=[jax.ShapeDtypeStruct((b * nc * c, vw), BF16),
                   jax.ShapeDtypeStruct(state0.shape, F32)],
        compiler_params=_cparams("parallel", "arbitrary"),
        name="retention",
    )(p_main, p_main, p_main, p_main, dec, qsc, ksc, gcs, state0)


def _merge_kernel(oa_ref, ob_ref, wa_ref, wb_ref, sa_ref, sb_ref, o_ref):
    for b in range(o_ref.shape[1] // MXU_COLS):
        sl = slice(b * MXU_COLS, (b + 1) * MXU_COLS)
        m = (_dot(oa_ref[...], wa_ref[:, sl]) * sa_ref[:, sl].astype(F32)
             + _dot(ob_ref[...], wb_ref[:, sl]) * sb_ref[:, sl].astype(F32))
        o_ref[:, sl] = m.astype(o_ref.dtype)


def _merge(o_a, o_b, w_pa, w_pb, p_main, ga_off, gb_off, tm, tn):
    n, wa = o_a.shape
    wb = o_b.shape[1]
    d = w_pa.shape[1]
    return pl.pallas_call(
        _merge_kernel,
        grid=(n // tm, d // tn),
        in_specs=[pl.BlockSpec((tm, wa), lambda i, j: (i, 0)),
                  pl.BlockSpec((tm, wb), lambda i, j: (i, 0)),
                  pl.BlockSpec((wa, tn), lambda i, j: (0, j)),
                  pl.BlockSpec((wb, tn), lambda i, j: (0, j)),
                  pl.BlockSpec((tm, tn), lambda i, j: (i, ga_off // tn + j)),
                  pl.BlockSpec((tm, tn), lambda i, j: (i, gb_off // tn + j))],
        out_specs=pl.BlockSpec((tm, tn), lambda i, j: (i, j)),
        out_shape=jax.ShapeDtypeStruct((n, d), BF16),
        compiler_params=_cparams("parallel", "arbitrary"),
        name="merge",
    )(o_a, o_b, w_pa, w_pb, p_main, p_main)


def _out_proj_kernel(x_ref, m_ref, w_ref, x1_ref):
    x1_ref[...] = x_ref[...] + _dot(m_ref[...], w_ref[...])


def _out_proj(x, merged, w_o, tm):
    n, d = x.shape
    row = lambda i: (i, 0)
    return pl.pallas_call(
        _out_proj_kernel,
        grid=(n // tm,),
        in_specs=[pl.BlockSpec((tm, d), row), pl.BlockSpec((tm, d), row),
                  pl.BlockSpec((d, d), lambda i: (0, 0))],
        out_specs=pl.BlockSpec((tm, d), row),
        out_shape=jax.ShapeDtypeStruct((n, d), F32),
        compiler_params=_cparams("parallel"),
        name="out_proj",
    )(x, merged, w_o)


def _ffn_kernel(x1_ref, g2_ref, wg_ref, wu_ref, wd_ref, g_ref, y_ref, h_ref, *, final_norm):
    c = pl.program_id(1)

    @pl.when(c == 0)
    def _():
        x1 = x1_ref[...]
        y_ref[...] = x1
        h_ref[...] = _rmsnorm_rows(x1, g2_ref[...]).astype(h_ref.dtype)

    acts = []
    for b in range(wg_ref.shape[1] // MXU_COLS):
        sl = slice(b * MXU_COLS, (b + 1) * MXU_COLS)
        a = _dot(h_ref[...], wg_ref[:, sl])
        u = _dot(h_ref[...], wu_ref[:, sl])
        acts.append((a / (1.0 + jnp.exp(-a)) * u).astype(BF16))
    upd = _dot(acts[0], wd_ref[0:MXU_COLS, :])
    for b in range(1, len(acts)):
        upd += _dot(acts[b], wd_ref[b * MXU_COLS:(b + 1) * MXU_COLS, :])
    y_ref[...] += upd

    if final_norm:
        @pl.when(c == pl.num_programs(1) - 1)
        def _():
            y_ref[...] = _rmsnorm_rows(y_ref[...], g_ref[...])


def _ffn(x1, g2, wg, wu, wd, gf, tm, tc, final_norm):
    n, d = x1.shape
    f = wg.shape[1]
    row = lambda i, c: (i, 0)
    vec = pl.BlockSpec((1, d), lambda i, c: (0, 0))
    return pl.pallas_call(
        functools.partial(_ffn_kernel, final_norm=final_norm),
        grid=(n // tm, f // tc),
        in_specs=[pl.BlockSpec((tm, d), row, pipeline_mode=pl.Buffered(1)), vec,
                  pl.BlockSpec((d, tc), lambda i, c: (0, c)),
                  pl.BlockSpec((d, tc), lambda i, c: (0, c)),
                  pl.BlockSpec((tc, d), lambda i, c: (c, 0)), vec],
        out_specs=pl.BlockSpec((tm, d), row),
        out_shape=jax.ShapeDtypeStruct((n, d), F32),
        scratch_shapes=[pltpu.VMEM((tm, d), BF16)],
        compiler_params=_cparams("parallel", "arbitrary"),
        name="ffn",
    )(x1, g2.reshape(1, d), wg, wu, wd, gf.reshape(1, d))


class _MainLayout:
    def __init__(self, d_model):
        wa = N_HEADS_A * HEAD_DIM_A
        wb = N_HEADS_R * VAL_DIM_R
        kr = N_HEADS_R * KEY_DIM_R
        qi = N_HEADS_IDX * HEAD_DIM_IDX
        order = [("qa", wa, "rope128", HEAD_DIM_A ** -0.5 * LOG2E), ("vr", wb, "plain", 1.0),
                 ("gr", wb, "silu", 1.0), ("ga", d_model, "sigmoid", 1.0), ("gb", d_model, "sigmoid", 1.0),
                 ("qi", qi, "rope64", 1.0), ("qr", kr, "rope128", 1.0),
                 ("kr", kr, "rope128", KEY_DIM_R ** -0.5)]
        self.off, self.width = {}, {}
        self.order = order
        o = 0
        for name, w, _, _ in order:
            self.off[name], self.width[name] = o, w
            o += w
        self.total = o

    def groups(self, tn):
        out = []
        for name, w, mode, scale in self.order:
            assert self.off[name] % tn == 0 and w % tn == 0
            out.append((self.off[name] // tn, (self.off[name] + w) // tn, mode, scale))
        return tuple(out)

    def block(self, name, width=None):
        width = width or self.width[name]
        assert self.off[name] % width == 0
        return self.off[name] // width


def _split_w_in(w, d_model):
    wa = N_HEADS_A * HEAD_DIM_A
    nk = N_KV_A * HEAD_DIM_A
    wb = N_HEADS_R * VAL_DIM_R
    kr = N_HEADS_R * KEY_DIM_R
    names = ("qa", "ka", "va", "qi", "ki", "wi", "qr", "kr", "vr", "gr", "ga", "gb")
    widths = (wa, nk, nk, N_HEADS_IDX * HEAD_DIM_IDX, HEAD_DIM_IDX, N_HEADS_IDX, kr, kr, wb, wb,
              d_model, d_model)
    parts, o = {}, 0
    for name, wd in zip(names, widths):
        parts[name] = w[:, o:o + wd]
        o += wd
    assert o == w.shape[1]
    return parts


def _tile(n, pref):
    t = min(n, pref)
    assert n % t == 0
    return t


def _dense_pre(x2, pos_rows, lw, lay):
    n = x2.shape[0]
    rope = _rope_tables(pos_rows)
    h, *kv = _proj_kv(x2, lw["g1"], lw["w_kv"], rope, _tile(n, 512))
    tn = 1024
    p_main = _proj_main(h, lw["w_main"], rope, lay.groups(tn), _tile(n, 1024), tn)
    return p_main, kv


def _dense_post(x2, o_a, o_b, p_main, lw, lay, gf, final_norm):
    n = x2.shape[0]
    tm = _tile(n, 1024)
    merged = _merge(o_a, o_b, lw["w_pa"], lw["w_pb"], p_main, lay.off["ga"], lay.off["gb"], tm, 512)
    x1 = _out_proj(x2, merged, lw["w_o"], _tile(n, 512))
    return _ffn(x1, lw["g2"], lw["wg"], lw["wu"], lw["wd"], gf, tm, 512, final_norm)


def _prompt_layer(xp, lw, lay, gf, final_norm):
    b, t, d = xp.shape
    x2 = xp.reshape(b * t, d)
    p_main, (ka, va, ki, wt, ka_bf, vt_bf, kia, kib) = _dense_pre(x2, jnp.arange(t, dtype=jnp.int32), lw, lay)
    top_k = min(TOPK_MAX, t // 4)
    tq, ck = _tile(t, 256), _tile(t, 512)
    mask_t = _index_mask(p_main, lay.block("qi"), wt, kia.reshape(b, t, LANES), kib.reshape(b, t, LANES),
                         tq=tq, ck=ck, l_valid=t, pos_base=0, causal=True, top_k=top_k)
    nk = N_KV_A * HEAD_DIM_A
    o_a = _attention(p_main, ka_bf.reshape(b, t, nk), vt_bf, mask_t, tq=tq, tk=ck, causal=True, kvb=2)
    state0 = jnp.zeros((b, N_HEADS_R, KEY_DIM_R, VAL_DIM_R), F32)
    cols = (lay.block("qr"), lay.block("kr"), lay.block("vr"), lay.block("gr"))
    o_b, st = _retention(p_main, cols, state0, _tile(t, 256))
    y = _dense_post(x2, o_a, o_b, p_main, lw, lay, gf, final_norm)
    return (y.reshape(b, t, d), ka.reshape(b, t, N_KV_A, HEAD_DIM_A), va.reshape(b, t, N_KV_A, HEAD_DIM_A),
            ki.reshape(b, t, HEAD_DIM_IDX), st)


def _sample_layer(xs, cache_k, cache_v, cache_i, state, lw, lay, gf, final_norm):
    b, t, d = xs.shape
    past = cache_k.shape[1]
    nk = N_KV_A * HEAD_DIM_A
    x2 = xs.reshape(b * t, d)
    pos = jnp.tile(past + jnp.arange(t, dtype=jnp.int32), b)
    p_main, (ka, va, ki, wt, ka_bf, vt_bf, kia, kib) = _dense_pre(x2, pos, lw, lay)

    tq = LANES
    l_valid = past + t
    tk = 3 * LANES
    lp = -(-l_valid // tk) * tk
    pq = jnp.pad(p_main.reshape(b, t, -1), ((0, 0), (0, tq - t), (0, 0)), mode="edge").reshape(b * tq, -1)
    wtq = jnp.pad(wt.reshape(N_HEADS_IDX, b, t), ((0, 0), (0, 0), (0, tq - t)),
                  mode="edge").reshape(N_HEADS_IDX, b * tq)
    kpad = ((0, 0), (0, lp - l_valid), (0, 0))
    ci = cache_i.astype(BF16)
    zi = jnp.zeros_like(ci)
    kia_all = jnp.pad(jnp.concatenate([jnp.concatenate([ci, zi], -1), kia.reshape(b, t, LANES)], 1), kpad)
    kib_all = jnp.pad(jnp.concatenate([jnp.concatenate([zi, ci], -1), kib.reshape(b, t, LANES)], 1), kpad)
    k_all = jnp.pad(jnp.concatenate([cache_k.reshape(b, past, nk).astype(BF16), ka_bf.reshape(b, t, nk)], 1), kpad)
    vt_new = vt_bf.reshape(N_KV_A, VT_ROWS, b, t)
    vt_cache = cache_v.reshape(b, past, N_KV_A, HEAD_DIM_A).transpose(2, 3, 0, 1).astype(BF16)
    ones_rows = jnp.zeros((N_KV_A, VT_ROWS - HEAD_DIM_A, b, past), BF16).at[:, 0].set(1.0)
    vt_all = jnp.concatenate([jnp.concatenate([vt_cache, ones_rows], 1), vt_new], 3)
    vt_all = jnp.pad(vt_all, ((0, 0), (0, 0), (0, 0), (0, lp - l_valid))).reshape(N_KV_A * VT_ROWS, b * lp)

    top_k = min(TOPK_MAX, l_valid // 4)
    mask_t = _index_mask(pq, lay.block("qi"), wtq, kia_all, kib_all, tq=tq, ck=tk, l_valid=l_valid,
                         pos_base=past, causal=False, top_k=top_k)
    o_a = _attention(pq, k_all, vt_all, mask_t, tq=tq, tk=tk, causal=False, kvb=1)
    o_a = o_a.reshape(b, tq, -1)[:, :t].reshape(b * t, -1)
    cols = (lay.block("qr"), lay.block("kr"), lay.block("vr"), lay.block("gr"))
    o_b, st = _retention(p_main, cols, state.astype(F32), t)
    y = _dense_post(x2, o_a, o_b, p_main, lw, lay, gf, final_norm)
    return (y.reshape(b, t, d), ka.reshape(b, t, N_KV_A, HEAD_DIM_A), va.reshape(b, t, N_KV_A, HEAD_DIM_A),
            ki.reshape(b, t, HEAD_DIM_IDX), st)


def kernel(x_prompt, x_sample, cache_k, cache_v, cache_idx_k, state_ret, norm1_g, w_in, w_pa, w_pb, w_o,
           norm2_g, w_ffn_gate, w_ffn_up, w_ffn_down, norm_f_g):
    assert HEAD_DIM_A == LANES and HEAD_DIM_IDX * 2 == LANES and N_HEADS_IDX % 2 == 0
    depth, d_model = norm1_g.shape
    lay = _MainLayout(d_model)
    xp, xs = x_prompt, x_sample
    outs = [[] for _ in range(8)]
    for l in range(depth):
        parts = _split_w_in(w_in[l], d_model)
        pad = jnp.zeros((d_model, LANES - HEAD_DIM_IDX - N_HEADS_IDX), F32)
        lw = {
            "g1": norm1_g[l], "g2": norm2_g[l],
            "w_main": jnp.concatenate([parts[name] for name, _, _, _ in lay.order], axis=1).astype(BF16),
            "w_kv": jnp.concatenate([parts["ka"], parts["va"], parts["ki"], parts["wi"], pad], axis=1).astype(BF16),
            "w_pa": w_pa[l].astype(BF16), "w_pb": w_pb[l].astype(BF16), "w_o": w_o[l].astype(BF16),
            "wg": w_ffn_gate[l].astype(BF16), "wu": w_ffn_up[l].astype(BF16), "wd": w_ffn_down[l].astype(BF16),
        }
        last = l == depth - 1
        xp, kp, vp, ip, sp = _prompt_layer(xp, lw, lay, norm_f_g, last)
        xs, ks, vs, isl, ss = _sample_layer(xs, cache_k[l], cache_v[l], cache_idx_k[l], state_ret[l],
                                            lw, lay, norm_f_g, last)
        for lst, val in zip(outs, (kp, vp, ip, sp, ks, vs, isl, ss)):
            lst.append(val)
    stacked = [jnp.stack(o) for o in outs]
    return (xp, xs, stacked[0], stacked[1], stacked[2], stacked[3].astype(x_prompt.dtype),
            stacked[4], stacked[5], stacked[6], stacked[7].astype(state_ret.dtype))
```

```python
import functools
import math

import jax
import jax.numpy as jnp
from jax import lax
from jax.experimental import pallas as pl
from jax.experimental.pallas import tpu as pltpu

F32 = jnp.float32
BF16 = jnp.bfloat16

CHUNK = 64
EPS = 1e-6
ROPE_THETA = 10000.0
N_HEADS_A = 16
N_KV_A = 4
HEAD_DIM_A = 128
N_HEADS_IDX = 16
HEAD_DIM_IDX = 64
TOPK_MAX = 256
N_HEADS_R = 8
KEY_DIM_R = 128
VAL_DIM_R = 256

LANES = 128
MXU_COLS = 256
V7X_VMEM_BYTES = 64 * 1024 * 1024
VMEM_LIMIT = V7X_VMEM_BYTES - 8 * 1024 * 1024
BF16_SUBLANES = 16
VT_ROWS = HEAD_DIM_A + BF16_SUBLANES
LOG2E = math.log2(math.e)
MAX_BISECT_ITERS = 300
NEG_BIG = float(jnp.finfo(jnp.float32).min)


def _cparams(*sem):
    return pltpu.CompilerParams(dimension_semantics=sem, vmem_limit_bytes=VMEM_LIMIT)


def _dot(a, b):
    return jnp.dot(a, b, preferred_element_type=F32)


def _dot_nt(a, b):
    return lax.dot_general(a, b, (((1,), (1,)), ((), ())), preferred_element_type=F32)


def _rmsnorm_rows(x, g):
    return x * lax.rsqrt(jnp.mean(x * x, axis=-1, keepdims=True) + EPS) * g


def _rope_tables(pos):
    posf = pos.astype(F32)[:, None]

    def cs(half):
        inv_freq = ROPE_THETA ** (-jnp.arange(half, dtype=F32) / half)
        ang = posf * inv_freq[None, :]
        return jnp.cos(ang), jnp.sin(ang)

    c, s = cs(HEAD_DIM_A // 2)
    c128 = jnp.concatenate([c, c], axis=1)
    s128 = jnp.concatenate([-s, s], axis=1)
    c, s = cs(HEAD_DIM_IDX // 2)
    z = jnp.zeros_like(s)
    c64 = jnp.tile(c, (1, 4))
    a64 = jnp.tile(jnp.concatenate([-s, z], axis=1), (1, 2))
    b64 = jnp.tile(jnp.concatenate([z, s], axis=1), (1, 2))
    return jnp.stack([c128, s128, c64, a64, b64])


def _rope128(x, rope_ref):
    return x * rope_ref[0] + pltpu.roll(x, 64, 1) * rope_ref[1]


def _rope64(x, rope_ref):
    return (x * rope_ref[2] + pltpu.roll(x, 96, 1) * rope_ref[3]
            + pltpu.roll(x, 32, 1) * rope_ref[4])


def _proj_main_kernel(h_ref, w_ref, rope_ref, o_ref, *, groups, tn):
    j = pl.program_id(1)
    for lo, hi, mode, scale in groups:
        @pl.when(jnp.logical_and(j >= lo, j < hi))
        def _(mode=mode, scale=scale):
            for b in range(tn // MXU_COLS):
                acc = _dot(h_ref[...], w_ref[:, b * MXU_COLS:(b + 1) * MXU_COLS])
                for c in range(MXU_COLS // LANES):
                    x = acc[:, c * LANES:(c + 1) * LANES]
                    if mode == "rope128":
                        y = _rope128(x, rope_ref)
                    elif mode == "rope64":
                        y = _rope64(x, rope_ref)
                    elif mode == "sigmoid":
                        y = 1.0 / (1.0 + jnp.exp(-x))
                    elif mode == "silu":
                        y = x / (1.0 + jnp.exp(-x))
                    else:
                        y = x
                    if scale != 1.0:
                        y = y * scale
                    col = b * MXU_COLS + c * LANES
                    o_ref[:, col:col + LANES] = y.astype(o_ref.dtype)


def _proj_main(h, w_main, rope, groups, tm, tn):
    n, d = h.shape
    p = w_main.shape[1]
    n_pos_tiles = rope.shape[1] // tm
    return pl.pallas_call(
        functools.partial(_proj_main_kernel, groups=groups, tn=tn),
        grid=(n // tm, p // tn),
        in_specs=[pl.BlockSpec((tm, d), lambda i, j: (i, 0)),
                  pl.BlockSpec((d, tn), lambda i, j: (0, j)),
                  pl.BlockSpec((5, tm, LANES), lambda i, j: (0, i % n_pos_tiles, 0))],
        out_specs=pl.BlockSpec((tm, tn), lambda i, j: (i, j)),
        out_shape=jax.ShapeDtypeStruct((n, p), BF16),
        compiler_params=_cparams("parallel", "arbitrary"),
        name="proj_main",
    )(h, w_main, rope)


def _proj_kv_kernel(x_ref, g_ref, w_ref, rope_ref, h_ref, ka_ref, va_ref, ki_ref, wt_ref,
                    kab_ref, vtb_ref, kia_ref, kib_ref, acc_ref, *, nk, idx_scale):
    h_ref[...] = _rmsnorm_rows(x_ref[...], g_ref[...]).astype(h_ref.dtype)
    acc_ref[...] = _dot(h_ref[...], w_ref[...])
    tm = acc_ref.shape[0]
    for c in range(N_KV_A):
        sl = slice(c * LANES, (c + 1) * LANES)
        y = _rope128(acc_ref[:, sl], rope_ref)
        ka_ref[pl.ds(c, tm, stride=N_KV_A), :] = y
        kab_ref[:, sl] = y.astype(BF16)
        va_ref[pl.ds(c, tm, stride=N_KV_A), :] = acc_ref[:, nk + c * LANES:nk + (c + 1) * LANES]
    ones_rows = jnp.where(lax.broadcasted_iota(jnp.int32, (VT_ROWS - HEAD_DIM_A, tm), 0) == 0, 1.0, 0.0)
    for g in range(N_KV_A):
        vg = acc_ref[:, nk + g * HEAD_DIM_A:nk + (g + 1) * HEAD_DIM_A]
        vtb_ref[g * VT_ROWS:g * VT_ROWS + HEAD_DIM_A, :] = vg.T.astype(BF16)
        vtb_ref[g * VT_ROWS + HEAD_DIM_A:(g + 1) * VT_ROWS, :] = ones_rows.astype(BF16)
    z = acc_ref[:, 2 * nk:2 * nk + LANES]
    y = _rope64(z, rope_ref)
    ki_ref[...] = y[:, :HEAD_DIM_IDX]
    lane = lax.broadcasted_iota(jnp.int32, y.shape, 1)
    ya = jnp.where(lane < HEAD_DIM_IDX, y, 0.0)
    kia_ref[...] = ya.astype(BF16)
    kib_ref[...] = pltpu.roll(ya, HEAD_DIM_IDX, 1).astype(BF16)
    wt_ref[...] = (z * idx_scale).T[HEAD_DIM_IDX:HEAD_DIM_IDX + N_HEADS_IDX, :]


def _proj_kv(x, g1, w_kv, rope, tm):
    n, d = x.shape
    nk = N_KV_A * HEAD_DIM_A
    pw = w_kv.shape[1]
    n_pos_tiles = rope.shape[1] // tm
    idx_scale = (HEAD_DIM_IDX ** -0.5) * (N_HEADS_IDX ** -0.5)
    row = lambda i: (i, 0)
    return pl.pallas_call(
        functools.partial(_proj_kv_kernel, nk=nk, idx_scale=idx_scale),
        grid=(n // tm,),
        in_specs=[pl.BlockSpec((tm, d), row),
                  pl.BlockSpec((1, d), lambda i: (0, 0)),
                  pl.BlockSpec((d, pw), lambda i: (0, 0)),
                  pl.BlockSpec((5, tm, LANES), lambda i: (0, i % n_pos_tiles, 0))],
        out_specs=[pl.BlockSpec((tm, d), row),
                   pl.BlockSpec((tm * N_KV_A, HEAD_DIM_A), row), pl.BlockSpec((tm * N_KV_A, HEAD_DIM_A), row),
                   pl.BlockSpec((tm, HEAD_DIM_IDX), row),
                   pl.BlockSpec((N_HEADS_IDX, tm), lambda i: (0, i)),
                   pl.BlockSpec((tm, nk), row),
                   pl.BlockSpec((N_KV_A * VT_ROWS, tm), lambda i: (0, i)),
                   pl.BlockSpec((tm, LANES), row), pl.BlockSpec((tm, LANES), row)],
        out_shape=[jax.ShapeDtypeStruct((n, d), BF16),
                   jax.ShapeDtypeStruct((n * N_KV_A, HEAD_DIM_A), F32),
                   jax.ShapeDtypeStruct((n * N_KV_A, HEAD_DIM_A), F32),
                   jax.ShapeDtypeStruct((n, HEAD_DIM_IDX), F32),
                   jax.ShapeDtypeStruct((N_HEADS_IDX, n), F32),
                   jax.ShapeDtypeStruct((n, nk), BF16),
                   jax.ShapeDtypeStruct((N_KV_A * VT_ROWS, n), BF16),
                   jax.ShapeDtypeStruct((n, LANES), BF16), jax.ShapeDtypeStruct((n, LANES), BF16)],
        scratch_shapes=[pltpu.VMEM((tm, pw), F32)],
        compiler_params=_cparams("parallel"),
        name="proj_kv",
    )(x, g1.reshape(1, d), w_kv, rope)


def _fori_pairs(n, body, init):
    def two(j, carry):
        return body(2 * j + 1, body(2 * j, carry))
    carry = lax.fori_loop(0, n // 2, two, init)
    return lax.cond(n % 2 == 1, lambda c: body(n - 1, c), lambda c: c, carry)


def _index_kernel(qi_ref, wt_ref, kia_ref, kib_ref, m_ref, s_ref, *,
                  tq, ck, n_ck_total, l_valid, pos_base, causal, top_k):
    i = pl.program_id(1)
    if causal:
        n_c = (i * tq + tq + ck - 1) // ck
    else:
        n_c = n_ck_total
    qpos = pos_base + i * tq + lax.broadcasted_iota(jnp.int32, (1, tq), 1)
    lim = jnp.minimum(qpos - lax.rem(qpos, CHUNK) + CHUNK, l_valid)
    limf = lim.astype(F32)
    kprime = jnp.minimum(float(top_k), limf)
    inf = jnp.float32(jnp.inf)

    def key_index(off):
        return off + lax.broadcasted_iota(jnp.int32, (ck, tq), 0)

    def score_chunk(c, carry):
        rmax, rmin = carry
        off = pl.multiple_of(c * ck, ck)
        ka = kia_ref[pl.ds(off, ck), :]
        kb = kib_ref[pl.ds(off, ck), :]
        acc = jnp.zeros((ck, tq), F32)
        for p in range(N_HEADS_IDX // 2):
            qp = qi_ref[:, p * LANES:(p + 1) * LANES]
            acc += jnp.maximum(_dot_nt(ka, qp), 0.0) * wt_ref[2 * p:2 * p + 1, :]
            acc += jnp.maximum(_dot_nt(kb, qp), 0.0) * wt_ref[2 * p + 1:2 * p + 2, :]
        adm = key_index(off) < lim
        s_ref[pl.ds(off, ck), :] = jnp.where(adm, acc, -inf)
        rmax = jnp.maximum(rmax, jnp.max(jnp.where(adm, acc, -inf), axis=0, keepdims=True))
        rmin = jnp.minimum(rmin, jnp.min(jnp.where(adm, acc, inf), axis=0, keepdims=True))
        return rmax, rmin

    rmax, rmin = _fori_pairs(n_c, score_chunk, (jnp.full((1, tq), -inf, F32), jnp.full((1, tq), inf, F32)))

    fold_rows = min(ck, 64)

    def fold(ind):
        return ind.reshape(ck // fold_rows, fold_rows, tq).sum(axis=0)

    def count_ge(th):
        def body(c, acc):
            off = pl.multiple_of(c * ck, ck)
            return acc + fold(jnp.where(s_ref[pl.ds(off, ck), :] >= th, 1.0, 0.0))
        acc = lax.fori_loop(0, n_c, body, jnp.zeros((fold_rows, tq), F32))
        return acc.sum(axis=0, keepdims=True)

    def cond(st):
        it, lo, hi, cnt, stuck = st
        active = jnp.where(jnp.logical_and(cnt > kprime, stuck < 0.5), 1.0, 0.0)
        return jnp.logical_and(it < MAX_BISECT_ITERS, jnp.max(active) > 0.5)

    def body(st):
        it, lo, hi, cnt, stuck = st
        open_top = hi == inf
        mid = lo + 0.5 * (jnp.where(open_top, rmax, hi) - lo)
        mid = jnp.where(jnp.logical_and(open_top, mid <= lo), rmax, mid)
        c = count_ge(mid)
        ge = c >= kprime
        no_progress = jnp.logical_or(mid <= lo, mid >= hi)
        return (it + 1, jnp.where(ge, mid, lo), jnp.where(ge, hi, mid), jnp.where(ge, c, cnt),
                jnp.where(no_progress, 1.0, stuck))

    _, lo, hi, cnt, _ = lax.while_loop(
        cond, body, (jnp.int32(0), rmin, jnp.full((1, tq), inf, F32), limf, jnp.zeros((1, tq), F32)))

    unresolved = jnp.max(jnp.where(cnt > kprime, 1.0, 0.0)) > 0.5

    @pl.when(jnp.logical_not(unresolved))
    def _():
        def body(c, _):
            off = pl.multiple_of(c * ck, ck)
            m_ref[pl.ds(off, ck), :] = jnp.where(s_ref[pl.ds(off, ck), :] >= lo, 0.0, -inf).astype(m_ref.dtype)
            return 0
        lax.fori_loop(0, n_c, body, 0)

    @pl.when(unresolved)
    def _():
        need = kprime - count_ge(hi)

        def in_tie(blk):
            return jnp.logical_and(blk >= lo, blk < hi)

        def count_tie_below(jcut):
            def body(c, acc):
                off = pl.multiple_of(c * ck, ck)
                blk = s_ref[pl.ds(off, ck), :]
                e = jnp.logical_and(in_tie(blk), key_index(off).astype(F32) < jcut)
                return acc + fold(jnp.where(e, 1.0, 0.0))
            acc = lax.fori_loop(0, n_c, body, jnp.zeros((fold_rows, tq), F32))
            return acc.sum(axis=0, keepdims=True)

        def jbody(_, st):
            jlo, jhi = st
            mid = jnp.floor(0.5 * (jlo + jhi))
            ok = count_tie_below(mid) >= need
            return jnp.where(ok, jlo, mid + 1.0), jnp.where(ok, mid, jhi)

        n_total = n_ck_total * ck
        _, jcut = lax.fori_loop(0, int(math.ceil(math.log2(n_total + 1))), jbody,
                                (jnp.zeros((1, tq), F32), jnp.full((1, tq), float(n_total), F32)))

        def body(c, _):
            off = pl.multiple_of(c * ck, ck)
            blk = s_ref[pl.ds(off, ck), :]
            e = jnp.logical_and(in_tie(blk), key_index(off).astype(F32) < jcut)
            keep = jnp.logical_or(blk >= hi, e)
            m_ref[pl.ds(off, ck), :] = jnp.where(keep, 0.0, -inf).astype(m_ref.dtype)
            return 0
        lax.fori_loop(0, n_c, body, 0)

    def zero_body(c, _):
        off = pl.multiple_of(c * ck, ck)
        m_ref[pl.ds(off, ck), :] = jnp.full((ck, tq), -inf, m_ref.dtype)
        return 0
    lax.fori_loop(n_c, n_ck_total, zero_body, 0)


def _index_mask(p_main, qi_col_block, wt, kia, kib, *, tq, ck, l_valid, pos_base, causal, top_k):
    b, lp, _ = kia.shape
    nq = p_main.shape[0] // (b * tq)
    qw = N_HEADS_IDX * HEAD_DIM_IDX
    return pl.pallas_call(
        functools.partial(_index_kernel, tq=tq, ck=ck, n_ck_total=lp // ck, l_valid=l_valid,
                          pos_base=pos_base, causal=causal, top_k=top_k),
        grid=(b, nq),
        in_specs=[pl.BlockSpec((tq, qw), lambda bi, i: (bi * nq + i, qi_col_block)),
                  pl.BlockSpec((N_HEADS_IDX, tq), lambda bi, i: (0, bi * nq + i)),
                  pl.BlockSpec((None, lp, LANES), lambda bi, i: (bi, 0, 0)),
                  pl.BlockSpec((None, lp, LANES), lambda bi, i: (bi, 0, 0))],
        out_specs=pl.BlockSpec((None, lp, tq), lambda bi, i: (bi, 0, i)),
        out_shape=jax.ShapeDtypeStruct((b, lp, nq * tq), BF16),
        scratch_shapes=[pltpu.VMEM((lp, tq), F32)],
        compiler_params=_cparams("parallel", "parallel"),
        name="index_mask",
    )(p_main, wt, kia, kib)


def _attn_kernel(q_ref, k_ref, vt_ref, m_ref, o_ref, acc_ref, sa_ref, sb_ref, ma_ref, mb_ref, *,
                 tq, tk, n_kt_total, causal, group, kvb):
    i = pl.program_id(1)
    heads = kvb * group
    if causal:
        n_kt = (i * tq + tq + tk - 1) // tk
    else:
        n_kt = n_kt_total
    acc_ref[...] = jnp.zeros_like(acc_ref)

    def scores(kt, bufs):
        s_ref, mx_ref = bufs
        off = pl.multiple_of(kt * tk, tk)
        bias = m_ref[pl.ds(off, tk), :].astype(F32)
        for h in range(heads):
            kv = h // group
            k = k_ref[pl.ds(off, tk), kv * HEAD_DIM_A:(kv + 1) * HEAD_DIM_A]
            s = _dot_nt(k, q_ref[:, h * HEAD_DIM_A:(h + 1) * HEAD_DIM_A]) + bias
            s_ref[h] = s
            mx_ref[h] = jnp.max(s, axis=0, keepdims=True)

    def softmax_pv(kt, bufs, ms):
        s_ref, mx_ref = bufs
        off = pl.multiple_of(kt * tk, tk)
        out = []
        for h in range(heads):
            kv = h // group
            vt = vt_ref[kv * VT_ROWS:(kv + 1) * VT_ROWS, pl.ds(off, tk)]
            m_new = jnp.maximum(ms[h], mx_ref[h])
            alpha = jnp.exp2(ms[h] - m_new)
            p = jnp.exp2(s_ref[h] - m_new).astype(BF16)
            acc_ref[h] = acc_ref[h] * alpha + _dot(vt, p)
            out.append(m_new)
        return tuple(out)

    buf_a, buf_b = (sa_ref, ma_ref), (sb_ref, mb_ref)

    def body(j, ms):
        scores(2 * j + 1, buf_b)
        ms = softmax_pv(2 * j, buf_a, ms)
        scores(2 * j + 2, buf_a)
        return softmax_pv(2 * j + 1, buf_b, ms)

    scores(0, buf_a)
    n_pairs = (n_kt - 1) // 2
    ms = lax.fori_loop(0, n_pairs, body, tuple(jnp.full((1, tq), NEG_BIG, F32) for _ in range(heads)))
    even = n_kt - 1 > 2 * n_pairs

    @pl.when(even)
    def _():
        scores(2 * n_pairs + 1, buf_b)
        softmax_pv(2 * n_pairs + 1, buf_b, softmax_pv(2 * n_pairs, buf_a, ms))

    @pl.when(jnp.logical_not(even))
    def _():
        softmax_pv(2 * n_pairs, buf_a, ms)

    for h in range(heads):
        a = acc_ref[h]
        o = a[:HEAD_DIM_A] / a[HEAD_DIM_A:HEAD_DIM_A + 1]
        o_ref[:, h * HEAD_DIM_A:(h + 1) * HEAD_DIM_A] = o.T.astype(o_ref.dtype)


def _attention(p_main, k_bf, vt_bf, mask_t, *, tq, tk, causal, kvb):
    b, lp, _ = k_bf.shape
    nq = mask_t.shape[2] // tq
    group = N_HEADS_A // N_KV_A
    heads = kvb * group
    gw = heads * HEAD_DIM_A
    return pl.pallas_call(
        functools.partial(_attn_kernel, tq=tq, tk=tk, n_kt_total=lp // tk, causal=causal, group=group,
                          kvb=kvb),
        grid=(b, nq, N_KV_A // kvb),
        in_specs=[pl.BlockSpec((tq, gw), lambda bi, i, g: (bi * nq + i, g)),
                  pl.BlockSpec((None, lp, kvb * HEAD_DIM_A), lambda bi, i, g: (bi, 0, g)),
                  pl.BlockSpec((kvb * VT_ROWS, lp), lambda bi, i, g: (g, bi)),
                  pl.BlockSpec((None, lp, tq), lambda bi, i, g: (bi, 0, i))],
        out_specs=pl.BlockSpec((tq, gw), lambda bi, i, g: (bi * nq + i, g)),
        out_shape=jax.ShapeDtypeStruct((b * nq * tq, N_HEADS_A * HEAD_DIM_A), BF16),
        scratch_shapes=[pltpu.VMEM((heads, VT_ROWS, tq), F32),
                        pltpu.VMEM((heads, tk, tq), F32), pltpu.VMEM((heads, tk, tq), F32),
                        pltpu.VMEM((heads, 1, tq), F32), pltpu.VMEM((heads, 1, tq), F32)],
        compiler_params=_cparams("parallel", "parallel", "arbitrary"),
        name="attention",
    )(p_main, k_bf, vt_bf, mask_t)


def _retention_tables(c):
    log_gamma = jnp.log1p(-(2.0 ** (-5.0 - jnp.arange(N_HEADS_R, dtype=F32))))
    idx = jnp.arange(c, dtype=F32)
    diff = idx[:, None] - idx[None, :]
    decay = jnp.where(diff[None] >= 0,
                      jnp.exp(jnp.maximum(diff, 0.0)[None] * log_gamma[:, None, None]), 0.0)
    cross = jnp.exp((idx + 1.0)[None, :] * log_gamma[:, None])
    kdec = jnp.exp((c - 1.0 - idx)[None, :] * log_gamma[:, None])
    full = jnp.exp(c * log_gamma)
    bc = lambda a, w: jnp.broadcast_to(a[..., None], a.shape + (w,))
    return decay, bc(cross, KEY_DIM_R), bc(kdec, KEY_DIM_R), bc(full[:, None], VAL_DIM_R)


def _retention_kernel(q_ref, k_ref, v_ref, g_ref, dec_ref, qsc_ref, ksc_ref, gc_ref, s0_ref,
                      o_ref, st_ref):
    @pl.when(pl.program_id(1) == 0)
    def _():
        st_ref[...] = s0_ref[...]

    for h in range(N_HEADS_R):
        ks = slice(h * KEY_DIM_R, (h + 1) * KEY_DIM_R)
        vs = slice(h * VAL_DIM_R, (h + 1) * VAL_DIM_R)
        q = q_ref[:, ks]
        k = k_ref[:, ks]
        v = v_ref[:, vs]
        st = st_ref[h]
        inner = _dot_nt(q, k) * dec_ref[h]
        qd = (q.astype(F32) * qsc_ref[h]).astype(BF16)
        o = _dot(inner.astype(BF16), v) + _dot(qd, st.astype(BF16))
        kdt = (k.astype(F32) * ksc_ref[h]).T.astype(BF16)
        st_ref[h] = st * gc_ref[h] + _dot(kdt, v)
        o = o * lax.rsqrt(jnp.mean(o * o, axis=-1, keepdims=True) + EPS)
        o_ref[:, vs] = (o * g_ref[:, vs].astype(F32)).astype(o_ref.dtype)


def _retention(p_main, cols, state0, c):
    b = state0.shape[0]
    nc = p_main.shape[0] // (b * c)
    kw = N_HEADS_R * KEY_DIM_R
    vw = N_HEADS_R * VAL_DIM_R
    dec, qsc, ksc, gcs = _retention_tables(c)
    qb, kb, vb, gb = cols
    const3 = lambda bi, ci: (0, 0, 0)
    st_spec = pl.BlockSpec((None, N_HEADS_R, KEY_DIM_R, VAL_DIM_R), lambda bi, ci: (bi, 0, 0, 0))
    return pl.pallas_call(
        _retention_kernel,
        grid=(b, nc),
        in_specs=[pl.BlockSpec((c, kw), lambda bi, ci: (bi * nc + ci, qb)),
                  pl.BlockSpec((c, kw), lambda bi, ci: (bi * nc + ci, kb)),
                  pl.BlockSpec((c, vw), lambda bi, ci: (bi * nc + ci, vb)),
                  pl.BlockSpec((c, vw), lambda bi, ci: (bi * nc + ci, gb)),
                  pl.BlockSpec(dec.shape, const3), pl.BlockSpec(qsc.shape, const3),
                  pl.BlockSpec(ksc.shape, const3), pl.BlockSpec(gcs.shape, const3),
                  st_spec],
        out_specs=[pl.BlockSpec((c, vw), lambda bi, ci: (bi * nc + ci, 0)), st_spec],
        out_shape=[jax.ShapeDtypeStruct((b * nc * c, vw), BF16),
                   jax.ShapeDtypeStruct(state0.shape, F32)],
        compiler_params=_cparams("parallel", "arbitrary"),
        name="retention",
    )(p_main, p_main, p_main, p_main, dec, qsc, ksc, gcs, state0)


def _merge_kernel(oa_ref, ob_ref, wa_ref, wb_ref, sa_ref, sb_ref, o_ref):
    for b in range(o_ref.shape[1] // MXU_COLS):
        sl = slice(b * MXU_COLS, (b + 1) * MXU_COLS)
        m = (_dot(oa_ref[...], wa_ref[:, sl]) * sa_ref[:, sl].astype(F32)
             + _dot(ob_ref[...], wb_ref[:, sl]) * sb_ref[:, sl].astype(F32))
        o_ref[:, sl] = m.astype(o_ref.dtype)


def _merge(o_a, o_b, w_pa, w_pb, p_main, ga_off, gb_off, tm, tn):
    n, wa = o_a.shape
    wb = o_b.shape[1]
    d = w_pa.shape[1]
    return pl.pallas_call(
        _merge_kernel,
        grid=(n // tm, d // tn),
        in_specs=[pl.BlockSpec((tm, wa), lambda i, j: (i, 0)),
                  pl.BlockSpec((tm, wb), lambda i, j: (i, 0)),
                  pl.BlockSpec((wa, tn), lambda i, j: (0, j)),
                  pl.BlockSpec((wb, tn), lambda i, j: (0, j)),
                  pl.BlockSpec((tm, tn), lambda i, j: (i, ga_off // tn + j)),
                  pl.BlockSpec((tm, tn), lambda i, j: (i, gb_off // tn + j))],
        out_specs=pl.BlockSpec((tm, tn), lambda i, j: (i, j)),
        out_shape=jax.ShapeDtypeStruct((n, d), BF16),
        compiler_params=_cparams("parallel", "arbitrary"),
        name="merge",
    )(o_a, o_b, w_pa, w_pb, p_main, p_main)


def _out_proj_kernel(x_ref, m_ref, w_ref, x1_ref):
    x1_ref[...] = x_ref[...] + _dot(m_ref[...], w_ref[...])


def _out_proj(x, merged, w_o, tm):
    n, d = x.shape
    row = lambda i: (i, 0)
    return pl.pallas_call(
        _out_proj_kernel,
        grid=(n // tm,),
        in_specs=[pl.BlockSpec((tm, d), row), pl.BlockSpec((tm, d), row),
                  pl.BlockSpec((d, d), lambda i: (0, 0))],
        out_specs=pl.BlockSpec((tm, d), row),
        out_shape=jax.ShapeDtypeStruct((n, d), F32),
        compiler_params=_cparams("parallel"),
        name="out_proj",
    )(x, merged, w_o)


def _ffn_kernel(x1_ref, g2_ref, wg_ref, wu_ref, wd_ref, g_ref, y_ref, h_ref, *, final_norm):
    c = pl.program_id(1)

    @pl.when(c == 0)
    def _():
        x1 = x1_ref[...]
        y_ref[...] = x1
        h_ref[...] = _rmsnorm_rows(x1, g2_ref[...]).astype(h_ref.dtype)

    acts = []
    for b in range(wg_ref.shape[1] // MXU_COLS):
        sl = slice(b * MXU_COLS, (b + 1) * MXU_COLS)
        a = _dot(h_ref[...], wg_ref[:, sl])
        u = _dot(h_ref[...], wu_ref[:, sl])
        acts.append((a / (1.0 + jnp.exp(-a)) * u).astype(BF16))
    upd = _dot(acts[0], wd_ref[0:MXU_COLS, :])
    for b in range(1, len(acts)):
        upd += _dot(acts[b], wd_ref[b * MXU_COLS:(b + 1) * MXU_COLS, :])
    y_ref[...] += upd

    if final_norm:
        @pl.when(c == pl.num_programs(1) - 1)
        def _():
            y_ref[...] = _rmsnorm_rows(y_ref[...], g_ref[...])


def _ffn(x1, g2, wg, wu, wd, gf, tm, tc, final_norm):
    n, d = x1.shape
    f = wg.shape[1]
    row = lambda i, c: (i, 0)
    vec = pl.BlockSpec((1, d), lambda i, c: (0, 0))
    return pl.pallas_call(
        functools.partial(_ffn_kernel, final_norm=final_norm),
        grid=(n // tm, f // tc),
        in_specs=[pl.BlockSpec((tm, d), row, pipeline_mode=pl.Buffered(1)), vec,
                  pl.BlockSpec((d, tc), lambda i, c: (0, c)),
                  pl.BlockSpec((d, tc), lambda i, c: (0, c)),
                  pl.BlockSpec((tc, d), lambda i, c: (c, 0)), vec],
        out_specs=pl.BlockSpec((tm, d), row),
        out_shape=jax.ShapeDtypeStruct((n, d), F32),
        scratch_shapes=[pltpu.VMEM((tm, d), BF16)],
        compiler_params=_cparams("parallel", "arbitrary"),
        name="ffn",
    )(x1, g2.reshape(1, d), wg, wu, wd, gf.reshape(1, d))


class _MainLayout:
    def __init__(self, d_model):
        wa = N_HEADS_A * HEAD_DIM_A
        wb = N_HEADS_R * VAL_DIM_R
        kr = N_HEADS_R * KEY_DIM_R
        qi = N_HEADS_IDX * HEAD_DIM_IDX
        order = [("qa", wa, "rope128", HEAD_DIM_A ** -0.5 * LOG2E), ("vr", wb, "plain", 1.0),
                 ("gr", wb, "silu", 1.0), ("ga", d_model, "sigmoid", 1.0), ("gb", d_model, "sigmoid", 1.0),
                 ("qi", qi, "rope64", 1.0), ("qr", kr, "rope128", 1.0),
                 ("kr", kr, "rope128", KEY_DIM_R ** -0.5)]
        self.off, self.width = {}, {}
        self.order = order
        o = 0
        for name, w, _, _ in order:
            self.off[name], self.width[name] = o, w
            o += w
        self.total = o

    def groups(self, tn):
        out = []
        for name, w, mode, scale in self.order:
            assert self.off[name] % tn == 0 and w % tn == 0
            out.append((self.off[name] // tn, (self.off[name] + w) // tn, mode, scale))
        return tuple(out)

    def block(self, name, width=None):
        width = width or self.width[name]
        assert self.off[name] % width == 0
        return self.off[name] // width


def _split_w_in(w, d_model):
    wa = N_HEADS_A * HEAD_DIM_A
    nk = N_KV_A * HEAD_DIM_A
    wb = N_HEADS_R * VAL_DIM_R
    kr = N_HEADS_R * KEY_DIM_R
    names = ("qa", "ka", "va", "qi", "ki", "wi", "qr", "kr", "vr", "gr", "ga", "gb")
    widths = (wa, nk, nk, N_HEADS_IDX * HEAD_DIM_IDX, HEAD_DIM_IDX, N_HEADS_IDX, kr, kr, wb, wb,
              d_model, d_model)
    parts, o = {}, 0
    for name, wd in zip(names, widths):
        parts[name] = w[:, o:o + wd]
        o += wd
    assert o == w.shape[1]
    return parts


def _tile(n, pref):
    t = min(n, pref)
    assert n % t == 0
    return t


def _dense_pre(x2, pos_rows, lw, lay):
    n = x2.shape[0]
    rope = _rope_tables(pos_rows)
    h, *kv = _proj_kv(x2, lw["g1"], lw["w_kv"], rope, _tile(n, 512))
    tn = 1024
    p_main = _proj_main(h, lw["w_main"], rope, lay.groups(tn), _tile(n, 1024), tn)
    return p_main, kv


def _dense_post(x2, o_a, o_b, p_main, lw, lay, gf, final_norm):
    n = x2.shape[0]
    tm = _tile(n, 1024)
    merged = _merge(o_a, o_b, lw["w_pa"], lw["w_pb"], p_main, lay.off["ga"], lay.off["gb"], tm, 512)
    x1 = _out_proj(x2, merged, lw["w_o"], _tile(n, 512))
    return _ffn(x1, lw["g2"], lw["wg"], lw["wu"], lw["wd"], gf, tm, 512, final_norm)


def _prompt_layer(xp, lw, lay, gf, final_norm):
    b, t, d = xp.shape
    x2 = xp.reshape(b * t, d)
    p_main, (ka, va, ki, wt, ka_bf, vt_bf, kia, kib) = _dense_pre(x2, jnp.arange(t, dtype=jnp.int32), lw, lay)
    top_k = min(TOPK_MAX, t // 4)
    tq, ck = _tile(t, 256), _tile(t, 512)
    mask_t = _index_mask(p_main, lay.block("qi"), wt, kia.reshape(b, t, LANES), kib.reshape(b, t, LANES),
                         tq=tq, ck=ck, l_valid=t, pos_base=0, causal=True, top_k=top_k)
    nk = N_KV_A * HEAD_DIM_A
    o_a = _attention(p_main, ka_bf.reshape(b, t, nk), vt_bf, mask_t, tq=tq, tk=ck, causal=True, kvb=2)
    state0 = jnp.zeros((b, N_HEADS_R, KEY_DIM_R, VAL_DIM_R), F32)
    cols = (lay.block("qr"), lay.block("kr"), lay.block("vr"), lay.block("gr"))
    o_b, st = _retention(p_main, cols, state0, _tile(t, 256))
    y = _dense_post(x2, o_a, o_b, p_main, lw, lay, gf, final_norm)
    return (y.reshape(b, t, d), ka.reshape(b, t, N_KV_A, HEAD_DIM_A), va.reshape(b, t, N_KV_A, HEAD_DIM_A),
            ki.reshape(b, t, HEAD_DIM_IDX), st)


def _sample_layer(xs, cache_k, cache_v, cache_i, state, lw, lay, gf, final_norm):
    b, t, d = xs.shape
    past = cache_k.shape[1]
    nk = N_KV_A * HEAD_DIM_A
    x2 = xs.reshape(b * t, d)
    pos = jnp.tile(past + jnp.arange(t, dtype=jnp.int32), b)
    p_main, (ka, va, ki, wt, ka_bf, vt_bf, kia, kib) = _dense_pre(x2, pos, lw, lay)

    tq = LANES
    l_valid = past + t
    tk = 3 * LANES
    lp = -(-l_valid // tk) * tk
    pq = jnp.pad(p_main.reshape(b, t, -1), ((0, 0), (0, tq - t), (0, 0)), mode="edge").reshape(b * tq, -1)
    wtq = jnp.pad(wt.reshape(N_HEADS_IDX, b, t), ((0, 0), (0, 0), (0, tq - t)),
                  mode="edge").reshape(N_HEADS_IDX, b * tq)
    kpad = ((0, 0), (0, lp - l_valid), (0, 0))
    ci = cache_i.astype(BF16)
    zi = jnp.zeros_like(ci)
    kia_all = jnp.pad(jnp.concatenate([jnp.concatenate([ci, zi], -1), kia.reshape(b, t, LANES)], 1), kpad)
    kib_all = jnp.pad(jnp.concatenate([jnp.concatenate([zi, ci], -1), kib.reshape(b, t, LANES)], 1), kpad)
    k_all = jnp.pad(jnp.concatenate([cache_k.reshape(b, past, nk).astype(BF16), ka_bf.reshape(b, t, nk)], 1), kpad)
    vt_new = vt_bf.reshape(N_KV_A, VT_ROWS, b, t)
    vt_cache = cache_v.reshape(b, past, N_KV_A, HEAD_DIM_A).transpose(2, 3, 0, 1).astype(BF16)
    ones_rows = jnp.zeros((N_KV_A, VT_ROWS - HEAD_DIM_A, b, past), BF16).at[:, 0].set(1.0)
    vt_all = jnp.concatenate([jnp.concatenate([vt_cache, ones_rows], 1), vt_new], 3)
    vt_all = jnp.pad(vt_all, ((0, 0), (0, 0), (0, 0), (0, lp - l_valid))).reshape(N_KV_A * VT_ROWS, b * lp)

    top_k = min(TOPK_MAX, l_valid // 4)
    mask_t = _index_mask(pq, lay.block("qi"), wtq, kia_all, kib_all, tq=tq, ck=tk, l_valid=l_valid,
                         pos_base=past, causal=False, top_k=top_k)
    o_a = _attention(pq, k_all, vt_all, mask_t, tq=tq, tk=tk, causal=False, kvb=1)
    o_a = o_a.reshape(b, tq, -1)[:, :t].reshape(b * t, -1)
    cols = (lay.block("qr"), lay.block("kr"), lay.block("vr"), lay.block("gr"))
    o_b, st = _retention(p_main, cols, state.astype(F32), t)
    y = _dense_post(x2, o_a, o_b, p_main, lw, lay, gf, final_norm)
    return (y.reshape(b, t, d), ka.reshape(b, t, N_KV_A, HEAD_DIM_A), va.reshape(b, t, N_KV_A, HEAD_DIM_A),
            ki.reshape(b, t, HEAD_DIM_IDX), st)


def kernel(x_prompt, x_sample, cache_k, cache_v, cache_idx_k, state_ret, norm1_g, w_in, w_pa, w_pb, w_o,
           norm2_g, w_ffn_gate, w_ffn_up, w_ffn_down, norm_f_g):
    assert HEAD_DIM_A == LANES and HEAD_DIM_IDX * 2 == LANES and N_HEADS_IDX % 2 == 0
    depth, d_model = norm1_g.shape
    lay = _MainLayout(d_model)
    xp, xs = x_prompt, x_sample
    outs = [[] for _ in range(8)]
    for l in range(depth):
        parts = _split_w_in(w_in[l], d_model)
        pad = jnp.zeros((d_model, LANES - HEAD_DIM_IDX - N_HEADS_IDX), F32)
        lw = {
            "g1": norm1_g[l], "g2": norm2_g[l],
            "w_main": jnp.concatenate([parts[name] for name, _, _, _ in lay.order], axis=1).astype(BF16),
            "w_kv": jnp.concatenate([parts["ka"], parts["va"], parts["ki"], parts["wi"], pad], axis=1).astype(BF16),
            "w_pa": w_pa[l].astype(BF16), "w_pb": w_pb[l].astype(BF16), "w_o": w_o[l].astype(BF16),
            "wg": w_ffn_gate[l].astype(BF16), "wu": w_ffn_up[l].astype(BF16), "wd": w_ffn_down[l].astype(BF16),
        }
        last = l == depth - 1
        xp, kp, vp, ip, sp = _prompt_layer(xp, lw, lay, norm_f_g, last)
        xs, ks, vs, isl, ss = _sample_layer(xs, cache_k[l], cache_v[l], cache_idx_k[l], state_ret[l],
                                            lw, lay, norm_f_g, last)
        for lst, val in zip(outs, (kp, vp, ip, sp, ks, vs, isl, ss)):
            lst.append(val)
    stacked = [jnp.stack(o) for o in outs]
    return (xp, xs, stacked[0], stacked[1], stacked[2], stacked[3].astype(x_prompt.dtype),
            stacked[4], stacked[5], stacked[6], stacked[7].astype(state_ret.dtype))
```

```python
import functools
import math

import jax
import jax.numpy as jnp
from jax import lax
from jax.experimental import pallas as pl
from jax.experimental.pallas import tpu as pltpu

F32 = jnp.float32
BF16 = jnp.bfloat16

CHUNK = 64
EPS = 1e-6
ROPE_THETA = 10000.0
N_HEADS_A = 16
N_KV_A = 4
HEAD_DIM_A = 128
N_HEADS_IDX = 16
HEAD_DIM_IDX = 64
TOPK_MAX = 256
N_HEADS_R = 8
KEY_DIM_R = 128
VAL_DIM_R = 256

LANES = 128
MXU_COLS = 256
V7X_VMEM_BYTES = 64 * 1024 * 1024
VMEM_LIMIT = V7X_VMEM_BYTES - 8 * 1024 * 1024
BF16_SUBLANES = 16
VT_ROWS = HEAD_DIM_A + BF16_SUBLANES
LOG2E = math.log2(math.e)
MAX_BISECT_ITERS = 300
NEG_BIG = float(jnp.finfo(jnp.float32).min)


def _cparams(*sem):
    return pltpu.CompilerParams(dimension_semantics=sem, vmem_limit_bytes=VMEM_LIMIT)


def _dot(a, b):
    return jnp.dot(a, b, preferred_element_type=F32)


def _dot_nt(a, b):
    return lax.dot_general(a, b, (((1,), (1,)), ((), ())), preferred_element_type=F32)


def _rmsnorm_rows(x, g):
    return x * lax.rsqrt(jnp.mean(x * x, axis=-1, keepdims=True) + EPS) * g


def _rope_tables(pos):
    posf = pos.astype(F32)[:, None]

    def cs(half):
        inv_freq = ROPE_THETA ** (-jnp.arange(half, dtype=F32) / half)
        ang = posf * inv_freq[None, :]
        return jnp.cos(ang), jnp.sin(ang)

    c, s = cs(HEAD_DIM_A // 2)
    c128 = jnp.concatenate([c, c], axis=1)
    s128 = jnp.concatenate([-s, s], axis=1)
    c, s = cs(HEAD_DIM_IDX // 2)
    z = jnp.zeros_like(s)
    c64 = jnp.tile(c, (1, 4))
    a64 = jnp.tile(jnp.concatenate([-s, z], axis=1), (1, 2))
    b64 = jnp.tile(jnp.concatenate([z, s], axis=1), (1, 2))
    return jnp.stack([c128, s128, c64, a64, b64])


def _rope128(x, rope_ref):
    return x * rope_ref[0] + pltpu.roll(x, 64, 1) * rope_ref[1]


def _rope64(x, rope_ref):
    return (x * rope_ref[2] + pltpu.roll(x, 96, 1) * rope_ref[3]
            + pltpu.roll(x, 32, 1) * rope_ref[4])


def _proj_main_kernel(h_ref, w_ref, rope_ref, o_ref, *, groups, tn):
    j = pl.program_id(1)
    for lo, hi, mode, scale in groups:
        @pl.when(jnp.logical_and(j >= lo, j < hi))
        def _(mode=mode, scale=scale):
            for b in range(tn // MXU_COLS):
                acc = _dot(h_ref[...], w_ref[:, b * MXU_COLS:(b + 1) * MXU_COLS])
                for c in range(MXU_COLS // LANES):
                    x = acc[:, c * LANES:(c + 1) * LANES]
                    if mode == "rope128":
                        y = _rope128(x, rope_ref)
                    elif mode == "rope64":
                        y = _rope64(x, rope_ref)
                    elif mode == "sigmoid":
                        y = 1.0 / (1.0 + jnp.exp(-x))
                    elif mode == "silu":
                        y = x / (1.0 + jnp.exp(-x))
                    else:
                        y = x
                    if scale != 1.0:
                        y = y * scale
                    col = b * MXU_COLS + c * LANES
                    o_ref[:, col:col + LANES] = y.astype(o_ref.dtype)


def _proj_main(h, w_main, rope, groups, tm, tn):
    n, d = h.shape
    p = w_main.shape[1]
    n_pos_tiles = rope.shape[1] // tm
    return pl.pallas_call(
        functools.partial(_proj_main_kernel, groups=groups, tn=tn),
        grid=(n // tm, p // tn),
        in_specs=[pl.BlockSpec((tm, d), lambda i, j: (i, 0)),
                  pl.BlockSpec((d, tn), lambda i, j: (0, j)),
                  pl.BlockSpec((5, tm, LANES), lambda i, j: (0, i % n_pos_tiles, 0))],
        out_specs=pl.BlockSpec((tm, tn), lambda i, j: (i, j)),
        out_shape=jax.ShapeDtypeStruct((n, p), BF16),
        compiler_params=_cparams("parallel", "arbitrary"),
        name="proj_main",
    )(h, w_main, rope)


def _proj_kv_kernel(x_ref, g_ref, w_ref, rope_ref, h_ref, ka_ref, va_ref, ki_ref, wt_ref,
                    kab_ref, vtb_ref, kia_ref, kib_ref, acc_ref, *, nk, idx_scale):
    h_ref[...] = _rmsnorm_rows(x_ref[...], g_ref[...]).astype(h_ref.dtype)
    acc_ref[...] = _dot(h_ref[...], w_ref[...])
    tm = acc_ref.shape[0]
    for c in range(N_KV_A):
        sl = slice(c * LANES, (c + 1) * LANES)
        y = _rope128(acc_ref[:, sl], rope_ref)
        ka_ref[pl.ds(c, tm, stride=N_KV_A), :] = y
        kab_ref[:, sl] = y.astype(BF16)
        va_ref[pl.ds(c, tm, stride=N_KV_A), :] = acc_ref[:, nk + c * LANES:nk + (c + 1) * LANES]
    ones_rows = jnp.where(lax.broadcasted_iota(jnp.int32, (VT_ROWS - HEAD_DIM_A, tm), 0) == 0, 1.0, 0.0)
    for g in range(N_KV_A):
        vg = acc_ref[:, nk + g * HEAD_DIM_A:nk + (g + 1) * HEAD_DIM_A]
        vtb_ref[g * VT_ROWS:g * VT_ROWS + HEAD_DIM_A, :] = vg.T.astype(BF16)
        vtb_ref[g * VT_ROWS + HEAD_DIM_A:(g + 1) * VT_ROWS, :] = ones_rows.astype(BF16)
    z = acc_ref[:, 2 * nk:2 * nk + LANES]
    y = _rope64(z, rope_ref)
    ki_ref[...] = y[:, :HEAD_DIM_IDX]
    lane = lax.broadcasted_iota(jnp.int32, y.shape, 1)
    ya = jnp.where(lane < HEAD_DIM_IDX, y, 0.0)
    kia_ref[...] = ya.astype(BF16)
    kib_ref[...] = pltpu.roll(ya, HEAD_DIM_IDX, 1).astype(BF16)
    wt_ref[...] = (z * idx_scale).T[HEAD_DIM_IDX:HEAD_DIM_IDX + N_HEADS_IDX, :]


def _proj_kv(x, g1, w_kv, rope, tm):
    n, d = x.shape
    nk = N_KV_A * HEAD_DIM_A
    pw = w_kv.shape[1]
    n_pos_tiles = rope.shape[1] // tm
    idx_scale = (HEAD_DIM_IDX ** -0.5) * (N_HEADS_IDX ** -0.5)
    row = lambda i: (i, 0)
    return pl.pallas_call(
        functools.partial(_proj_kv_kernel, nk=nk, idx_scale=idx_scale),
        grid=(n // tm,),
        in_specs=[pl.BlockSpec((tm, d), row),
                  pl.BlockSpec((1, d), lambda i: (0, 0)),
                  pl.BlockSpec((d, pw), lambda i: (0, 0)),
                  pl.BlockSpec((5, tm, LANES), lambda i: (0, i % n_pos_tiles, 0))],
        out_specs=[pl.BlockSpec((tm, d), row),
                   pl.BlockSpec((tm * N_KV_A, HEAD_DIM_A), row), pl.BlockSpec((tm * N_KV_A, HEAD_DIM_A), row),
                   pl.BlockSpec((tm, HEAD_DIM_IDX), row),
                   pl.BlockSpec((N_HEADS_IDX, tm), lambda i: (0, i)),
                   pl.BlockSpec((tm, nk), row),
                   pl.BlockSpec((N_KV_A * VT_ROWS, tm), lambda i: (0, i)),
                   pl.BlockSpec((tm, LANES), row), pl.BlockSpec((tm, LANES), row)],
        out_shape=[jax.ShapeDtypeStruct((n, d), BF16),
                   jax.ShapeDtypeStruct((n * N_KV_A, HEAD_DIM_A), F32),
                   jax.ShapeDtypeStruct((n * N_KV_A, HEAD_DIM_A), F32),
                   jax.ShapeDtypeStruct((n, HEAD_DIM_IDX), F32),
                   jax.ShapeDtypeStruct((N_HEADS_IDX, n), F32),
                   jax.ShapeDtypeStruct((n, nk), BF16),
                   jax.ShapeDtypeStruct((N_KV_A * VT_ROWS, n), BF16),
                   jax.ShapeDtypeStruct((n, LANES), BF16), jax.ShapeDtypeStruct((n, LANES), BF16)],
        scratch_shapes=[pltpu.VMEM((tm, pw), F32)],
        compiler_params=_cparams("parallel"),
        name="proj_kv",
    )(x, g1.reshape(1, d), w_kv, rope)


def _fori_pairs(n, body, init):
    def two(j, carry):
        return body(2 * j + 1, body(2 * j, carry))
    carry = lax.fori_loop(0, n // 2, two, init)
    return lax.cond(n % 2 == 1, lambda c: body(n - 1, c), lambda c: c, carry)


def _index_kernel(qi_ref, wt_ref, kia_ref, kib_ref, m_ref, s_ref, *,
                  tq, ck, n_ck_total, l_valid, pos_base, causal, top_k):
    i = pl.program_id(1)
    if causal:
        n_c = (i * tq + tq + ck - 1) // ck
    else:
        n_c = n_ck_total
    qpos = pos_base + i * tq + lax.broadcasted_iota(jnp.int32, (1, tq), 1)
    lim = jnp.minimum(qpos - lax.rem(qpos, CHUNK) + CHUNK, l_valid)
    limf = lim.astype(F32)
    kprime = jnp.minimum(float(top_k), limf)
    inf = jnp.float32(jnp.inf)

    def key_index(off):
        return off + lax.broadcasted_iota(jnp.int32, (ck, tq), 0)

    def score_chunk(c, carry):
        rmax, rmin = carry
        off = pl.multiple_of(c * ck, ck)
        ka = kia_ref[pl.ds(off, ck), :]
        kb = kib_ref[pl.ds(off, ck), :]
        acc = jnp.zeros((ck, tq), F32)
        for p in range(N_HEADS_IDX // 2):
            qp = qi_ref[:, p * LANES:(p + 1) * LANES]
            acc += jnp.maximum(_dot_nt(ka, qp), 0.0) * wt_ref[2 * p:2 * p + 1, :]
            acc += jnp.maximum(_dot_nt(kb, qp), 0.0) * wt_ref[2 * p + 1:2 * p + 2, :]
        adm = key_index(off) < lim
        s_ref[pl.ds(off, ck), :] = jnp.where(adm, acc, -inf)
        rmax = jnp.maximum(rmax, jnp.max(jnp.where(adm, acc, -inf), axis=0, keepdims=True))
        rmin = jnp.minimum(rmin, jnp.min(jnp.where(adm, acc, inf), axis=0, keepdims=True))
        return rmax, rmin

    rmax, rmin = _fori_pairs(n_c, score_chunk, (jnp.full((1, tq), -inf, F32), jnp.full((1, tq), inf, F32)))

    fold_rows = min(ck, 64)

    def fold(ind):
        return ind.reshape(ck // fold_rows, fold_rows, tq).sum(axis=0)

    def count_ge(th):
        def body(c, acc):
            off = pl.multiple_of(c * ck, ck)
            return acc + fold(jnp.where(s_ref[pl.ds(off, ck), :] >= th, 1.0, 0.0))
        acc = lax.fori_loop(0, n_c, body, jnp.zeros((fold_rows, tq), F32))
        return acc.sum(axis=0, keepdims=True)

    def cond(st):
        it, lo, hi, cnt, stuck = st
        active = jnp.where(jnp.logical_and(cnt > kprime, stuck < 0.5), 1.0, 0.0)
        return jnp.logical_and(it < MAX_BISECT_ITERS, jnp.max(active) > 0.5)

    def body(st):
        it, lo, hi, cnt, stuck = st
        open_top = hi == inf
        mid = lo + 0.5 * (jnp.where(open_top, rmax, hi) - lo)
        mid = jnp.where(jnp.logical_and(open_top, mid <= lo), rmax, mid)
        c = count_ge(mid)
        ge = c >= kprime
        no_progress = jnp.logical_or(mid <= lo, mid >= hi)
        return (it + 1, jnp.where(ge, mid, lo), jnp.where(ge, hi, mid), jnp.where(ge, c, cnt),
                jnp.where(no_progress, 1.0, stuck))

    _, lo, hi, cnt, _ = lax.while_loop(
        cond, body, (jnp.int32(0), rmin, jnp.full((1, tq), inf, F32), limf, jnp.zeros((1, tq), F32)))

    unresolved = jnp.max(jnp.where(cnt > kprime, 1.0, 0.0)) > 0.5

    @pl.when(jnp.logical_not(unresolved))
    def _():
        def body(c, _):
            off = pl.multiple_of(c * ck, ck)
            m_ref[pl.ds(off, ck), :] = jnp.where(s_ref[pl.ds(off, ck), :] >= lo, 0.0, -inf).astype(m_ref.dtype)
            return 0
        lax.fori_loop(0, n_c, body, 0)

    @pl.when(unresolved)
    def _():
        need = kprime - count_ge(hi)

        def in_tie(blk):
            return jnp.logical_and(blk >= lo, blk < hi)

        def count_tie_below(jcut):
            def body(c, acc):
                off = pl.multiple_of(c * ck, ck)
                blk = s_ref[pl.ds(off, ck), :]
                e = jnp.logical_and(in_tie(blk), key_index(off).astype(F32) < jcut)
                return acc + fold(jnp.where(e, 1.0, 0.0))
            acc = lax.fori_loop(0, n_c, body, jnp.zeros((fold_rows, tq), F32))
            return acc.sum(axis=0, keepdims=True)

        def jbody(_, st):
            jlo, jhi = st
            mid = jnp.floor(0.5 * (jlo + jhi))
            ok = count_tie_below(mid) >= need
            return jnp.where(ok, jlo, mid + 1.0), jnp.where(ok, mid, jhi)

        n_total = n_ck_total * ck
        _, jcut = lax.fori_loop(0, int(math.ceil(math.log2(n_total + 1))), jbody,
                                (jnp.zeros((1, tq), F32), jnp.full((1, tq), float(n_total), F32)))

        def body(c, _):
            off = pl.multiple_of(c * ck, ck)
            blk = s_ref[pl.ds(off, ck), :]
            e = jnp.logical_and(in_tie(blk), key_index(off).astype(F32) < jcut)
            keep = jnp.logical_or(blk >= hi, e)
            m_ref[pl.ds(off, ck), :] = jnp.where(keep, 0.0, -inf).astype(m_ref.dtype)
            return 0
        lax.fori_loop(0, n_c, body, 0)

    def zero_body(c, _):
        off = pl.multiple_of(c * ck, ck)
        m_ref[pl.ds(off, ck), :] = jnp.full((ck, tq), -inf, m_ref.dtype)
        return 0
    lax.fori_loop(n_c, n_ck_total, zero_body, 0)


def _index_mask(p_main, qi_col_block, wt, kia, kib, *, tq, ck, l_valid, pos_base, causal, top_k):
    b, lp, _ = kia.shape
    nq = p_main.shape[0] // (b * tq)
    qw = N_HEADS_IDX * HEAD_DIM_IDX
    return pl.pallas_call(
        functools.partial(_index_kernel, tq=tq, ck=ck, n_ck_total=lp // ck, l_valid=l_valid,
                          pos_base=pos_base, causal=causal, top_k=top_k),
        grid=(b, nq),
        in_specs=[pl.BlockSpec((tq, qw), lambda bi, i: (bi * nq + i, qi_col_block)),
                  pl.BlockSpec((N_HEADS_IDX, tq), lambda bi, i: (0, bi * nq + i)),
                  pl.BlockSpec((None, lp, LANES), lambda bi, i: (bi, 0, 0)),
                  pl.BlockSpec((None, lp, LANES), lambda bi, i: (bi, 0, 0))],
        out_specs=pl.BlockSpec((None, lp, tq), lambda bi, i: (bi, 0, i)),
        out_shape=jax.ShapeDtypeStruct((b, lp, nq * tq), BF16),
        scratch_shapes=[pltpu.VMEM((lp, tq), F32)],
        compiler_params=_cparams("parallel", "parallel"),
        name="index_mask",
    )(p_main, wt, kia, kib)


def _attn_kernel(q_ref, k_ref, vt_ref, m_ref, o_ref, acc_ref, sa_ref, sb_ref, ma_ref, mb_ref, *,
                 tq, tk, n_kt_total, causal, group, kvb):
    i = pl.program_id(1)
    heads = kvb * group
    if causal:
        n_kt = (i * tq + tq + tk - 1) // tk
    else:
        n_kt = n_kt_total
    acc_ref[...] = jnp.zeros_like(acc_ref)

    def scores(kt, bufs):
        s_ref, mx_ref = bufs
        off = pl.multiple_of(kt * tk, tk)
        bias = m_ref[pl.ds(off, tk), :].astype(F32)
        for h in range(heads):
            kv = h // group
            k = k_ref[pl.ds(off, tk), kv * HEAD_DIM_A:(kv + 1) * HEAD_DIM_A]
            s = _dot_nt(k, q_ref[:, h * HEAD_DIM_A:(h + 1) * HEAD_DIM_A]) + bias
            s_ref[h] = s
            mx_ref[h] = jnp.max(s, axis=0, keepdims=True)

    def softmax_pv(kt, bufs, ms):
        s_ref, mx_ref = bufs
        off = pl.multiple_of(kt * tk, tk)
        out = []
        for h in range(heads):
            kv = h // group
            vt = vt_ref[kv * VT_ROWS:(kv + 1) * VT_ROWS, pl.ds(off, tk)]
            m_new = jnp.maximum(ms[h], mx_ref[h])
            alpha = jnp.exp2(ms[h] - m_new)
            p = jnp.exp2(s_ref[h] - m_new).astype(BF16)
            acc_ref[h] = acc_ref[h] * alpha + _dot(vt, p)
            out.append(m_new)
        return tuple(out)

    buf_a, buf_b = (sa_ref, ma_ref), (sb_ref, mb_ref)

    def body(j, ms):
        scores(2 * j + 1, buf_b)
        ms = softmax_pv(2 * j, buf_a, ms)
        scores(2 * j + 2, buf_a)
        return softmax_pv(2 * j + 1, buf_b, ms)

    scores(0, buf_a)
    n_pairs = (n_kt - 1) // 2
    ms = lax.fori_loop(0, n_pairs, body, tuple(jnp.full((1, tq), NEG_BIG, F32) for _ in range(heads)))
    even = n_kt - 1 > 2 * n_pairs

    @pl.when(even)
    def _():
        scores(2 * n_pairs + 1, buf_b)
        softmax_pv(2 * n_pairs + 1, buf_b, softmax_pv(2 * n_pairs, buf_a, ms))

    @pl.when(jnp.logical_not(even))
    def _():
        softmax_pv(2 * n_pairs, buf_a, ms)

    for h in range(heads):
        a = acc_ref[h]
        o = a[:HEAD_DIM_A] / a[HEAD_DIM_A:HEAD_DIM_A + 1]
        o_ref[:, h * HEAD_DIM_A:(h + 1) * HEAD_DIM_A] = o.T.astype(o_ref.dtype)


def _attention(p_main, k_bf, vt_bf, mask_t, *, tq, tk, causal, kvb):
    b, lp, _ = k_bf.shape
    nq = mask_t.shape[2] // tq
    group = N_HEADS_A // N_KV_A
    heads = kvb * group
    gw = heads * HEAD_DIM_A
    return pl.pallas_call(
        functools.partial(_attn_kernel, tq=tq, tk=tk, n_kt_total=lp // tk, causal=causal, group=group,
                          kvb=kvb),
        grid=(b, nq, N_KV_A // kvb),
        in_specs=[pl.BlockSpec((tq, gw), lambda bi, i, g: (bi * nq + i, g)),
                  pl.BlockSpec((None, lp, kvb * HEAD_DIM_A), lambda bi, i, g: (bi, 0, g)),
                  pl.BlockSpec((kvb * VT_ROWS, lp), lambda bi, i, g: (g, bi)),
                  pl.BlockSpec((None, lp, tq), lambda bi, i, g: (bi, 0, i))],
        out_specs=pl.BlockSpec((tq, gw), lambda bi, i, g: (bi * nq + i, g)),
        out_shape=jax.ShapeDtypeStruct((b * nq * tq, N_HEADS_A * HEAD_DIM_A), BF16),
        scratch_shapes=[pltpu.VMEM((heads, VT_ROWS, tq), F32),
                        pltpu.VMEM((heads, tk, tq), F32), pltpu.VMEM((heads, tk, tq), F32),
                        pltpu.VMEM((heads, 1, tq), F32), pltpu.VMEM((heads, 1, tq), F32)],
        compiler_params=_cparams("parallel", "parallel", "arbitrary"),
        name="attention",
    )(p_main, k_bf, vt_bf, mask_t)


def _retention_tables(c):
    log_gamma = jnp.log1p(-(2.0 ** (-5.0 - jnp.arange(N_HEADS_R, dtype=F32))))
    idx = jnp.arange(c, dtype=F32)
    diff = idx[:, None] - idx[None, :]
    decay = jnp.where(diff[None] >= 0,
                      jnp.exp(jnp.maximum(diff, 0.0)[None] * log_gamma[:, None, None]), 0.0)
    cross = jnp.exp((idx + 1.0)[None, :] * log_gamma[:, None])
    kdec = jnp.exp((c - 1.0 - idx)[None, :] * log_gamma[:, None])
    full = jnp.exp(c * log_gamma)
    bc = lambda a, w: jnp.broadcast_to(a[..., None], a.shape + (w,))
    return decay, bc(cross, KEY_DIM_R), bc(kdec, KEY_DIM_R), bc(full[:, None], VAL_DIM_R)


def _retention_kernel(q_ref, k_ref, v_ref, g_ref, dec_ref, qsc_ref, ksc_ref, gc_ref, s0_ref,
                      o_ref, st_ref):
    @pl.when(pl.program_id(1) == 0)
    def _():
        st_ref[...] = s0_ref[...]

    for h in range(N_HEADS_R):
        ks = slice(h * KEY_DIM_R, (h + 1) * KEY_DIM_R)
        vs = slice(h * VAL_DIM_R, (h + 1) * VAL_DIM_R)
        q = q_ref[:, ks]
        k = k_ref[:, ks]
        v = v_ref[:, vs]
        st = st_ref[h]
        inner = _dot_nt(q, k) * dec_ref[h]
        qd = (q.astype(F32) * qsc_ref[h]).astype(BF16)
        o = _dot(inner.astype(BF16), v) + _dot(qd, st.astype(BF16))
        kdt = (k.astype(F32) * ksc_ref[h]).T.astype(BF16)
        st_ref[h] = st * gc_ref[h] + _dot(kdt, v)
        o = o * lax.rsqrt(jnp.mean(o * o, axis=-1, keepdims=True) + EPS)
        o_ref[:, vs] = (o * g_ref[:, vs].astype(F32)).astype(o_ref.dtype)


def _retention(p_main, cols, state0, c):
    b = state0.shape[0]
    nc = p_main.shape[0] // (b * c)
    kw = N_HEADS_R * KEY_DIM_R
    vw = N_HEADS_R * VAL_DIM_R
    dec, qsc, ksc, gcs = _retention_tables(c)
    qb, kb, vb, gb = cols
    const3 = lambda bi, ci: (0, 0, 0)
    st_spec = pl.BlockSpec((None, N_HEADS_R, KEY_DIM_R, VAL_DIM_R), lambda bi, ci: (bi, 0, 0, 0))
    return pl.pallas_call(
        _retention_kernel,
        grid=(b, nc),
        in_specs=[pl.BlockSpec((c, kw), lambda bi, ci: (bi * nc + ci, qb)),
                  pl.BlockSpec((c, kw), lambda bi, ci: (bi * nc + ci, kb)),
                  pl.BlockSpec((c, vw), lambda bi, ci: (bi * nc + ci, vb)),
                  pl.BlockSpec((c, vw), lambda bi, ci: (bi * nc + ci, gb)),
                  pl.BlockSpec(dec.shape, const3), pl.BlockSpec(qsc.shape, const3),
                  pl.BlockSpec(ksc.shape, const3), pl.BlockSpec(gcs.shape, const3),
                  st_spec],
        out_specs=[pl.BlockSpec((c, vw), lambda bi, ci: (bi * nc + ci, 0)), st_spec],
        out_shape=[jax.ShapeDtypeStruct((b * nc * c, vw), BF16),
                   jax.ShapeDtypeStruct(state0.shape, F32)],
        compiler_params=_cparams("parallel", "arbitrary"),
        name="retention",
    )(p_main, p_main, p_main, p_main, dec, qsc, ksc, gcs, state0)


def _merge_kernel(oa_ref, ob_ref, wa_ref, wb_ref, sa_ref, sb_ref, o_ref):
    for b in range(o_ref.shape[1] // MXU_COLS):
        sl = slice(b * MXU_COLS, (b + 1) * MXU_COLS)
        m = (_dot(oa_ref[...], wa_ref[:, sl]) * sa_ref[:, sl].astype(F32)
             + _dot(ob_ref[...], wb_ref[:, sl]) * sb_ref[:, sl].astype(F32))
        o_ref[:, sl] = m.astype(o_ref.dtype)


def _merge(o_a, o_b, w_pa, w_pb, p_main, ga_off, gb_off, tm, tn):
    n, wa = o_a.shape
    wb = o_b.shape[1]
    d = w_pa.shape[1]
    return pl.pallas_call(
        _merge_kernel,
        grid=(n // tm, d // tn),
        in_specs=[pl.BlockSpec((tm, wa), lambda i, j: (i, 0)),
                  pl.BlockSpec((tm, wb), lambda i, j: (i, 0)),
                  pl.BlockSpec((wa, tn), lambda i, j: (0, j)),
                  pl.BlockSpec((wb, tn), lambda i, j: (0, j)),
                  pl.BlockSpec((tm, tn), lambda i, j: (i, ga_off // tn + j)),
                  pl.BlockSpec((tm, tn), lambda i, j: (i, gb_off // tn + j))],
        out_specs=pl.BlockSpec((tm, tn), lambda i, j: (i, j)),
        out_shape=jax.ShapeDtypeStruct((n, d), BF16),
        compiler_params=_cparams("parallel", "arbitrary"),
        name="merge",
    )(o_a, o_b, w_pa, w_pb, p_main, p_main)


def _out_proj_kernel(x_ref, m_ref, w_ref, x1_ref):
    x1_ref[...] = x_ref[...] + _dot(m_ref[...], w_ref[...])


def _out_proj(x, merged, w_o, tm):
    n, d = x.shape
    row = lambda i: (i, 0)
    return pl.pallas_call(
        _out_proj_kernel,
        grid=(n // tm,),
        in_specs=[pl.BlockSpec((tm, d), row), pl.BlockSpec((tm, d), row),
                  pl.BlockSpec((d, d), lambda i: (0, 0))],
        out_specs=pl.BlockSpec((tm, d), row),
        out_shape=jax.ShapeDtypeStruct((n, d), F32),
        compiler_params=_cparams("parallel"),
        name="out_proj",
    )(x, merged, w_o)


def _ffn_kernel(x1_ref, g2_ref, wg_ref, wu_ref, wd_ref, g_ref, y_ref, h_ref, *, final_norm):
    c = pl.program_id(1)

    @pl.when(c == 0)
    def _():
        x1 = x1_ref[...]
        y_ref[...] = x1
        h_ref[...] = _rmsnorm_rows(x1, g2_ref[...]).astype(h_ref.dtype)

    acts = []
    for b in range(wg_ref.shape[1] // MXU_COLS):
        sl = slice(b * MXU_COLS, (b + 1) * MXU_COLS)
        a = _dot(h_ref[...], wg_ref[:, sl])
        u = _dot(h_ref[...], wu_ref[:, sl])
        acts.append((a / (1.0 + jnp.exp(-a)) * u).astype(BF16))
    upd = _dot(acts[0], wd_ref[0:MXU_COLS, :])
    for b in range(1, len(acts)):
        upd += _dot(acts[b], wd_ref[b * MXU_COLS:(b + 1) * MXU_COLS, :])
    y_ref[...] += upd

    if final_norm:
        @pl.when(c == pl.num_programs(1) - 1)
        def _():
            y_ref[...] = _rmsnorm_rows(y_ref[...], g_ref[...])


def _ffn(x1, g2, wg, wu, wd, gf, tm, tc, final_norm):
    n, d = x1.shape
    f = wg.shape[1]
    row = lambda i, c: (i, 0)
    vec = pl.BlockSpec((1, d), lambda i, c: (0, 0))
    return pl.pallas_call(
        functools.partial(_ffn_kernel, final_norm=final_norm),
        grid=(n // tm, f // tc),
        in_specs=[pl.BlockSpec((tm, d), row, pipeline_mode=pl.Buffered(1)), vec,
                  pl.BlockSpec((d, tc), lambda i, c: (0, c)),
                  pl.BlockSpec((d, tc), lambda i, c: (0, c)),
                  pl.BlockSpec((tc, d), lambda i, c: (c, 0)), vec],
        out_specs=pl.BlockSpec((tm, d), row),
        out_shape=jax.ShapeDtypeStruct((n, d), F32),
        scratch_shapes=[pltpu.VMEM((tm, d), BF16)],
        compiler_params=_cparams("parallel", "arbitrary"),
        name="ffn",
    )(x1, g2.reshape(1, d), wg, wu, wd, gf.reshape(1, d))


class _MainLayout:
    def __init__(self, d_model):
        wa = N_HEADS_A * HEAD_DIM_A
        wb = N_HEADS_R * VAL_DIM_R
        kr = N_HEADS_R * KEY_DIM_R
        qi = N_HEADS_IDX * HEAD_DIM_IDX
        order = [("qa", wa, "rope128", HEAD_DIM_A ** -0.5 * LOG2E), ("vr", wb, "plain", 1.0),
                 ("gr", wb, "silu", 1.0), ("ga", d_model, "sigmoid", 1.0), ("gb", d_model, "sigmoid", 1.0),
                 ("qi", qi, "rope64", 1.0), ("qr", kr, "rope128", 1.0),
                 ("kr", kr, "rope128", KEY_DIM_R ** -0.5)]
        self.off, self.width = {}, {}
        self.order = order
        o = 0
        for name, w, _, _ in order:
            self.off[name], self.width[name] = o, w
            o += w
        self.total = o

    def groups(self, tn):
        out = []
        for name, w, mode, scale in self.order:
            assert self.off[name] % tn == 0 and w % tn == 0
            out.append((self.off[name] // tn, (self.off[name] + w) // tn, mode, scale))
        return tuple(out)

    def block(self, name, width=None):
        width = width or self.width[name]
        assert self.off[name] % width == 0
        return self.off[name] // width


def _split_w_in(w, d_model):
    wa = N_HEADS_A * HEAD_DIM_A
    nk = N_KV_A * HEAD_DIM_A
    wb = N_HEADS_R * VAL_DIM_R
    kr = N_HEADS_R * KEY_DIM_R
    names = ("qa", "ka", "va", "qi", "ki", "wi", "qr", "kr", "vr", "gr", "ga", "gb")
    widths = (wa, nk, nk, N_HEADS_IDX * HEAD_DIM_IDX, HEAD_DIM_IDX, N_HEADS_IDX, kr, kr, wb, wb,
              d_model, d_model)
    parts, o = {}, 0
    for name, wd in zip(names, widths):
        parts[name] = w[:, o:o + wd]
        o += wd
    assert o == w.shape[1]
    return parts


def _tile(n, pref):
    t = min(n, pref)
    assert n % t == 0
    return t


def _dense_pre(x2, pos_rows, lw, lay):
    n = x2.shape[0]
    rope = _rope_tables(pos_rows)
    h, *kv = _proj_kv(x2, lw["g1"], lw["w_kv"], rope, _tile(n, 512))
    tn = 1024
    p_main = _proj_main(h, lw["w_main"], rope, lay.groups(tn), _tile(n, 1024), tn)
    return p_main, kv


def _dense_post(x2, o_a, o_b, p_main, lw, lay, gf, final_norm):
    n = x2.shape[0]
    tm = _tile(n, 1024)
    merged = _merge(o_a, o_b, lw["w_pa"], lw["w_pb"], p_main, lay.off["ga"], lay.off["gb"], tm, 512)
    x1 = _out_proj(x2, merged, lw["w_o"], _tile(n, 512))
    return _ffn(x1, lw["g2"], lw["wg"], lw["wu"], lw["wd"], gf, tm, 512, final_norm)


def _prompt_layer(xp, lw, lay, gf, final_norm):
    b, t, d = xp.shape
    x2 = xp.reshape(b * t, d)
    p_main, (ka, va, ki, wt, ka_bf, vt_bf, kia, kib) = _dense_pre(x2, jnp.arange(t, dtype=jnp.int32), lw, lay)
    top_k = min(TOPK_MAX, t // 4)
    tq, ck = _tile(t, 256), _tile(t, 512)
    mask_t = _index_mask(p_main, lay.block("qi"), wt, kia.reshape(b, t, LANES), kib.reshape(b, t, LANES),
                         tq=_tile(t, 512), ck=ck, l_valid=t, pos_base=0, causal=True, top_k=top_k)
    nk = N_KV_A * HEAD_DIM_A
    o_a = _attention(p_main, ka_bf.reshape(b, t, nk), vt_bf, mask_t, tq=tq, tk=ck, causal=True, kvb=2)
    state0 = jnp.zeros((b, N_HEADS_R, KEY_DIM_R, VAL_DIM_R), F32)
    cols = (lay.block("qr"), lay.block("kr"), lay.block("vr"), lay.block("gr"))
    o_b, st = _retention(p_main, cols, state0, _tile(t, 256))
    y = _dense_post(x2, o_a, o_b, p_main, lw, lay, gf, final_norm)
    return (y.reshape(b, t, d), ka.reshape(b, t, N_KV_A, HEAD_DIM_A), va.reshape(b, t, N_KV_A, HEAD_DIM_A),
            ki.reshape(b, t, HEAD_DIM_IDX), st)


def _sample_layer(xs, cache_k, cache_v, cache_i, state, lw, lay, gf, final_norm):
    b, t, d = xs.shape
    past = cache_k.shape[1]
    nk = N_KV_A * HEAD_DIM_A
    x2 = xs.reshape(b * t, d)
    pos = jnp.tile(past + jnp.arange(t, dtype=jnp.int32), b)
    p_main, (ka, va, ki, wt, ka_bf, vt_bf, kia, kib) = _dense_pre(x2, pos, lw, lay)

    tq = LANES
    l_valid = past + t
    tk = 3 * LANES
    lp = -(-l_valid // tk) * tk
    pq = jnp.pad(p_main.reshape(b, t, -1), ((0, 0), (0, tq - t), (0, 0)), mode="edge").reshape(b * tq, -1)
    wtq = jnp.pad(wt.reshape(N_HEADS_IDX, b, t), ((0, 0), (0, 0), (0, tq - t)),
                  mode="edge").reshape(N_HEADS_IDX, b * tq)
    kpad = ((0, 0), (0, lp - l_valid), (0, 0))
    ci = cache_i.astype(BF16)
    zi = jnp.zeros_like(ci)
    kia_all = jnp.pad(jnp.concatenate([jnp.concatenate([ci, zi], -1), kia.reshape(b, t, LANES)], 1), kpad)
    kib_all = jnp.pad(jnp.concatenate([jnp.concatenate([zi, ci], -1), kib.reshape(b, t, LANES)], 1), kpad)
    k_all = jnp.pad(jnp.concatenate([cache_k.reshape(b, past, nk).astype(BF16), ka_bf.reshape(b, t, nk)], 1), kpad)
    vt_new = vt_bf.reshape(N_KV_A, VT_ROWS, b, t)
    vt_cache = cache_v.reshape(b, past, N_KV_A, HEAD_DIM_A).transpose(2, 3, 0, 1).astype(BF16)
    ones_rows = jnp.zeros((N_KV_A, VT_ROWS - HEAD_DIM_A, b, past), BF16).at[:, 0].set(1.0)
    vt_all = jnp.concatenate([jnp.concatenate([vt_cache, ones_rows], 1), vt_new], 3)
    vt_all = jnp.pad(vt_all, ((0, 0), (0, 0), (0, 0), (0, lp - l_valid))).reshape(N_KV_A * VT_ROWS, b * lp)

    top_k = min(TOPK_MAX, l_valid // 4)
    mask_t = _index_mask(pq, lay.block("qi"), wtq, kia_all, kib_all, tq=tq, ck=tk, l_valid=l_valid,
                         pos_base=past, causal=False, top_k=top_k)
    o_a = _attention(pq, k_all, vt_all, mask_t, tq=tq, tk=tk, causal=False, kvb=1)
    o_a = o_a.reshape(b, tq, -1)[:, :t].reshape(b * t, -1)
    cols = (lay.block("qr"), lay.block("kr"), lay.block("vr"), lay.block("gr"))
    o_b, st = _retention(p_main, cols, state.astype(F32), t)
    y = _dense_post(x2, o_a, o_b, p_main, lw, lay, gf, final_norm)
    return (y.reshape(b, t, d), ka.reshape(b, t, N_KV_A, HEAD_DIM_A), va.reshape(b, t, N_KV_A, HEAD_DIM_A),
            ki.reshape(b, t, HEAD_DIM_IDX), st)


def kernel(x_prompt, x_sample, cache_k, cache_v, cache_idx_k, state_ret, norm1_g, w_in, w_pa, w_pb, w_o,
           norm2_g, w_ffn_gate, w_ffn_up, w_ffn_down, norm_f_g):
    assert HEAD_DIM_A == LANES and HEAD_DIM_IDX * 2 == LANES and N_HEADS_IDX % 2 == 0
    depth, d_model = norm1_g.shape
    lay = _MainLayout(d_model)
    xp, xs = x_prompt, x_sample
    outs = [[] for _ in range(8)]
    for l in range(depth):
        parts = _split_w_in(w_in[l], d_model)
        pad = jnp.zeros((d_model, LANES - HEAD_DIM_IDX - N_HEADS_IDX), F32)
        lw = {
            "g1": norm1_g[l], "g2": norm2_g[l],
            "w_main": jnp.concatenate([parts[name] for name, _, _, _ in lay.order], axis=1).astype(BF16),
            "w_kv": jnp.concatenate([parts["ka"], parts["va"], parts["ki"], parts["wi"], pad], axis=1).astype(BF16),
            "w_pa": w_pa[l].astype(BF16), "w_pb": w_pb[l].astype(BF16), "w_o": w_o[l].astype(BF16),
            "wg": w_ffn_gate[l].astype(BF16), "wu": w_ffn_up[l].astype(BF16), "wd": w_ffn_down[l].astype(BF16),
        }
        last = l == depth - 1
        xp, kp, vp, ip, sp = _prompt_layer(xp, lw, lay, norm_f_g, last)
        xs, ks, vs, isl, ss = _sample_layer(xs, cache_k[l], cache_v[l], cache_idx_k[l], state_ret[l],
                                            lw, lay, norm_f_g, last)
        for lst, val in zip(outs, (kp, vp, ip, sp, ks, vs, isl, ss)):
            lst.append(val)
    stacked = [jnp.stack(o) for o in outs]
    return (xp, xs, stacked[0], stacked[1], stacked[2], stacked[3].astype(x_prompt.dtype),
            stacked[4], stacked[5], stacked[6], stacked[7].astype(state_ret.dtype))
```

```python
import functools
import math

import jax
import jax.numpy as jnp
from jax import lax
from jax.experimental import pallas as pl
from jax.experimental.pallas import tpu as pltpu

F32 = jnp.float32
BF16 = jnp.bfloat16

CHUNK = 64
EPS = 1e-6
ROPE_THETA = 10000.0
N_HEADS_A = 16
N_KV_A = 4
HEAD_DIM_A = 128
N_HEADS_IDX = 16
HEAD_DIM_IDX = 64
TOPK_MAX = 256
N_HEADS_R = 8
KEY_DIM_R = 128
VAL_DIM_R = 256

LANES = 128
MXU_COLS = 256
V7X_VMEM_BYTES = 64 * 1024 * 1024
VMEM_LIMIT = V7X_VMEM_BYTES - 8 * 1024 * 1024
BF16_SUBLANES = 16
VT_ROWS = HEAD_DIM_A + BF16_SUBLANES
LOG2E = math.log2(math.e)
MAX_BISECT_ITERS = 300
NEG_BIG = float(jnp.finfo(jnp.float32).min)


def _cparams(*sem):
    return pltpu.CompilerParams(dimension_semantics=sem, vmem_limit_bytes=VMEM_LIMIT)


def _dot(a, b):
    return jnp.dot(a, b, preferred_element_type=F32)


def _dot_nt(a, b):
    return lax.dot_general(a, b, (((1,), (1,)), ((), ())), preferred_element_type=F32)


def _rmsnorm_rows(x, g):
    return x * lax.rsqrt(jnp.mean(x * x, axis=-1, keepdims=True) + EPS) * g


def _rope_tables(pos):
    posf = pos.astype(F32)[:, None]

    def cs(half):
        inv_freq = ROPE_THETA ** (-jnp.arange(half, dtype=F32) / half)
        ang = posf * inv_freq[None, :]
        return jnp.cos(ang), jnp.sin(ang)

    c, s = cs(HEAD_DIM_A // 2)
    c128 = jnp.concatenate([c, c], axis=1)
    s128 = jnp.concatenate([-s, s], axis=1)
    c, s = cs(HEAD_DIM_IDX // 2)
    z = jnp.zeros_like(s)
    c64 = jnp.tile(c, (1, 4))
    a64 = jnp.tile(jnp.concatenate([-s, z], axis=1), (1, 2))
    b64 = jnp.tile(jnp.concatenate([z, s], axis=1), (1, 2))
    return jnp.stack([c128, s128, c64, a64, b64])


def _rope128(x, rope_ref):
    return x * rope_ref[0] + pltpu.roll(x, 64, 1) * rope_ref[1]


def _rope64(x, rope_ref):
    return (x * rope_ref[2] + pltpu.roll(x, 96, 1) * rope_ref[3]
            + pltpu.roll(x, 32, 1) * rope_ref[4])


def _proj_main_kernel(h_ref, w_ref, rope_ref, o_ref, *, groups, tn):
    j = pl.program_id(1)
    for lo, hi, mode, scale in groups:
        @pl.when(jnp.logical_and(j >= lo, j < hi))
        def _(mode=mode, scale=scale):
            for b in range(tn // MXU_COLS):
                acc = _dot(h_ref[...], w_ref[:, b * MXU_COLS:(b + 1) * MXU_COLS])
                for c in range(MXU_COLS // LANES):
                    x = acc[:, c * LANES:(c + 1) * LANES]
                    if mode == "rope128":
                        y = _rope128(x, rope_ref)
                    elif mode == "rope64":
                        y = _rope64(x, rope_ref)
                    elif mode == "sigmoid":
                        y = 1.0 / (1.0 + jnp.exp(-x))
                    elif mode == "silu":
                        y = x / (1.0 + jnp.exp(-x))
                    else:
                        y = x
                    if scale != 1.0:
                        y = y * scale
                    col = b * MXU_COLS + c * LANES
                    o_ref[:, col:col + LANES] = y.astype(o_ref.dtype)


def _proj_main(h, w_main, rope, groups, tm, tn):
    n, d = h.shape
    p = w_main.shape[1]
    n_pos_tiles = rope.shape[1] // tm
    return pl.pallas_call(
        functools.partial(_proj_main_kernel, groups=groups, tn=tn),
        grid=(n // tm, p // tn),
        in_specs=[pl.BlockSpec((tm, d), lambda i, j: (i, 0)),
                  pl.BlockSpec((d, tn), lambda i, j: (0, j)),
                  pl.BlockSpec((5, tm, LANES), lambda i, j: (0, i % n_pos_tiles, 0))],
        out_specs=pl.BlockSpec((tm, tn), lambda i, j: (i, j)),
        out_shape=jax.ShapeDtypeStruct((n, p), BF16),
        compiler_params=_cparams("parallel", "arbitrary"),
        name="proj_main",
    )(h, w_main, rope)


def _proj_kv_kernel(x_ref, g_ref, w_ref, rope_ref, h_ref, ka_ref, va_ref, ki_ref, wt_ref,
                    kab_ref, vtb_ref, kia_ref, kib_ref, acc_ref, *, nk, idx_scale):
    h_ref[...] = _rmsnorm_rows(x_ref[...], g_ref[...]).astype(h_ref.dtype)
    acc_ref[...] = _dot(h_ref[...], w_ref[...])
    tm = acc_ref.shape[0]
    for c in range(N_KV_A):
        sl = slice(c * LANES, (c + 1) * LANES)
        y = _rope128(acc_ref[:, sl], rope_ref)
        ka_ref[pl.ds(c, tm, stride=N_KV_A), :] = y
        kab_ref[:, sl] = y.astype(BF16)
        va_ref[pl.ds(c, tm, stride=N_KV_A), :] = acc_ref[:, nk + c * LANES:nk + (c + 1) * LANES]
    ones_rows = jnp.where(lax.broadcasted_iota(jnp.int32, (VT_ROWS - HEAD_DIM_A, tm), 0) == 0, 1.0, 0.0)
    for g in range(N_KV_A):
        vg = acc_ref[:, nk + g * HEAD_DIM_A:nk + (g + 1) * HEAD_DIM_A]
        vtb_ref[g * VT_ROWS:g * VT_ROWS + HEAD_DIM_A, :] = vg.T.astype(BF16)
        vtb_ref[g * VT_ROWS + HEAD_DIM_A:(g + 1) * VT_ROWS, :] = ones_rows.astype(BF16)
    z = acc_ref[:, 2 * nk:2 * nk + LANES]
    y = _rope64(z, rope_ref)
    ki_ref[...] = y[:, :HEAD_DIM_IDX]
    lane = lax.broadcasted_iota(jnp.int32, y.shape, 1)
    ya = jnp.where(lane < HEAD_DIM_IDX, y, 0.0)
    kia_ref[...] = ya.astype(BF16)
    kib_ref[...] = pltpu.roll(ya, HEAD_DIM_IDX, 1).astype(BF16)
    wt_ref[...] = (z * idx_scale).T[HEAD_DIM_IDX:HEAD_DIM_IDX + N_HEADS_IDX, :]


def _proj_kv(x, g1, w_kv, rope, tm):
    n, d = x.shape
    nk = N_KV_A * HEAD_DIM_A
    pw = w_kv.shape[1]
    n_pos_tiles = rope.shape[1] // tm
    idx_scale = (HEAD_DIM_IDX ** -0.5) * (N_HEADS_IDX ** -0.5)
    row = lambda i: (i, 0)
    return pl.pallas_call(
        functools.partial(_proj_kv_kernel, nk=nk, idx_scale=idx_scale),
        grid=(n // tm,),
        in_specs=[pl.BlockSpec((tm, d), row),
                  pl.BlockSpec((1, d), lambda i: (0, 0)),
                  pl.BlockSpec((d, pw), lambda i: (0, 0)),
                  pl.BlockSpec((5, tm, LANES), lambda i: (0, i % n_pos_tiles, 0))],
        out_specs=[pl.BlockSpec((tm, d), row),
                   pl.BlockSpec((tm * N_KV_A, HEAD_DIM_A), row), pl.BlockSpec((tm * N_KV_A, HEAD_DIM_A), row),
                   pl.BlockSpec((tm, HEAD_DIM_IDX), row),
                   pl.BlockSpec((N_HEADS_IDX, tm), lambda i: (0, i)),
                   pl.BlockSpec((tm, nk), row),
                   pl.BlockSpec((N_KV_A * VT_ROWS, tm), lambda i: (0, i)),
                   pl.BlockSpec((tm, LANES), row), pl.BlockSpec((tm, LANES), row)],
        out_shape=[jax.ShapeDtypeStruct((n, d), BF16),
                   jax.ShapeDtypeStruct((n * N_KV_A, HEAD_DIM_A), F32),
                   jax.ShapeDtypeStruct((n * N_KV_A, HEAD_DIM_A), F32),
                   jax.ShapeDtypeStruct((n, HEAD_DIM_IDX), F32),
                   jax.ShapeDtypeStruct((N_HEADS_IDX, n), F32),
                   jax.ShapeDtypeStruct((n, nk), BF16),
                   jax.ShapeDtypeStruct((N_KV_A * VT_ROWS, n), BF16),
                   jax.ShapeDtypeStruct((n, LANES), BF16), jax.ShapeDtypeStruct((n, LANES), BF16)],
        scratch_shapes=[pltpu.VMEM((tm, pw), F32)],
        compiler_params=_cparams("parallel"),
        name="proj_kv",
    )(x, g1.reshape(1, d), w_kv, rope)


def _fori_pairs(n, body, init):
    def two(j, carry):
        return body(2 * j + 1, body(2 * j, carry))
    carry = lax.fori_loop(0, n // 2, two, init)
    return lax.cond(n % 2 == 1, lambda c: body(n - 1, c), lambda c: c, carry)


def _index_kernel(qi_ref, wt_ref, kia_ref, kib_ref, m_ref, s_ref, *,
                  tq, ck, n_ck_total, l_valid, pos_base, causal, top_k):
    i = pl.program_id(1)
    if causal:
        n_c = (i * tq + tq + ck - 1) // ck
    else:
        n_c = n_ck_total
    qpos = pos_base + i * tq + lax.broadcasted_iota(jnp.int32, (1, tq), 1)
    lim = jnp.minimum(qpos - lax.rem(qpos, CHUNK) + CHUNK, l_valid)
    limf = lim.astype(F32)
    kprime = jnp.minimum(float(top_k), limf)
    inf = jnp.float32(jnp.inf)

    def key_index(off):
        return off + lax.broadcasted_iota(jnp.int32, (ck, tq), 0)

    def score_chunk(c, carry):
        rmax, rmin = carry
        off = pl.multiple_of(c * ck, ck)
        ka = kia_ref[pl.ds(off, ck), :]
        kb = kib_ref[pl.ds(off, ck), :]
        acc = jnp.zeros((ck, tq), F32)
        for p in range(N_HEADS_IDX // 2):
            qp = qi_ref[:, p * LANES:(p + 1) * LANES]
            acc += jnp.maximum(_dot_nt(ka, qp), 0.0) * wt_ref[2 * p:2 * p + 1, :]
            acc += jnp.maximum(_dot_nt(kb, qp), 0.0) * wt_ref[2 * p + 1:2 * p + 2, :]
        adm = key_index(off) < lim
        s_ref[pl.ds(off, ck), :] = jnp.where(adm, acc, -inf)
        rmax = jnp.maximum(rmax, jnp.max(jnp.where(adm, acc, -inf), axis=0, keepdims=True))
        rmin = jnp.minimum(rmin, jnp.min(jnp.where(adm, acc, inf), axis=0, keepdims=True))
        return rmax, rmin

    rmax, rmin = _fori_pairs(n_c, score_chunk, (jnp.full((1, tq), -inf, F32), jnp.full((1, tq), inf, F32)))

    fold_rows = min(ck, 16)

    def fold(ind):
        return ind.reshape(ck // fold_rows, fold_rows, tq).sum(axis=0)

    def count_ge(th):
        def body(c, acc):
            off = pl.multiple_of(c * ck, ck)
            return acc + fold(jnp.where(s_ref[pl.ds(off, ck), :] >= th, 1.0, 0.0))
        acc = lax.fori_loop(0, n_c, body, jnp.zeros((fold_rows, tq), F32))
        return acc.sum(axis=0, keepdims=True)

    def cond(st):
        it, lo, hi, cnt, stuck = st
        active = jnp.where(jnp.logical_and(cnt > kprime, stuck < 0.5), 1.0, 0.0)
        return jnp.logical_and(it < MAX_BISECT_ITERS, jnp.max(active) > 0.5)

    def body(st):
        it, lo, hi, cnt, stuck = st
        open_top = hi == inf
        mid = lo + 0.5 * (jnp.where(open_top, rmax, hi) - lo)
        mid = jnp.where(jnp.logical_and(open_top, mid <= lo), rmax, mid)
        c = count_ge(mid)
        ge = c >= kprime
        no_progress = jnp.logical_or(mid <= lo, mid >= hi)
        return (it + 1, jnp.where(ge, mid, lo), jnp.where(ge, hi, mid), jnp.where(ge, c, cnt),
                jnp.where(no_progress, 1.0, stuck))

    _, lo, hi, cnt, _ = lax.while_loop(
        cond, body, (jnp.int32(0), rmin, jnp.full((1, tq), inf, F32), limf, jnp.zeros((1, tq), F32)))

    unresolved = jnp.max(jnp.where(cnt > kprime, 1.0, 0.0)) > 0.5

    @pl.when(jnp.logical_not(unresolved))
    def _():
        def body(c, _):
            off = pl.multiple_of(c * ck, ck)
            m_ref[pl.ds(off, ck), :] = jnp.where(s_ref[pl.ds(off, ck), :] >= lo, 0.0, -inf).astype(m_ref.dtype)
            return 0
        lax.fori_loop(0, n_c, body, 0)

    @pl.when(unresolved)
    def _():
        need = kprime - count_ge(hi)

        def in_tie(blk):
            return jnp.logical_and(blk >= lo, blk < hi)

        def count_tie_below(jcut):
            def body(c, acc):
                off = pl.multiple_of(c * ck, ck)
                blk = s_ref[pl.ds(off, ck), :]
                e = jnp.logical_and(in_tie(blk), key_index(off).astype(F32) < jcut)
                return acc + fold(jnp.where(e, 1.0, 0.0))
            acc = lax.fori_loop(0, n_c, body, jnp.zeros((fold_rows, tq), F32))
            return acc.sum(axis=0, keepdims=True)

        def jbody(_, st):
            jlo, jhi = st
            mid = jnp.floor(0.5 * (jlo + jhi))
            ok = count_tie_below(mid) >= need
            return jnp.where(ok, jlo, mid + 1.0), jnp.where(ok, mid, jhi)

        n_total = n_ck_total * ck
        _, jcut = lax.fori_loop(0, int(math.ceil(math.log2(n_total + 1))), jbody,
                                (jnp.zeros((1, tq), F32), jnp.full((1, tq), float(n_total), F32)))

        def body(c, _):
            off = pl.multiple_of(c * ck, ck)
            blk = s_ref[pl.ds(off, ck), :]
            e = jnp.logical_and(in_tie(blk), key_index(off).astype(F32) < jcut)
            keep = jnp.logical_or(blk >= hi, e)
            m_ref[pl.ds(off, ck), :] = jnp.where(keep, 0.0, -inf).astype(m_ref.dtype)
            return 0
        lax.fori_loop(0, n_c, body, 0)

    def zero_body(c, _):
        off = pl.multiple_of(c * ck, ck)
        m_ref[pl.ds(off, ck), :] = jnp.full((ck, tq), -inf, m_ref.dtype)
        return 0
    lax.fori_loop(n_c, n_ck_total, zero_body, 0)


def _index_mask(p_main, qi_col_block, wt, kia, kib, *, tq, ck, l_valid, pos_base, causal, top_k):
    b, lp, _ = kia.shape
    nq = p_main.shape[0] // (b * tq)
    qw = N_HEADS_IDX * HEAD_DIM_IDX
    return pl.pallas_call(
        functools.partial(_index_kernel, tq=tq, ck=ck, n_ck_total=lp // ck, l_valid=l_valid,
                          pos_base=pos_base, causal=causal, top_k=top_k),
        grid=(b, nq),
        in_specs=[pl.BlockSpec((tq, qw), lambda bi, i: (bi * nq + i, qi_col_block)),
                  pl.BlockSpec((N_HEADS_IDX, tq), lambda bi, i: (0, bi * nq + i)),
                  pl.BlockSpec((None, lp, LANES), lambda bi, i: (bi, 0, 0)),
                  pl.BlockSpec((None, lp, LANES), lambda bi, i: (bi, 0, 0))],
        out_specs=pl.BlockSpec((None, lp, tq), lambda bi, i: (bi, 0, i)),
        out_shape=jax.ShapeDtypeStruct((b, lp, nq * tq), BF16),
        scratch_shapes=[pltpu.VMEM((lp, tq), F32)],
        compiler_params=_cparams("parallel", "parallel"),
        name="index_mask",
    )(p_main, wt, kia, kib)


def _attn_kernel(q_ref, k_ref, vt_ref, m_ref, o_ref, acc_ref, sa_ref, sb_ref, ma_ref, mb_ref, *,
                 tq, tk, n_kt_total, causal, group, kvb):
    i = pl.program_id(1)
    heads = kvb * group
    if causal:
        n_kt = (i * tq + tq + tk - 1) // tk
    else:
        n_kt = n_kt_total
    acc_ref[...] = jnp.zeros_like(acc_ref)

    def scores(kt, bufs):
        s_ref, mx_ref = bufs
        off = pl.multiple_of(kt * tk, tk)
        bias = m_ref[pl.ds(off, tk), :].astype(F32)
        for h in range(heads):
            kv = h // group
            k = k_ref[pl.ds(off, tk), kv * HEAD_DIM_A:(kv + 1) * HEAD_DIM_A]
            s = _dot_nt(k, q_ref[:, h * HEAD_DIM_A:(h + 1) * HEAD_DIM_A]) + bias
            s_ref[h] = s
            mx_ref[h] = jnp.max(s, axis=0, keepdims=True)

    def softmax_pv(kt, bufs, ms):
        s_ref, mx_ref = bufs
        off = pl.multiple_of(kt * tk, tk)
        out = []
        for h in range(heads):
            kv = h // group
            vt = vt_ref[kv * VT_ROWS:(kv + 1) * VT_ROWS, pl.ds(off, tk)]
            m_new = jnp.maximum(ms[h], mx_ref[h])
            alpha = jnp.exp2(ms[h] - m_new)
            p = jnp.exp2(s_ref[h] - m_new).astype(BF16)
            acc_ref[h] = acc_ref[h] * alpha + _dot(vt, p)
            out.append(m_new)
        return tuple(out)

    buf_a, buf_b = (sa_ref, ma_ref), (sb_ref, mb_ref)

    def body(j, ms):
        scores(2 * j + 1, buf_b)
        ms = softmax_pv(2 * j, buf_a, ms)
        scores(2 * j + 2, buf_a)
        return softmax_pv(2 * j + 1, buf_b, ms)

    scores(0, buf_a)
    n_pairs = (n_kt - 1) // 2
    ms = lax.fori_loop(0, n_pairs, body, tuple(jnp.full((1, tq), NEG_BIG, F32) for _ in range(heads)))
    even = n_kt - 1 > 2 * n_pairs

    @pl.when(even)
    def _():
        scores(2 * n_pairs + 1, buf_b)
        softmax_pv(2 * n_pairs + 1, buf_b, softmax_pv(2 * n_pairs, buf_a, ms))

    @pl.when(jnp.logical_not(even))
    def _():
        softmax_pv(2 * n_pairs, buf_a, ms)

    for h in range(heads):
        a = acc_ref[h]
        o = a[:HEAD_DIM_A] / a[HEAD_DIM_A:HEAD_DIM_A + 1]
        o_ref[:, h * HEAD_DIM_A:(h + 1) * HEAD_DIM_A] = o.T.astype(o_ref.dtype)


def _attention(p_main, k_bf, vt_bf, mask_t, *, tq, tk, causal, kvb):
    b, lp, _ = k_bf.shape
    nq = mask_t.shape[2] // tq
    group = N_HEADS_A // N_KV_A
    heads = kvb * group
    gw = heads * HEAD_DIM_A
    return pl.pallas_call(
        functools.partial(_attn_kernel, tq=tq, tk=tk, n_kt_total=lp // tk, causal=causal, group=group,
                          kvb=kvb),
        grid=(b, nq, N_KV_A // kvb),
        in_specs=[pl.BlockSpec((tq, gw), lambda bi, i, g: (bi * nq + i, g)),
                  pl.BlockSpec((None, lp, kvb * HEAD_DIM_A), lambda bi, i, g: (bi, 0, g)),
                  pl.BlockSpec((kvb * VT_ROWS, lp), lambda bi, i, g: (g, bi)),
                  pl.BlockSpec((None, lp, tq), lambda bi, i, g: (bi, 0, i))],
        out_specs=pl.BlockSpec((tq, gw), lambda bi, i, g: (bi * nq + i, g)),
        out_shape=jax.ShapeDtypeStruct((b * nq * tq, N_HEADS_A * HEAD_DIM_A), BF16),
        scratch_shapes=[pltpu.VMEM((heads, VT_ROWS, tq), F32),
                        pltpu.VMEM((heads, tk, tq), F32), pltpu.VMEM((heads, tk, tq), F32),
                        pltpu.VMEM((heads, 1, tq), F32), pltpu.VMEM((heads, 1, tq), F32)],
        compiler_params=_cparams("parallel", "parallel", "arbitrary"),
        name="attention",
    )(p_main, k_bf, vt_bf, mask_t)


def _retention_tables(c):
    log_gamma = jnp.log1p(-(2.0 ** (-5.0 - jnp.arange(N_HEADS_R, dtype=F32))))
    idx = jnp.arange(c, dtype=F32)
    diff = idx[:, None] - idx[None, :]
    decay = jnp.where(diff[None] >= 0,
                      jnp.exp(jnp.maximum(diff, 0.0)[None] * log_gamma[:, None, None]), 0.0)
    cross = jnp.exp((idx + 1.0)[None, :] * log_gamma[:, None])
    kdec = jnp.exp((c - 1.0 - idx)[None, :] * log_gamma[:, None])
    full = jnp.exp(c * log_gamma)
    bc = lambda a, w: jnp.broadcast_to(a[..., None], a.shape + (w,))
    return decay, bc(cross, KEY_DIM_R), bc(kdec, KEY_DIM_R), bc(full[:, None], VAL_DIM_R)


def _retention_kernel(q_ref, k_ref, v_ref, g_ref, dec_ref, qsc_ref, ksc_ref, gc_ref, s0_ref,
                      o_ref, st_ref):
    @pl.when(pl.program_id(1) == 0)
    def _():
        st_ref[...] = s0_ref[...]

    for h in range(N_HEADS_R):
        ks = slice(h * KEY_DIM_R, (h + 1) * KEY_DIM_R)
        vs = slice(h * VAL_DIM_R, (h + 1) * VAL_DIM_R)
        q = q_ref[:, ks]
        k = k_ref[:, ks]
        v = v_ref[:, vs]
        st = st_ref[h]
        inner = _dot_nt(q, k) * dec_ref[h]
        qd = (q.astype(F32) * qsc_ref[h]).astype(BF16)
        o = _dot(inner.astype(BF16), v) + _dot(qd, st.astype(BF16))
        kdt = (k.astype(F32) * ksc_ref[h]).T.astype(BF16)
        st_ref[h] = st * gc_ref[h] + _dot(kdt, v)
        o = o * lax.rsqrt(jnp.mean(o * o, axis=-1, keepdims=True) + EPS)
        o_ref[:, vs] = (o * g_ref[:, vs].astype(F32)).astype(o_ref.dtype)


def _retention(p_main, cols, state0, c):
    b = state0.shape[0]
    nc = p_main.shape[0] // (b * c)
    kw = N_HEADS_R * KEY_DIM_R
    vw = N_HEADS_R * VAL_DIM_R
    dec, qsc, ksc, gcs = _retention_tables(c)
    qb, kb, vb, gb = cols
    const3 = lambda bi, ci: (0, 0, 0)
    st_spec = pl.BlockSpec((None, N_HEADS_R, KEY_DIM_R, VAL_DIM_R), lambda bi, ci: (bi, 0, 0, 0))
    return pl.pallas_call(
        _retention_kernel,
        grid=(b, nc),
        in_specs=[pl.BlockSpec((c, kw), lambda bi, ci: (bi * nc + ci, qb)),
                  pl.BlockSpec((c, kw), lambda bi, ci: (bi * nc + ci, kb)),
                  pl.BlockSpec((c, vw), lambda bi, ci: (bi * nc + ci, vb)),
                  pl.BlockSpec((c, vw), lambda bi, ci: (bi * nc + ci, gb)),
                  pl.BlockSpec(dec.shape, const3), pl.BlockSpec(qsc.shape, const3),
                  pl.BlockSpec(ksc.shape, const3), pl.BlockSpec(gcs.shape, const3),
                  st_spec],
        out_specs=[pl.BlockSpec((c, vw), lambda bi, ci: (bi * nc + ci, 0)), st_spec],
        out_shape=[jax.ShapeDtypeStruct((b * nc * c, vw), BF16),
                   jax.ShapeDtypeStruct(state0.shape, F32)],
        compiler_params=_cparams("parallel", "arbitrary"),
        name="retention",
    )(p_main, p_main, p_main, p_main, dec, qsc, ksc, gcs, state0)


def _merge_kernel(oa_ref, ob_ref, wa_ref, wb_ref, sa_ref, sb_ref, o_ref):
    for b in range(o_ref.shape[1] // MXU_COLS):
        sl = slice(b * MXU_COLS, (b + 1) * MXU_COLS)
        m = (_dot(oa_ref[...], wa_ref[:, sl]) * sa_ref[:, sl].astype(F32)
             + _dot(ob_ref[...], wb_ref[:, sl]) * sb_ref[:, sl].astype(F32))
        o_ref[:, sl] = m.astype(o_ref.dtype)


def _merge(o_a, o_b, w_pa, w_pb, p_main, ga_off, gb_off, tm, tn):
    n, wa = o_a.shape
    wb = o_b.shape[1]
    d = w_pa.shape[1]
    return pl.pallas_call(
        _merge_kernel,
        grid=(n // tm, d // tn),
        in_specs=[pl.BlockSpec((tm, wa), lambda i, j: (i, 0)),
                  pl.BlockSpec((tm, wb), lambda i, j: (i, 0)),
                  pl.BlockSpec((wa, tn), lambda i, j: (0, j)),
                  pl.BlockSpec((wb, tn), lambda i, j: (0, j)),
                  pl.BlockSpec((tm, tn), lambda i, j: (i, ga_off // tn + j)),
                  pl.BlockSpec((tm, tn), lambda i, j: (i, gb_off // tn + j))],
        out_specs=pl.BlockSpec((tm, tn), lambda i, j: (i, j)),
        out_shape=jax.ShapeDtypeStruct((n, d), BF16),
        compiler_params=_cparams("parallel", "arbitrary"),
        name="merge",
    )(o_a, o_b, w_pa, w_pb, p_main, p_main)


def _out_proj_kernel(x_ref, m_ref, w_ref, x1_ref):
    x1_ref[...] = x_ref[...] + _dot(m_ref[...], w_ref[...])


def _out_proj(x, merged, w_o, tm):
    n, d = x.shape
    row = lambda i: (i, 0)
    return pl.pallas_call(
        _out_proj_kernel,
        grid=(n // tm,),
        in_specs=[pl.BlockSpec((tm, d), row), pl.BlockSpec((tm, d), row),
                  pl.BlockSpec((d, d), lambda i: (0, 0))],
        out_specs=pl.BlockSpec((tm, d), row),
        out_shape=jax.ShapeDtypeStruct((n, d), F32),
        compiler_params=_cparams("parallel"),
        name="out_proj",
    )(x, merged, w_o)


def _ffn_kernel(x1_ref, g2_ref, wg_ref, wu_ref, wd_ref, g_ref, y_ref, h_ref, *, final_norm):
    c = pl.program_id(1)

    @pl.when(c == 0)
    def _():
        x1 = x1_ref[...]
        y_ref[...] = x1
        h_ref[...] = _rmsnorm_rows(x1, g2_ref[...]).astype(h_ref.dtype)

    acts = []
    for b in range(wg_ref.shape[1] // MXU_COLS):
        sl = slice(b * MXU_COLS, (b + 1) * MXU_COLS)
        a = _dot(h_ref[...], wg_ref[:, sl])
        u = _dot(h_ref[...], wu_ref[:, sl])
        acts.append((a / (1.0 + jnp.exp(-a)) * u).astype(BF16))
    upd = _dot(acts[0], wd_ref[0:MXU_COLS, :])
    for b in range(1, len(acts)):
        upd += _dot(acts[b], wd_ref[b * MXU_COLS:(b + 1) * MXU_COLS, :])
    y_ref[...] += upd

    if final_norm:
        @pl.when(c == pl.num_programs(1) - 1)
        def _():
            y_ref[...] = _rmsnorm_rows(y_ref[...], g_ref[...])


def _ffn(x1, g2, wg, wu, wd, gf, tm, tc, final_norm):
    n, d = x1.shape
    f = wg.shape[1]
    row = lambda i, c: (i, 0)
    vec = pl.BlockSpec((1, d), lambda i, c: (0, 0))
    return pl.pallas_call(
        functools.partial(_ffn_kernel, final_norm=final_norm),
        grid=(n // tm, f // tc),
        in_specs=[pl.BlockSpec((tm, d), row, pipeline_mode=pl.Buffered(1)), vec,
                  pl.BlockSpec((d, tc), lambda i, c: (0, c)),
                  pl.BlockSpec((d, tc), lambda i, c: (0, c)),
                  pl.BlockSpec((tc, d), lambda i, c: (c, 0)), vec],
        out_specs=pl.BlockSpec((tm, d), row),
        out_shape=jax.ShapeDtypeStruct((n, d), F32),
        scratch_shapes=[pltpu.VMEM((tm, d), BF16)],
        compiler_params=_cparams("parallel", "arbitrary"),
        name="ffn",
    )(x1, g2.reshape(1, d), wg, wu, wd, gf.reshape(1, d))


class _MainLayout:
    def __init__(self, d_model):
        wa = N_HEADS_A * HEAD_DIM_A
        wb = N_HEADS_R * VAL_DIM_R
        kr = N_HEADS_R * KEY_DIM_R
        qi = N_HEADS_IDX * HEAD_DIM_IDX
        order = [("qa", wa, "rope128", HEAD_DIM_A ** -0.5 * LOG2E), ("vr", wb, "plain", 1.0),
                 ("gr", wb, "silu", 1.0), ("ga", d_model, "sigmoid", 1.0), ("gb", d_model, "sigmoid", 1.0),
                 ("qi", qi, "rope64", 1.0), ("qr", kr, "rope128", 1.0),
                 ("kr", kr, "rope128", KEY_DIM_R ** -0.5)]
        self.off, self.width = {}, {}
        self.order = order
        o = 0
        for name, w, _, _ in order:
            self.off[name], self.width[name] = o, w
            o += w
        self.total = o

    def groups(self, tn):
        out = []
        for name, w, mode, scale in self.order:
            assert self.off[name] % tn == 0 and w % tn == 0
            out.append((self.off[name] // tn, (self.off[name] + w) // tn, mode, scale))
        return tuple(out)

    def block(self, name, width=None):
        width = width or self.width[name]
        assert self.off[name] % width == 0
        return self.off[name] // width


def _split_w_in(w, d_model):
    wa = N_HEADS_A * HEAD_DIM_A
    nk = N_KV_A * HEAD_DIM_A
    wb = N_HEADS_R * VAL_DIM_R
    kr = N_HEADS_R * KEY_DIM_R
    names = ("qa", "ka", "va", "qi", "ki", "wi", "qr", "kr", "vr", "gr", "ga", "gb")
    widths = (wa, nk, nk, N_HEADS_IDX * HEAD_DIM_IDX, HEAD_DIM_IDX, N_HEADS_IDX, kr, kr, wb, wb,
              d_model, d_model)
    parts, o = {}, 0
    for name, wd in zip(names, widths):
        parts[name] = w[:, o:o + wd]
        o += wd
    assert o == w.shape[1]
    return parts


def _tile(n, pref):
    t = min(n, pref)
    assert n % t == 0
    return t


def _dense_pre(x2, pos_rows, lw, lay):
    n = x2.shape[0]
    rope = _rope_tables(pos_rows)
    h, *kv = _proj_kv(x2, lw["g1"], lw["w_kv"], rope, _tile(n, 512))
    tn = 1024
    p_main = _proj_main(h, lw["w_main"], rope, lay.groups(tn), _tile(n, 1024), tn)
    return p_main, kv


def _dense_post(x2, o_a, o_b, p_main, lw, lay, gf, final_norm):
    n = x2.shape[0]
    tm = _tile(n, 1024)
    merged = _merge(o_a, o_b, lw["w_pa"], lw["w_pb"], p_main, lay.off["ga"], lay.off["gb"], tm, 512)
    x1 = _out_proj(x2, merged, lw["w_o"], _tile(n, 512))
    return _ffn(x1, lw["g2"], lw["wg"], lw["wu"], lw["wd"], gf, tm, 512, final_norm)


def _prompt_layer(xp, lw, lay, gf, final_norm):
    b, t, d = xp.shape
    x2 = xp.reshape(b * t, d)
    p_main, (ka, va, ki, wt, ka_bf, vt_bf, kia, kib) = _dense_pre(x2, jnp.arange(t, dtype=jnp.int32), lw, lay)
    top_k = min(TOPK_MAX, t // 4)
    tq, ck = _tile(t, 256), _tile(t, 512)
    mask_t = _index_mask(p_main, lay.block("qi"), wt, kia.reshape(b, t, LANES), kib.reshape(b, t, LANES),
                         tq=_tile(t, 512), ck=ck, l_valid=t, pos_base=0, causal=True, top_k=top_k)
    nk = N_KV_A * HEAD_DIM_A
    o_a = _attention(p_main, ka_bf.reshape(b, t, nk), vt_bf, mask_t, tq=tq, tk=ck, causal=True, kvb=2)
    state0 = jnp.zeros((b, N_HEADS_R, KEY_DIM_R, VAL_DIM_R), F32)
    cols = (lay.block("qr"), lay.block("kr"), lay.block("vr"), lay.block("gr"))
    o_b, st = _retention(p_main, cols, state0, _tile(t, 256))
    y = _dense_post(x2, o_a, o_b, p_main, lw, lay, gf, final_norm)
    return (y.reshape(b, t, d), ka.reshape(b, t, N_KV_A, HEAD_DIM_A), va.reshape(b, t, N_KV_A, HEAD_DIM_A),
            ki.reshape(b, t, HEAD_DIM_IDX), st)


def _sample_layer(xs, cache_k, cache_v, cache_i, state, lw, lay, gf, final_norm):
    b, t, d = xs.shape
    past = cache_k.shape[1]
    nk = N_KV_A * HEAD_DIM_A
    x2 = xs.reshape(b * t, d)
    pos = jnp.tile(past + jnp.arange(t, dtype=jnp.int32), b)
    p_main, (ka, va, ki, wt, ka_bf, vt_bf, kia, kib) = _dense_pre(x2, pos, lw, lay)

    tq = LANES
    l_valid = past + t
    tk = 3 * LANES
    lp = -(-l_valid // tk) * tk
    pq = jnp.pad(p_main.reshape(b, t, -1), ((0, 0), (0, tq - t), (0, 0)), mode="edge").reshape(b * tq, -1)
    wtq = jnp.pad(wt.reshape(N_HEADS_IDX, b, t), ((0, 0), (0, 0), (0, tq - t)),
                  mode="edge").reshape(N_HEADS_IDX, b * tq)
    kpad = ((0, 0), (0, lp - l_valid), (0, 0))
    ci = cache_i.astype(BF16)
    zi = jnp.zeros_like(ci)
    kia_all = jnp.pad(jnp.concatenate([jnp.concatenate([ci, zi], -1), kia.reshape(b, t, LANES)], 1), kpad)
    kib_all = jnp.pad(jnp.concatenate([jnp.concatenate([zi, ci], -1), kib.reshape(b, t, LANES)], 1), kpad)
    k_all = jnp.pad(jnp.concatenate([cache_k.reshape(b, past, nk).astype(BF16), ka_bf.reshape(b, t, nk)], 1), kpad)
    vt_new = vt_bf.reshape(N_KV_A, VT_ROWS, b, t)
    vt_cache = cache_v.reshape(b, past, N_KV_A, HEAD_DIM_A).transpose(2, 3, 0, 1).astype(BF16)
    ones_rows = jnp.zeros((N_KV_A, VT_ROWS - HEAD_DIM_A, b, past), BF16).at[:, 0].set(1.0)
    vt_all = jnp.concatenate([jnp.concatenate([vt_cache, ones_rows], 1), vt_new], 3)
    vt_all = jnp.pad(vt_all, ((0, 0), (0, 0), (0, 0), (0, lp - l_valid))).reshape(N_KV_A * VT_ROWS, b * lp)

    top_k = min(TOPK_MAX, l_valid // 4)
    mask_t = _index_mask(pq, lay.block("qi"), wtq, kia_all, kib_all, tq=tq, ck=tk, l_valid=l_valid,
                         pos_base=past, causal=False, top_k=top_k)
    o_a = _attention(pq, k_all, vt_all, mask_t, tq=tq, tk=tk, causal=False, kvb=1)
    o_a = o_a.reshape(b, tq, -1)[:, :t].reshape(b * t, -1)
    cols = (lay.block("qr"), lay.block("kr"), lay.block("vr"), lay.block("gr"))
    o_b, st = _retention(p_main, cols, state.astype(F32), t)
    y = _dense_post(x2, o_a, o_b, p_main, lw, lay, gf, final_norm)
    return (y.reshape(b, t, d), ka.reshape(b, t, N_KV_A, HEAD_DIM_A), va.reshape(b, t, N_KV_A, HEAD_DIM_A),
            ki.reshape(b, t, HEAD_DIM_IDX), st)


def kernel(x_prompt, x_sample, cache_k, cache_v, cache_idx_k, state_ret, norm1_g, w_in, w_pa, w_pb, w_o,
           norm2_g, w_ffn_gate, w_ffn_up, w_ffn_down, norm_f_g):
    assert HEAD_DIM_A == LANES and HEAD_DIM_IDX * 2 == LANES and N_HEADS_IDX % 2 == 0
    depth, d_model = norm1_g.shape
    lay = _MainLayout(d_model)
    xp, xs = x_prompt, x_sample
    outs = [[] for _ in range(8)]
    for l in range(depth):
        parts = _split_w_in(w_in[l], d_model)
        pad = jnp.zeros((d_model, LANES - HEAD_DIM_IDX - N_HEADS_IDX), F32)
        lw = {
            "g1": norm1_g[l], "g2": norm2_g[l],
            "w_main": jnp.concatenate([parts[name] for name, _, _, _ in lay.order], axis=1).astype(BF16),
            "w_kv": jnp.concatenate([parts["ka"], parts["va"], parts["ki"], parts["wi"], pad], axis=1).astype(BF16),
            "w_pa": w_pa[l].astype(BF16), "w_pb": w_pb[l].astype(BF16), "w_o": w_o[l].astype(BF16),
            "wg": w_ffn_gate[l].astype(BF16), "wu": w_ffn_up[l].astype(BF16), "wd": w_ffn_down[l].astype(BF16),
        }
        last = l == depth - 1
        xp, kp, vp, ip, sp = _prompt_layer(xp, lw, lay, norm_f_g, last)
        xs, ks, vs, isl, ss = _sample_layer(xs, cache_k[l], cache_v[l], cache_idx_k[l], state_ret[l],
                                            lw, lay, norm_f_g, last)
        for lst, val in zip(outs, (kp, vp, ip, sp, ks, vs, isl, ss)):
            lst.append(val)
    stacked = [jnp.stack(o) for o in outs]
    return (xp, xs, stacked[0], stacked[1], stacked[2], stacked[3].astype(x_prompt.dtype),
            stacked[4], stacked[5], stacked[6], stacked[7].astype(state_ret.dtype))
```

```python
import functools
import math

import jax
import jax.numpy as jnp
from jax import lax
from jax.experimental import pallas as pl
from jax.experimental.pallas import tpu as pltpu

F32 = jnp.float32
BF16 = jnp.bfloat16

CHUNK = 64
EPS = 1e-6
ROPE_THETA = 10000.0
N_HEADS_A = 16
N_KV_A = 4
HEAD_DIM_A = 128
N_HEADS_IDX = 16
HEAD_DIM_IDX = 64
TOPK_MAX = 256
N_HEADS_R = 8
KEY_DIM_R = 128
VAL_DIM_R = 256

LANES = 128
MXU_COLS = 256
V7X_VMEM_BYTES = 64 * 1024 * 1024
VMEM_LIMIT = V7X_VMEM_BYTES - 8 * 1024 * 1024
BF16_SUBLANES = 16
VT_ROWS = HEAD_DIM_A + BF16_SUBLANES
LOG2E = math.log2(math.e)
MAX_BISECT_ITERS = 300
NEG_BIG = float(jnp.finfo(jnp.float32).min)


def _cparams(*sem):
    return pltpu.CompilerParams(dimension_semantics=sem, vmem_limit_bytes=VMEM_LIMIT)


def _dot(a, b):
    return jnp.dot(a, b, preferred_element_type=F32)


def _dot_nt(a, b):
    return lax.dot_general(a, b, (((1,), (1,)), ((), ())), preferred_element_type=F32)


def _rmsnorm_rows(x, g):
    return x * lax.rsqrt(jnp.mean(x * x, axis=-1, keepdims=True) + EPS) * g


def _rope_tables(pos):
    posf = pos.astype(F32)[:, None]

    def cs(half):
        inv_freq = ROPE_THETA ** (-jnp.arange(half, dtype=F32) / half)
        ang = posf * inv_freq[None, :]
        return jnp.cos(ang), jnp.sin(ang)

    c, s = cs(HEAD_DIM_A // 2)
    c128 = jnp.concatenate([c, c], axis=1)
    s128 = jnp.concatenate([-s, s], axis=1)
    c, s = cs(HEAD_DIM_IDX // 2)
    z = jnp.zeros_like(s)
    c64 = jnp.tile(c, (1, 4))
    a64 = jnp.tile(jnp.concatenate([-s, z], axis=1), (1, 2))
    b64 = jnp.tile(jnp.concatenate([z, s], axis=1), (1, 2))
    return jnp.stack([c128, s128, c64, a64, b64])


def _rope128(x, rope_ref):
    return x * rope_ref[0] + pltpu.roll(x, 64, 1) * rope_ref[1]


def _rope64(x, rope_ref):
    return (x * rope_ref[2] + pltpu.roll(x, 96, 1) * rope_ref[3]
            + pltpu.roll(x, 32, 1) * rope_ref[4])


def _proj_main_kernel(h_ref, w_ref, rope_ref, o_ref, *, groups, tn):
    j = pl.program_id(1)
    for lo, hi, mode, scale in groups:
        @pl.when(jnp.logical_and(j >= lo, j < hi))
        def _(mode=mode, scale=scale):
            for b in range(tn // MXU_COLS):
                acc = _dot(h_ref[...], w_ref[:, b * MXU_COLS:(b + 1) * MXU_COLS])
                for c in range(MXU_COLS // LANES):
                    x = acc[:, c * LANES:(c + 1) * LANES]
                    if mode == "rope128":
                        y = _rope128(x, rope_ref)
                    elif mode == "rope64":
                        y = _rope64(x, rope_ref)
                    elif mode == "sigmoid":
                        y = 1.0 / (1.0 + jnp.exp(-x))
                    elif mode == "silu":
                        y = x / (1.0 + jnp.exp(-x))
                    else:
                        y = x
                    if scale != 1.0:
                        y = y * scale
                    col = b * MXU_COLS + c * LANES
                    o_ref[:, col:col + LANES] = y.astype(o_ref.dtype)


def _proj_main(h, w_main, rope, groups, tm, tn):
    n, d = h.shape
    p = w_main.shape[1]
    n_pos_tiles = rope.shape[1] // tm
    return pl.pallas_call(
        functools.partial(_proj_main_kernel, groups=groups, tn=tn),
        grid=(n // tm, p // tn),
        in_specs=[pl.BlockSpec((tm, d), lambda i, j: (i, 0)),
                  pl.BlockSpec((d, tn), lambda i, j: (0, j)),
                  pl.BlockSpec((5, tm, LANES), lambda i, j: (0, i % n_pos_tiles, 0))],
        out_specs=pl.BlockSpec((tm, tn), lambda i, j: (i, j)),
        out_shape=jax.ShapeDtypeStruct((n, p), BF16),
        compiler_params=_cparams("parallel", "arbitrary"),
        name="proj_main",
    )(h, w_main, rope)


def _proj_kv_kernel(x_ref, g_ref, w_ref, rope_ref, h_ref, ka_ref, va_ref, ki_ref, wt_ref,
                    kab_ref, vtb_ref, kia_ref, kib_ref, acc_ref, *, nk, idx_scale):
    h_ref[...] = _rmsnorm_rows(x_ref[...], g_ref[...]).astype(h_ref.dtype)
    acc_ref[...] = _dot(h_ref[...], w_ref[...])
    tm = acc_ref.shape[0]
    for c in range(N_KV_A):
        sl = slice(c * LANES, (c + 1) * LANES)
        y = _rope128(acc_ref[:, sl], rope_ref)
        ka_ref[pl.ds(c, tm, stride=N_KV_A), :] = y
        kab_ref[:, sl] = y.astype(BF16)
        va_ref[pl.ds(c, tm, stride=N_KV_A), :] = acc_ref[:, nk + c * LANES:nk + (c + 1) * LANES]
    ones_rows = jnp.where(lax.broadcasted_iota(jnp.int32, (VT_ROWS - HEAD_DIM_A, tm), 0) == 0, 1.0, 0.0)
    for g in range(N_KV_A):
        vg = acc_ref[:, nk + g * HEAD_DIM_A:nk + (g + 1) * HEAD_DIM_A]
        vtb_ref[g * VT_ROWS:g * VT_ROWS + HEAD_DIM_A, :] = vg.T.astype(BF16)
        vtb_ref[g * VT_ROWS + HEAD_DIM_A:(g + 1) * VT_ROWS, :] = ones_rows.astype(BF16)
    z = acc_ref[:, 2 * nk:2 * nk + LANES]
    y = _rope64(z, rope_ref)
    ki_ref[...] = y[:, :HEAD_DIM_IDX]
    lane = lax.broadcasted_iota(jnp.int32, y.shape, 1)
    ya = jnp.where(lane < HEAD_DIM_IDX, y, 0.0)
    kia_ref[...] = ya.astype(BF16)
    kib_ref[...] = pltpu.roll(ya, HEAD_DIM_IDX, 1).astype(BF16)
    wt_ref[...] = (z * idx_scale).T[HEAD_DIM_IDX:HEAD_DIM_IDX + N_HEADS_IDX, :]


def _proj_kv(x, g1, w_kv, rope, tm):
    n, d = x.shape
    nk = N_KV_A * HEAD_DIM_A
    pw = w_kv.shape[1]
    n_pos_tiles = rope.shape[1] // tm
    idx_scale = (HEAD_DIM_IDX ** -0.5) * (N_HEADS_IDX ** -0.5)
    row = lambda i: (i, 0)
    return pl.pallas_call(
        functools.partial(_proj_kv_kernel, nk=nk, idx_scale=idx_scale),
        grid=(n // tm,),
        in_specs=[pl.BlockSpec((tm, d), row),
                  pl.BlockSpec((1, d), lambda i: (0, 0)),
                  pl.BlockSpec((d, pw), lambda i: (0, 0)),
                  pl.BlockSpec((5, tm, LANES), lambda i: (0, i % n_pos_tiles, 0))],
        out_specs=[pl.BlockSpec((tm, d), row),
                   pl.BlockSpec((tm * N_KV_A, HEAD_DIM_A), row), pl.BlockSpec((tm * N_KV_A, HEAD_DIM_A), row),
                   pl.BlockSpec((tm, HEAD_DIM_IDX), row),
                   pl.BlockSpec((N_HEADS_IDX, tm), lambda i: (0, i)),
                   pl.BlockSpec((tm, nk), row),
                   pl.BlockSpec((N_KV_A * VT_ROWS, tm), lambda i: (0, i)),
                   pl.BlockSpec((tm, LANES), row), pl.BlockSpec((tm, LANES), row)],
        out_shape=[jax.ShapeDtypeStruct((n, d), BF16),
                   jax.ShapeDtypeStruct((n * N_KV_A, HEAD_DIM_A), F32),
                   jax.ShapeDtypeStruct((n * N_KV_A, HEAD_DIM_A), F32),
                   jax.ShapeDtypeStruct((n, HEAD_DIM_IDX), F32),
                   jax.ShapeDtypeStruct((N_HEADS_IDX, n), F32),
                   jax.ShapeDtypeStruct((n, nk), BF16),
                   jax.ShapeDtypeStruct((N_KV_A * VT_ROWS, n), BF16),
                   jax.ShapeDtypeStruct((n, LANES), BF16), jax.ShapeDtypeStruct((n, LANES), BF16)],
        scratch_shapes=[pltpu.VMEM((tm, pw), F32)],
        compiler_params=_cparams("parallel"),
        name="proj_kv",
    )(x, g1.reshape(1, d), w_kv, rope)


def _fori_pairs(n, body, init):
    def two(j, carry):
        return body(2 * j + 1, body(2 * j, carry))
    carry = lax.fori_loop(0, n // 2, two, init)
    return lax.cond(n % 2 == 1, lambda c: body(n - 1, c), lambda c: c, carry)


def _index_kernel(qi_ref, wt_ref, kia_ref, kib_ref, m_ref, s_ref, *,
                  tq, ck, n_ck_total, l_valid, pos_base, causal, top_k):
    i = pl.program_id(1)
    if causal:
        n_c = (i * tq + tq + ck - 1) // ck
    else:
        n_c = n_ck_total
    qpos = pos_base + i * tq + lax.broadcasted_iota(jnp.int32, (1, tq), 1)
    lim = jnp.minimum(qpos - lax.rem(qpos, CHUNK) + CHUNK, l_valid)
    limf = lim.astype(F32)
    kprime = jnp.minimum(float(top_k), limf)
    inf = jnp.float32(jnp.inf)

    def key_index(off):
        return off + lax.broadcasted_iota(jnp.int32, (ck, tq), 0)

    def score_chunk(c, carry):
        rmax, rmin = carry
        off = pl.multiple_of(c * ck, ck)
        ka = kia_ref[pl.ds(off, ck), :]
        kb = kib_ref[pl.ds(off, ck), :]
        acc = jnp.zeros((ck, tq), F32)
        for p in range(N_HEADS_IDX // 2):
            qp = qi_ref[:, p * LANES:(p + 1) * LANES]
            acc += jnp.maximum(_dot_nt(ka, qp), 0.0) * wt_ref[2 * p:2 * p + 1, :]
            acc += jnp.maximum(_dot_nt(kb, qp), 0.0) * wt_ref[2 * p + 1:2 * p + 2, :]
        adm = key_index(off) < lim
        s_ref[pl.ds(off, ck), :] = jnp.where(adm, acc, -inf)
        rmax = jnp.maximum(rmax, jnp.max(jnp.where(adm, acc, -inf), axis=0, keepdims=True))
        rmin = jnp.minimum(rmin, jnp.min(jnp.where(adm, acc, inf), axis=0, keepdims=True))
        return rmax, rmin

    rmax, rmin = _fori_pairs(n_c, score_chunk, (jnp.full((1, tq), -inf, F32), jnp.full((1, tq), inf, F32)))

    fold_rows = min(ck, 16)

    def fold(ind):
        return ind.reshape(ck // fold_rows, fold_rows, tq).sum(axis=0)

    lane_w = min(tq, MXU_COLS)

    def count_ge(th):
        def body(c, accs):
            off = pl.multiple_of(c * ck, ck)
            out = []
            for w, acc in enumerate(accs):
                sl = slice(w * lane_w, (w + 1) * lane_w)
                blk = s_ref[pl.ds(off, ck), sl].reshape(ck // fold_rows, fold_rows, lane_w)
                out.append(acc + jnp.where(blk >= th[:, sl][None], 1.0, 0.0).sum(axis=0))
            return tuple(out)
        accs = lax.fori_loop(0, n_c, body,
                             tuple(jnp.zeros((fold_rows, lane_w), F32) for _ in range(tq // lane_w)))
        return jnp.concatenate(accs, axis=1).sum(axis=0, keepdims=True)

    def cond(st):
        it, lo, hi, cnt, stuck = st
        active = jnp.where(jnp.logical_and(cnt > kprime, stuck < 0.5), 1.0, 0.0)
        return jnp.logical_and(it < MAX_BISECT_ITERS, jnp.max(active) > 0.5)

    def body(st):
        it, lo, hi, cnt, stuck = st
        open_top = hi == inf
        mid = lo + 0.5 * (jnp.where(open_top, rmax, hi) - lo)
        mid = jnp.where(jnp.logical_and(open_top, mid <= lo), rmax, mid)
        c = count_ge(mid)
        ge = c >= kprime
        no_progress = jnp.logical_or(mid <= lo, mid >= hi)
        return (it + 1, jnp.where(ge, mid, lo), jnp.where(ge, hi, mid), jnp.where(ge, c, cnt),
                jnp.where(no_progress, 1.0, stuck))

    _, lo, hi, cnt, _ = lax.while_loop(
        cond, body, (jnp.int32(0), rmin, jnp.full((1, tq), inf, F32), limf, jnp.zeros((1, tq), F32)))

    unresolved = jnp.max(jnp.where(cnt > kprime, 1.0, 0.0)) > 0.5

    @pl.when(jnp.logical_not(unresolved))
    def _():
        def body(c, _):
            off = pl.multiple_of(c * ck, ck)
            m_ref[pl.ds(off, ck), :] = jnp.where(s_ref[pl.ds(off, ck), :] >= lo, 0.0, -inf).astype(m_ref.dtype)
            return 0
        lax.fori_loop(0, n_c, body, 0)

    @pl.when(unresolved)
    def _():
        need = kprime - count_ge(hi)

        def in_tie(blk):
            return jnp.logical_and(blk >= lo, blk < hi)

        def count_tie_below(jcut):
            def body(c, acc):
                off = pl.multiple_of(c * ck, ck)
                blk = s_ref[pl.ds(off, ck), :]
                e = jnp.logical_and(in_tie(blk), key_index(off).astype(F32) < jcut)
                return acc + fold(jnp.where(e, 1.0, 0.0))
            acc = lax.fori_loop(0, n_c, body, jnp.zeros((fold_rows, tq), F32))
            return acc.sum(axis=0, keepdims=True)

        def jbody(_, st):
            jlo, jhi = st
            mid = jnp.floor(0.5 * (jlo + jhi))
            ok = count_tie_below(mid) >= need
            return jnp.where(ok, jlo, mid + 1.0), jnp.where(ok, mid, jhi)

        n_total = n_ck_total * ck
        _, jcut = lax.fori_loop(0, int(math.ceil(math.log2(n_total + 1))), jbody,
                                (jnp.zeros((1, tq), F32), jnp.full((1, tq), float(n_total), F32)))

        def body(c, _):
            off = pl.multiple_of(c * ck, ck)
            blk = s_ref[pl.ds(off, ck), :]
            e = jnp.logical_and(in_tie(blk), key_index(off).astype(F32) < jcut)
            keep = jnp.logical_or(blk >= hi, e)
            m_ref[pl.ds(off, ck), :] = jnp.where(keep, 0.0, -inf).astype(m_ref.dtype)
            return 0
        lax.fori_loop(0, n_c, body, 0)

    def zero_body(c, _):
        off = pl.multiple_of(c * ck, ck)
        m_ref[pl.ds(off, ck), :] = jnp.full((ck, tq), -inf, m_ref.dtype)
        return 0
    lax.fori_loop(n_c, n_ck_total, zero_body, 0)


def _index_mask(p_main, qi_col_block, wt, kia, kib, *, tq, ck, l_valid, pos_base, causal, top_k):
    b, lp, _ = kia.shape
    nq = p_main.shape[0] // (b * tq)
    qw = N_HEADS_IDX * HEAD_DIM_IDX
    return pl.pallas_call(
        functools.partial(_index_kernel, tq=tq, ck=ck, n_ck_total=lp // ck, l_valid=l_valid,
                          pos_base=pos_base, causal=causal, top_k=top_k),
        grid=(b, nq),
        in_specs=[pl.BlockSpec((tq, qw), lambda bi, i: (bi * nq + i, qi_col_block)),
                  pl.BlockSpec((N_HEADS_IDX, tq), lambda bi, i: (0, bi * nq + i)),
                  pl.BlockSpec((None, lp, LANES), lambda bi, i: (bi, 0, 0)),
                  pl.BlockSpec((None, lp, LANES), lambda bi, i: (bi, 0, 0))],
        out_specs=pl.BlockSpec((None, lp, tq), lambda bi, i: (bi, 0, i)),
        out_shape=jax.ShapeDtypeStruct((b, lp, nq * tq), BF16),
        scratch_shapes=[pltpu.VMEM((lp, tq), F32)],
        compiler_params=_cparams("parallel", "parallel"),
        name="index_mask",
    )(p_main, wt, kia, kib)


def _attn_kernel(q_ref, k_ref, vt_ref, m_ref, o_ref, acc_ref, sa_ref, sb_ref, ma_ref, mb_ref, *,
                 tq, tk, n_kt_total, causal, group, kvb):
    i = pl.program_id(1)
    heads = kvb * group
    if causal:
        n_kt = (i * tq + tq + tk - 1) // tk
    else:
        n_kt = n_kt_total
    acc_ref[...] = jnp.zeros_like(acc_ref)

    def scores(kt, bufs):
        s_ref, mx_ref = bufs
        off = pl.multiple_of(kt * tk, tk)
        bias = m_ref[pl.ds(off, tk), :].astype(F32)
        for h in range(heads):
            kv = h // group
            k = k_ref[pl.ds(off, tk), kv * HEAD_DIM_A:(kv + 1) * HEAD_DIM_A]
            s = _dot_nt(k, q_ref[:, h * HEAD_DIM_A:(h + 1) * HEAD_DIM_A]) + bias
            s_ref[h] = s
            mx_ref[h] = jnp.max(s, axis=0, keepdims=True)

    def softmax_pv(kt, bufs, ms):
        s_ref, mx_ref = bufs
        off = pl.multiple_of(kt * tk, tk)
        out = []
        for h in range(heads):
            kv = h // group
            vt = vt_ref[kv * VT_ROWS:(kv + 1) * VT_ROWS, pl.ds(off, tk)]
            m_new = jnp.maximum(ms[h], mx_ref[h])
            alpha = jnp.exp2(ms[h] - m_new)
            p = jnp.exp2(s_ref[h] - m_new).astype(BF16)
            acc_ref[h] = acc_ref[h] * alpha + _dot(vt, p)
            out.append(m_new)
        return tuple(out)

    buf_a, buf_b = (sa_ref, ma_ref), (sb_ref, mb_ref)

    def body(j, ms):
        scores(2 * j + 1, buf_b)
        ms = softmax_pv(2 * j, buf_a, ms)
        scores(2 * j + 2, buf_a)
        return softmax_pv(2 * j + 1, buf_b, ms)

    scores(0, buf_a)
    n_pairs = (n_kt - 1) // 2
    ms = lax.fori_loop(0, n_pairs, body, tuple(jnp.full((1, tq), NEG_BIG, F32) for _ in range(heads)))
    even = n_kt - 1 > 2 * n_pairs

    @pl.when(even)
    def _():
        scores(2 * n_pairs + 1, buf_b)
        softmax_pv(2 * n_pairs + 1, buf_b, softmax_pv(2 * n_pairs, buf_a, ms))

    @pl.when(jnp.logical_not(even))
    def _():
        softmax_pv(2 * n_pairs, buf_a, ms)

    for h in range(heads):
        a = acc_ref[h]
        o = a[:HEAD_DIM_A] / a[HEAD_DIM_A:HEAD_DIM_A + 1]
        o_ref[:, h * HEAD_DIM_A:(h + 1) * HEAD_DIM_A] = o.T.astype(o_ref.dtype)


def _attention(p_main, k_bf, vt_bf, mask_t, *, tq, tk, causal, kvb):
    b, lp, _ = k_bf.shape
    nq = mask_t.shape[2] // tq
    group = N_HEADS_A // N_KV_A
    heads = kvb * group
    gw = heads * HEAD_DIM_A
    return pl.pallas_call(
        functools.partial(_attn_kernel, tq=tq, tk=tk, n_kt_total=lp // tk, causal=causal, group=group,
                          kvb=kvb),
        grid=(b, nq, N_KV_A // kvb),
        in_specs=[pl.BlockSpec((tq, gw), lambda bi, i, g: (bi * nq + i, g)),
                  pl.BlockSpec((None, lp, kvb * HEAD_DIM_A), lambda bi, i, g: (bi, 0, g)),
                  pl.BlockSpec((kvb * VT_ROWS, lp), lambda bi, i, g: (g, bi)),
                  pl.BlockSpec((None, lp, tq), lambda bi, i, g: (bi, 0, i))],
        out_specs=pl.BlockSpec((tq, gw), lambda bi, i, g: (bi * nq + i, g)),
        out_shape=jax.ShapeDtypeStruct((b * nq * tq, N_HEADS_A * HEAD_DIM_A), BF16),
        scratch_shapes=[pltpu.VMEM((heads, VT_ROWS, tq), F32),
                        pltpu.VMEM((heads, tk, tq), F32), pltpu.VMEM((heads, tk, tq), F32),
                        pltpu.VMEM((heads, 1, tq), F32), pltpu.VMEM((heads, 1, tq), F32)],
        compiler_params=_cparams("parallel", "parallel", "arbitrary"),
        name="attention",
    )(p_main, k_bf, vt_bf, mask_t)


def _retention_tables(c):
    log_gamma = jnp.log1p(-(2.0 ** (-5.0 - jnp.arange(N_HEADS_R, dtype=F32))))
    idx = jnp.arange(c, dtype=F32)
    diff = idx[:, None] - idx[None, :]
    decay = jnp.where(diff[None] >= 0,
                      jnp.exp(jnp.maximum(diff, 0.0)[None] * log_gamma[:, None, None]), 0.0)
    cross = jnp.exp((idx + 1.0)[None, :] * log_gamma[:, None])
    kdec = jnp.exp((c - 1.0 - idx)[None, :] * log_gamma[:, None])
    full = jnp.exp(c * log_gamma)
    bc = lambda a, w: jnp.broadcast_to(a[..., None], a.shape + (w,))
    return decay, bc(cross, KEY_DIM_R), bc(kdec, KEY_DIM_R), bc(full[:, None], VAL_DIM_R)


def _retention_kernel(q_ref, k_ref, v_ref, g_ref, dec_ref, qsc_ref, ksc_ref, gc_ref, s0_ref,
                      o_ref, st_ref):
    @pl.when(pl.program_id(1) == 0)
    def _():
        st_ref[...] = s0_ref[...]

    for h in range(N_HEADS_R):
        ks = slice(h * KEY_DIM_R, (h + 1) * KEY_DIM_R)
        vs = slice(h * VAL_DIM_R, (h + 1) * VAL_DIM_R)
        q = q_ref[:, ks]
        k = k_ref[:, ks]
        v = v_ref[:, vs]
        st = st_ref[h]
        inner = _dot_nt(q, k) * dec_ref[h]
        qd = (q.astype(F32) * qsc_ref[h]).astype(BF16)
        o = _dot(inner.astype(BF16), v) + _dot(qd, st.astype(BF16))
        kdt = (k.astype(F32) * ksc_ref[h]).T.astype(BF16)
        st_ref[h] = st * gc_ref[h] + _dot(kdt, v)
        o = o * lax.rsqrt(jnp.mean(o * o, axis=-1, keepdims=True) + EPS)
        o_ref[:, vs] = (o * g_ref[:, vs].astype(F32)).astype(o_ref.dtype)


def _retention(p_main, cols, state0, c):
    b = state0.shape[0]
    nc = p_main.shape[0] // (b * c)
    kw = N_HEADS_R * KEY_DIM_R
    vw = N_HEADS_R * VAL_DIM_R
    dec, qsc, ksc, gcs = _retention_tables(c)
    qb, kb, vb, gb = cols
    const3 = lambda bi, ci: (0, 0, 0)
    st_spec = pl.BlockSpec((None, N_HEADS_R, KEY_DIM_R, VAL_DIM_R), lambda bi, ci: (bi, 0, 0, 0))
    return pl.pallas_call(
        _retention_kernel,
        grid=(b, nc),
        in_specs=[pl.BlockSpec((c, kw), lambda bi, ci: (bi * nc + ci, qb)),
                  pl.BlockSpec((c, kw), lambda bi, ci: (bi * nc + ci, kb)),
                  pl.BlockSpec((c, vw), lambda bi, ci: (bi * nc + ci, vb)),
                  pl.BlockSpec((c, vw), lambda bi, ci: (bi * nc + ci, gb)),
                  pl.BlockSpec(dec.shape, const3), pl.BlockSpec(qsc.shape, const3),
                  pl.BlockSpec(ksc.shape, const3), pl.BlockSpec(gcs.shape, const3),
                  st_spec],
        out_specs=[pl.BlockSpec((c, vw), lambda bi, ci: (bi * nc + ci, 0)), st_spec],
        out_shape=[jax.ShapeDtypeStruct((b * nc * c, vw), BF16),
                   jax.ShapeDtypeStruct(state0.shape, F32)],
        compiler_params=_cparams("parallel", "arbitrary"),
        name="retention",
    )(p_main, p_main, p_main, p_main, dec, qsc, ksc, gcs, state0)


def _merge_kernel(oa_ref, ob_ref, wa_ref, wb_ref, sa_ref, sb_ref, o_ref):
    for b in range(o_ref.shape[1] // MXU_COLS):
        sl = slice(b * MXU_COLS, (b + 1) * MXU_COLS)
        m = (_dot(oa_ref[...], wa_ref[:, sl]) * sa_ref[:, sl].astype(F32)
             + _dot(ob_ref[...], wb_ref[:, sl]) * sb_ref[:, sl].astype(F32))
        o_ref[:, sl] = m.astype(o_ref.dtype)


def _merge(o_a, o_b, w_pa, w_pb, p_main, ga_off, gb_off, tm, tn):
    n, wa = o_a.shape
    wb = o_b.shape[1]
    d = w_pa.shape[1]
    return pl.pallas_call(
        _merge_kernel,
        grid=(n // tm, d // tn),
        in_specs=[pl.BlockSpec((tm, wa), lambda i, j: (i, 0)),
                  pl.BlockSpec((tm, wb), lambda i, j: (i, 0)),
                  pl.BlockSpec((wa, tn), lambda i, j: (0, j)),
                  pl.BlockSpec((wb, tn), lambda i, j: (0, j)),
                  pl.BlockSpec((tm, tn), lambda i, j: (i, ga_off // tn + j)),
                  pl.BlockSpec((tm, tn), lambda i, j: (i, gb_off // tn + j))],
        out_specs=pl.BlockSpec((tm, tn), lambda i, j: (i, j)),
        out_shape=jax.ShapeDtypeStruct((n, d), BF16),
        compiler_params=_cparams("parallel", "arbitrary"),
        name="merge",
    )(o_a, o_b, w_pa, w_pb, p_main, p_main)


def _out_proj_kernel(x_ref, m_ref, w_ref, x1_ref):
    x1_ref[...] = x_ref[...] + _dot(m_ref[...], w_ref[...])


def _out_proj(x, merged, w_o, tm):
    n, d = x.shape
    row = lambda i: (i, 0)
    return pl.pallas_call(
        _out_proj_kernel,
        grid=(n // tm,),
        in_specs=[pl.BlockSpec((tm, d), row), pl.BlockSpec((tm, d), row),
                  pl.BlockSpec((d, d), lambda i: (0, 0))],
        out_specs=pl.BlockSpec((tm, d), row),
        out_shape=jax.ShapeDtypeStruct((n, d), F32),
        compiler_params=_cparams("parallel"),
        name="out_proj",
    )(x, merged, w_o)


def _ffn_kernel(x1_ref, g2_ref, wg_ref, wu_ref, wd_ref, g_ref, y_ref, h_ref, *, final_norm):
    c = pl.program_id(1)

    @pl.when(c == 0)
    def _():
        x1 = x1_ref[...]
        y_ref[...] = x1
        h_ref[...] = _rmsnorm_rows(x1, g2_ref[...]).astype(h_ref.dtype)

    acts = []
    for b in range(wg_ref.shape[1] // MXU_COLS):
        sl = slice(b * MXU_COLS, (b + 1) * MXU_COLS)
        a = _dot(h_ref[...], wg_ref[:, sl])
        u = _dot(h_ref[...], wu_ref[:, sl])
        acts.append((a / (1.0 + jnp.exp(-a)) * u).astype(BF16))
    upd = _dot(acts[0], wd_ref[0:MXU_COLS, :])
    for b in range(1, len(acts)):
        upd += _dot(acts[b], wd_ref[b * MXU_COLS:(b + 1) * MXU_COLS, :])
    y_ref[...] += upd

    if final_norm:
        @pl.when(c == pl.num_programs(1) - 1)
        def _():
            y_ref[...] = _rmsnorm_rows(y_ref[...], g_ref[...])


def _ffn(x1, g2, wg, wu, wd, gf, tm, tc, final_norm):
    n, d = x1.shape
    f = wg.shape[1]
    row = lambda i, c: (i, 0)
    vec = pl.BlockSpec((1, d), lambda i, c: (0, 0))
    return pl.pallas_call(
        functools.partial(_ffn_kernel, final_norm=final_norm),
        grid=(n // tm, f // tc),
        in_specs=[pl.BlockSpec((tm, d), row, pipeline_mode=pl.Buffered(1)), vec,
                  pl.BlockSpec((d, tc), lambda i, c: (0, c)),
                  pl.BlockSpec((d, tc), lambda i, c: (0, c)),
                  pl.BlockSpec((tc, d), lambda i, c: (c, 0)), vec],
        out_specs=pl.BlockSpec((tm, d), row),
        out_shape=jax.ShapeDtypeStruct((n, d), F32),
        scratch_shapes=[pltpu.VMEM((tm, d), BF16)],
        compiler_params=_cparams("parallel", "arbitrary"),
        name="ffn",
    )(x1, g2.reshape(1, d), wg, wu, wd, gf.reshape(1, d))


class _MainLayout:
    def __init__(self, d_model):
        wa = N_HEADS_A * HEAD_DIM_A
        wb = N_HEADS_R * VAL_DIM_R
        kr = N_HEADS_R * KEY_DIM_R
        qi = N_HEADS_IDX * HEAD_DIM_IDX
        order = [("qa", wa, "rope128", HEAD_DIM_A ** -0.5 * LOG2E), ("vr", wb, "plain", 1.0),
                 ("gr", wb, "silu", 1.0), ("ga", d_model, "sigmoid", 1.0), ("gb", d_model, "sigmoid", 1.0),
                 ("qi", qi, "rope64", 1.0), ("qr", kr, "rope128", 1.0),
                 ("kr", kr, "rope128", KEY_DIM_R ** -0.5)]
        self.off, self.width = {}, {}
        self.order = order
        o = 0
        for name, w, _, _ in order:
            self.off[name], self.width[name] = o, w
            o += w
        self.total = o

    def groups(self, tn):
        out = []
        for name, w, mode, scale in self.order:
            assert self.off[name] % tn == 0 and w % tn == 0
            out.append((self.off[name] // tn, (self.off[name] + w) // tn, mode, scale))
        return tuple(out)

    def block(self, name, width=None):
        width = width or self.width[name]
        assert self.off[name] % width == 0
        return self.off[name] // width


def _split_w_in(w, d_model):
    wa = N_HEADS_A * HEAD_DIM_A
    nk = N_KV_A * HEAD_DIM_A
    wb = N_HEADS_R * VAL_DIM_R
    kr = N_HEADS_R * KEY_DIM_R
    names = ("qa", "ka", "va", "qi", "ki", "wi", "qr", "kr", "vr", "gr", "ga", "gb")
    widths = (wa, nk, nk, N_HEADS_IDX * HEAD_DIM_IDX, HEAD_DIM_IDX, N_HEADS_IDX, kr, kr, wb, wb,
              d_model, d_model)
    parts, o = {}, 0
    for name, wd in zip(names, widths):
        parts[name] = w[:, o:o + wd]
        o += wd
    assert o == w.shape[1]
    return parts


def _tile(n, pref):
    t = min(n, pref)
    assert n % t == 0
    return t


def _dense_pre(x2, pos_rows, lw, lay):
    n = x2.shape[0]
    rope = _rope_tables(pos_rows)
    h, *kv = _proj_kv(x2, lw["g1"], lw["w_kv"], rope, _tile(n, 512))
    tn = 1024
    p_main = _proj_main(h, lw["w_main"], rope, lay.groups(tn), _tile(n, 1024), tn)
    return p_main, kv


def _dense_post(x2, o_a, o_b, p_main, lw, lay, gf, final_norm):
    n = x2.shape[0]
    tm = _tile(n, 1024)
    merged = _merge(o_a, o_b, lw["w_pa"], lw["w_pb"], p_main, lay.off["ga"], lay.off["gb"], tm, 512)
    x1 = _out_proj(x2, merged, lw["w_o"], _tile(n, 512))
    return _ffn(x1, lw["g2"], lw["wg"], lw["wu"], lw["wd"], gf, tm, 512, final_norm)


def _prompt_layer(xp, lw, lay, gf, final_norm):
    b, t, d = xp.shape
    x2 = xp.reshape(b * t, d)
    p_main, (ka, va, ki, wt, ka_bf, vt_bf, kia, kib) = _dense_pre(x2, jnp.arange(t, dtype=jnp.int32), lw, lay)
    top_k = min(TOPK_MAX, t // 4)
    tq, ck = _tile(t, 256), _tile(t, 512)
    mask_t = _index_mask(p_main, lay.block("qi"), wt, kia.reshape(b, t, LANES), kib.reshape(b, t, LANES),
                         tq=_tile(t, 512), ck=ck, l_valid=t, pos_base=0, causal=True, top_k=top_k)
    nk = N_KV_A * HEAD_DIM_A
    o_a = _attention(p_main, ka_bf.reshape(b, t, nk), vt_bf, mask_t, tq=tq, tk=ck, causal=True, kvb=2)
    state0 = jnp.zeros((b, N_HEADS_R, KEY_DIM_R, VAL_DIM_R), F32)
    cols = (lay.block("qr"), lay.block("kr"), lay.block("vr"), lay.block("gr"))
    o_b, st = _retention(p_main, cols, state0, _tile(t, 256))
    y = _dense_post(x2, o_a, o_b, p_main, lw, lay, gf, final_norm)
    return (y.reshape(b, t, d), ka.reshape(b, t, N_KV_A, HEAD_DIM_A), va.reshape(b, t, N_KV_A, HEAD_DIM_A),
            ki.reshape(b, t, HEAD_DIM_IDX), st)


def _sample_layer(xs, cache_k, cache_v, cache_i, state, lw, lay, gf, final_norm):
    b, t, d = xs.shape
    past = cache_k.shape[1]
    nk = N_KV_A * HEAD_DIM_A
    x2 = xs.reshape(b * t, d)
    pos = jnp.tile(past + jnp.arange(t, dtype=jnp.int32), b)
    p_main, (ka, va, ki, wt, ka_bf, vt_bf, kia, kib) = _dense_pre(x2, pos, lw, lay)

    tq = LANES
    l_valid = past + t
    tk = 3 * LANES
    lp = -(-l_valid // tk) * tk
    pq = jnp.pad(p_main.reshape(b, t, -1), ((0, 0), (0, tq - t), (0, 0)), mode="edge").reshape(b * tq, -1)
    wtq = jnp.pad(wt.reshape(N_HEADS_IDX, b, t), ((0, 0), (0, 0), (0, tq - t)),
                  mode="edge").reshape(N_HEADS_IDX, b * tq)
    kpad = ((0, 0), (0, lp - l_valid), (0, 0))
    ci = cache_i.astype(BF16)
    zi = jnp.zeros_like(ci)
    kia_all = jnp.pad(jnp.concatenate([jnp.concatenate([ci, zi], -1), kia.reshape(b, t, LANES)], 1), kpad)
    kib_all = jnp.pad(jnp.concatenate([jnp.concatenate([zi, ci], -1), kib.reshape(b, t, LANES)], 1), kpad)
    k_all = jnp.pad(jnp.concatenate([cache_k.reshape(b, past, nk).astype(BF16), ka_bf.reshape(b, t, nk)], 1), kpad)
    vt_new = vt_bf.reshape(N_KV_A, VT_ROWS, b, t)
    vt_cache = cache_v.reshape(b, past, N_KV_A, HEAD_DIM_A).transpose(2, 3, 0, 1).astype(BF16)
    ones_rows = jnp.zeros((N_KV_A, VT_ROWS - HEAD_DIM_A, b, past), BF16).at[:, 0].set(1.0)
    vt_all = jnp.concatenate([jnp.concatenate([vt_cache, ones_rows], 1), vt_new], 3)
    vt_all = jnp.pad(vt_all, ((0, 0), (0, 0), (0, 0), (0, lp - l_valid))).reshape(N_KV_A * VT_ROWS, b * lp)

    top_k = min(TOPK_MAX, l_valid // 4)
    mask_t = _index_mask(pq, lay.block("qi"), wtq, kia_all, kib_all, tq=tq, ck=tk, l_valid=l_valid,
                         pos_base=past, causal=False, top_k=top_k)
    o_a = _attention(pq, k_all, vt_all, mask_t, tq=tq, tk=tk, causal=False, kvb=1)
    o_a = o_a.reshape(b, tq, -1)[:, :t].reshape(b * t, -1)
    cols = (lay.block("qr"), lay.block("kr"), lay.block("vr"), lay.block("gr"))
    o_b, st = _retention(p_main, cols, state.astype(F32), t)
    y = _dense_post(x2, o_a, o_b, p_main, lw, lay, gf, final_norm)
    return (y.reshape(b, t, d), ka.reshape(b, t, N_KV_A, HEAD_DIM_A), va.reshape(b, t, N_KV_A, HEAD_DIM_A),
            ki.reshape(b, t, HEAD_DIM_IDX), st)


def kernel(x_prompt, x_sample, cache_k, cache_v, cache_idx_k, state_ret, norm1_g, w_in, w_pa, w_pb, w_o,
           norm2_g, w_ffn_gate, w_ffn_up, w_ffn_down, norm_f_g):
    assert HEAD_DIM_A == LANES and HEAD_DIM_IDX * 2 == LANES and N_HEADS_IDX % 2 == 0
    depth, d_model = norm1_g.shape
    lay = _MainLayout(d_model)
    xp, xs = x_prompt, x_sample
    outs = [[] for _ in range(8)]
    for l in range(depth):
        parts = _split_w_in(w_in[l], d_model)
        pad = jnp.zeros((d_model, LANES - HEAD_DIM_IDX - N_HEADS_IDX), F32)
        lw = {
            "g1": norm1_g[l], "g2": norm2_g[l],
            "w_main": jnp.concatenate([parts[name] for name, _, _, _ in lay.order], axis=1).astype(BF16),
            "w_kv": jnp.concatenate([parts["ka"], parts["va"], parts["ki"], parts["wi"], pad], axis=1).astype(BF16),
            "w_pa": w_pa[l].astype(BF16), "w_pb": w_pb[l].astype(BF16), "w_o": w_o[l].astype(BF16),
            "wg": w_ffn_gate[l].astype(BF16), "wu": w_ffn_up[l].astype(BF16), "wd": w_ffn_down[l].astype(BF16),
        }
        last = l == depth - 1
        xp, kp, vp, ip, sp = _prompt_layer(xp, lw, lay, norm_f_g, last)
        xs, ks, vs, isl, ss = _sample_layer(xs, cache_k[l], cache_v[l], cache_idx_k[l], state_ret[l],
                                            lw, lay, norm_f_g, last)
        for lst, val in zip(outs, (kp, vp, ip, sp, ks, vs, isl, ss)):
            lst.append(val)
    stacked = [jnp.stack(o) for o in outs]
    return (xp, xs, stacked[0], stacked[1], stacked[2], stacked[3].astype(x_prompt.dtype),
            stacked[4], stacked[5], stacked[6], stacked[7].astype(state_ret.dtype))
```
